```python
import math
import jax, jax.numpy as jnp
from jax import lax
import numpy as np

D_MODEL = 1024
BATCH = 4
SEQ = 8192
DEPTH = 1
DEC_BATCH = 32
DEC_SEQ = 16
PAST_LEN = 2048

CHUNK = 64
Q_BLOCK = 128
MLA_HEADS = 8
QK_NOPE = 64
QK_ROPE = 32
V_HEAD = 64
Q_RANK = 384
KV_RANK = 256
ROPE_THETA = 10000.0
MLA_OUT = MLA_HEADS * V_HEAD
ATTN_SCALE = 1.0 / math.sqrt(QK_NOPE + QK_ROPE)
SG_CHUNK = 128
SG_GROUPS = 4
SG_WIDTH = 512
SG_GROUP_CH = SG_WIDTH // SG_GROUPS
SPLIT_POINTS = (Q_RANK,
                Q_RANK + KV_RANK,
                Q_RANK + KV_RANK + QK_ROPE,
                Q_RANK + KV_RANK + QK_ROPE + SG_WIDTH,
                Q_RANK + KV_RANK + QK_ROPE + 2 * SG_WIDTH,
                Q_RANK + KV_RANK + QK_ROPE + 2 * SG_WIDTH + D_MODEL)
IN_COLS = Q_RANK + KV_RANK + QK_ROPE + 2 * SG_WIDTH + 2 * D_MODEL
N_EXPERTS = 32
TOP_K = 4
D_EXPERT = D_MODEL
SWIGLU_LIMIT = 7.0
SWIGLU_ALPHA = 1.702
MOE_BLOCK = 256
EPS = 1e-6

kernel_name = 'chunk_causal_mla_sgmlp_moe_adaln_step'


def rmsnorm(x, g):
    xf = x.astype(jnp.float32)
    y = xf * lax.rsqrt(jnp.mean(xf * xf, axis=-1, keepdims=True) + EPS)
    return (y * g.astype(jnp.float32)).astype(x.dtype)


def layernorm(x, g, b):
    xf = x.astype(jnp.float32)
    mu = jnp.mean(xf, axis=-1, keepdims=True)
    var = jnp.mean(jnp.square(xf - mu), axis=-1, keepdims=True)
    y = (xf - mu) * lax.rsqrt(var + EPS)
    return (y * g.astype(jnp.float32) + b.astype(jnp.float32)).astype(x.dtype)


def rope_tables(pos):
    inv = ROPE_THETA ** (-jnp.arange(0, QK_ROPE, 2, dtype=jnp.float32) / QK_ROPE)
    ang = pos.astype(jnp.float32)[:, None] * inv[None, :]
    return jnp.cos(ang), jnp.sin(ang)


def apply_rope(x, cos, sin):
    xf = x.astype(jnp.float32)
    x1, x2 = jnp.split(xf, 2, axis=-1)
    return jnp.concatenate([x1 * cos - x2 * sin, x1 * sin + x2 * cos], axis=-1).astype(x.dtype)


def mla_block(q_nope, q_rope, q_pos, k_nope, v, k_rope, k_chunk):
    s = (jnp.einsum('bqhn,bkhn->bhqk', q_nope, k_nope, preferred_element_type=jnp.float32)
         + jnp.einsum('bqhp,bkp->bhqk', q_rope, k_rope, preferred_element_type=jnp.float32)) * ATTN_SCALE
    visible = k_chunk[None, :] <= (q_pos // CHUNK)[:, None]
    s = jnp.where(visible[None, None], s, -jnp.inf)
    p = jax.nn.softmax(s, axis=-1).astype(v.dtype)
    return jnp.einsum('bhqk,bkhv->bqhv', p, v)


def mla_attend(q_nope, q_rope, q_pos, k_nope, v, k_rope, k_pos):
    B, Lq = q_nope.shape[:2]
    k_chunk = k_pos // CHUNK
    if Lq <= Q_BLOCK:
        return mla_block(q_nope, q_rope, q_pos, k_nope, v, k_rope, k_chunk)
    nb = Lq // Q_BLOCK

    def split(t):
        return jnp.moveaxis(t.reshape((B, nb, Q_BLOCK) + t.shape[2:]), 1, 0)

    out = lax.map(lambda a: mla_block(a[0], a[1], a[2], k_nope, v, k_rope, k_chunk),
                  (split(q_nope), split(q_rope), q_pos.reshape(nb, Q_BLOCK)))
    return jnp.moveaxis(out, 0, 1).reshape(B, Lq, MLA_HEADS, V_HEAD)


def spatial_gate(u, v, w_s, b_s):
    B, L, _ = v.shape
    n = -(-L // SG_CHUNK)
    pad = n * SG_CHUNK - L
    vp = jnp.pad(v, ((0, 0), (0, pad), (0, 0))).reshape(B, n, SG_CHUNK, SG_GROUPS, SG_GROUP_CH)
    tril = jnp.tril(jnp.ones((SG_CHUNK, SG_CHUNK), dtype=bool))
    w = jnp.where(tril[None], w_s, 0)
    mixed = jnp.einsum('gts,bnsgc->bntgc', w, vp) + jnp.transpose(b_s)[None, None, :, :, None]
    mixed = mixed.reshape(B, n * SG_CHUNK, SG_WIDTH)[:, :L]
    return u * mixed


def clamped_swiglu(gu):
    g, lin = jnp.split(gu, 2, axis=-1)
    g = jnp.minimum(g, SWIGLU_LIMIT)
    lin = jnp.clip(lin, -SWIGLU_LIMIT, SWIGLU_LIMIT)
    return g * jax.nn.sigmoid(SWIGLU_ALPHA * g) * (lin + 1)


def moe_ffn(h, router_w, router_b, w_gu, b_gu, w_dn, b_dn):
    T, D = h.shape
    logits = jnp.dot(h, router_w, preferred_element_type=jnp.float32) + router_b.astype(jnp.float32)
    top_val, top_idx = lax.top_k(logits, TOP_K)
    gate = jax.nn.softmax(top_val, axis=-1).astype(h.dtype)
    A = T * TOP_K
    flat_e = top_idx.reshape(A).astype(jnp.int32)
    flat_tok = jnp.repeat(jnp.arange(T, dtype=jnp.int32), TOP_K)
    flat_w = gate.reshape(A)
    order = jnp.argsort(flat_e)
    se = flat_e[order]
    counts = jnp.bincount(flat_e, length=N_EXPERTS).astype(jnp.int32)
    padded = (counts + MOE_BLOCK - 1) // MOE_BLOCK * MOE_BLOCK
    start = jnp.cumsum(counts) - counts
    pend = jnp.cumsum(padded)
    pstart = pend - padded
    dest = pstart[se] + jnp.arange(A, dtype=jnp.int32) - start[se]
    n_blocks = -(-A // MOE_BLOCK) + N_EXPERTS
    P = n_blocks * MOE_BLOCK
    slot_tok = jnp.full((P,), T, jnp.int32).at[dest].set(flat_tok[order])
    slot_w = jnp.zeros((P,), h.dtype).at[dest].set(flat_w[order])
    block_e = jnp.minimum(jnp.searchsorted(pend, jnp.arange(n_blocks, dtype=jnp.int32) * MOE_BLOCK,
                                           side='right'), N_EXPERTS - 1)
    h_pad = jnp.concatenate([h, jnp.zeros((1, D), h.dtype)], axis=0)

    def expert_block(args):
        tok, wt, e = args
        xb = h_pad[tok]
        gu = xb @ w_gu[e] + b_gu[e]
        y = clamped_swiglu(gu) @ w_dn[e] + b_dn[e]
        return y * wt[:, None]

    ys = lax.map(expert_block, (slot_tok.reshape(n_blocks, MOE_BLOCK),
                                slot_w.reshape(n_blocks, MOE_BLOCK), block_e))
    out = jnp.zeros((T + 1, D), h.dtype).at[slot_tok].add(ys.reshape(P, D))
    return out[:T]


def token_mixer(h, pos, past_ckv, past_krope, lw):
    B, L, _ = h.shape
    z = h @ lw['w_in']
    cq, ckv, kr, u, v, ga, gb = jnp.split(z, SPLIT_POINTS, axis=-1)
    cos, sin = rope_tables(pos)
    cq = rmsnorm(cq, lw['q_norm_g'])
    q = jnp.einsum('blr,rhd->blhd', cq, lw['w_uq'])
    q_nope = q[..., :QK_NOPE]
    q_rope = apply_rope(q[..., QK_NOPE:], cos[:, None, :], sin[:, None, :])
    ckv = rmsnorm(ckv, lw['kv_norm_g'])
    krope = apply_rope(kr, cos, sin)
    if past_ckv is None:
        ckv_all, krope_all, k_pos = ckv, krope, pos
    else:
        ckv_all = jnp.concatenate([past_ckv, ckv], axis=1)
        krope_all = jnp.concatenate([past_krope, krope], axis=1)
        k_pos = jnp.arange(past_ckv.shape[1] + L, dtype=jnp.int32)
    k_nope = jnp.einsum('bkr,rhn->bkhn', ckv_all, lw['w_uk'])
    v_all = jnp.einsum('bkr,rhv->bkhv', ckv_all, lw['w_uv'])
    o = mla_attend(q_nope, q_rope, pos, k_nope, v_all, krope_all, k_pos).reshape(B, L, MLA_OUT)
    y_a = o @ lw['w_pa']
    v_n = layernorm(v, lw['sg_norm_g'], lw['sg_norm_b'])
    y_b = spatial_gate(u, v_n, lw['w_spatial'], lw['b_spatial']) @ lw['w_pb']
    m = jax.nn.sigmoid(ga) * y_a + jax.nn.sigmoid(gb) * y_b
    return m @ lw['w_o'], ckv, krope, v_n


def trunk_layer(x, c, pos, past_ckv, past_krope, lw):
    mod = jax.nn.silu(c) @ lw['ada_w'] + lw['ada_b']
    sh_a, sc_a, g_a, sh_m, sc_m, g_m = [t[:, None, :] for t in jnp.split(mod, 6, axis=-1)]
    h = rmsnorm(x, lw['norm_mix_g']) * (1 + sc_a) + sh_a
    mix, ckv, krope, v_rows = token_mixer(h, pos, past_ckv, past_krope, lw)
    x = x + g_a * mix
    h = rmsnorm(x, lw['norm_ffn_g']) * (1 + sc_m) + sh_m
    B, L, D = h.shape
    f = moe_ffn(h.reshape(B * L, D), lw['router_w'], lw['router_b'],
                lw['w_gu'], lw['b_gu'], lw['w_dn'], lw['b_dn']).reshape(B, L, D)
    x = x + g_m * f
    return x, ckv, krope, v_rows


def setup_inputs(seed: int = 0) -> dict:
    key = jax.random.key(seed)
    ks = jax.random.split(key, 32)

    def nrm(k, shape, scale):
        return jax.random.normal(k, shape, jnp.float32) * scale

    L = DEPTH
    return {
        'x_prompt': nrm(ks[0], (BATCH, SEQ, D_MODEL), 1.0),
        'x_sample': nrm(ks[1], (DEC_BATCH, DEC_SEQ, D_MODEL), 1.0),
        'cache_ckv': nrm(ks[2], (L, DEC_BATCH, PAST_LEN, KV_RANK), 1.0),
        'cache_krope': nrm(ks[3], (L, DEC_BATCH, PAST_LEN, QK_ROPE), 1.0),
        'c_prompt': nrm(ks[4], (BATCH, D_MODEL), 1.0),
        'c_sample': nrm(ks[5], (DEC_BATCH, D_MODEL), 1.0),
        'ada_w': nrm(ks[6], (L, D_MODEL, 6 * D_MODEL), 0.5 * D_MODEL ** -0.5),
        'ada_b': nrm(ks[7], (L, 6 * D_MODEL), 0.02),
        'norm_mix_g': 1.0 + nrm(ks[8], (L, D_MODEL), 0.05),
        'w_in': nrm(ks[9], (L, D_MODEL, IN_COLS), D_MODEL ** -0.5),
        'q_norm_g': 1.0 + nrm(ks[10], (L, Q_RANK), 0.05),
        'w_uq': nrm(ks[11], (L, Q_RANK, MLA_HEADS, QK_NOPE + QK_ROPE), Q_RANK ** -0.5),
        'kv_norm_g': 1.0 + nrm(ks[12], (L, KV_RANK), 0.05),
        'w_uk': nrm(ks[13], (L, KV_RANK, MLA_HEADS, QK_NOPE), KV_RANK ** -0.5),
        'w_uv': nrm(ks[14], (L, KV_RANK, MLA_HEADS, V_HEAD), KV_RANK ** -0.5),
        'w_pa': nrm(ks[15], (L, MLA_OUT, D_MODEL), MLA_OUT ** -0.5),
        'sg_norm_g': 1.0 + nrm(ks[16], (L, SG_WIDTH), 0.05),
        'sg_norm_b': nrm(ks[17], (L, SG_WIDTH), 0.02),
        'w_spatial': nrm(ks[18], (L, SG_GROUPS, SG_CHUNK, SG_CHUNK), SG_CHUNK ** -0.5),
        'b_spatial': 1.0 + nrm(ks[19], (L, SG_GROUPS, SG_CHUNK), 0.1),
        'w_pb': nrm(ks[20], (L, SG_WIDTH, D_MODEL), SG_WIDTH ** -0.5),
        'w_o': nrm(ks[21], (L, D_MODEL, D_MODEL), D_MODEL ** -0.5),
        'norm_ffn_g': 1.0 + nrm(ks[22], (L, D_MODEL), 0.05),
        'router_w': nrm(ks[23], (L, D_MODEL, N_EXPERTS), D_MODEL ** -0.5),
        'router_b': nrm(ks[24], (L, N_EXPERTS), 0.01),
        'w_gu': nrm(ks[25], (L, N_EXPERTS, D_MODEL, 2 * D_EXPERT), D_MODEL ** -0.5),
        'b_gu': nrm(ks[26], (L, N_EXPERTS, 2 * D_EXPERT), 0.02),
        'w_dn': nrm(ks[27], (L, N_EXPERTS, D_EXPERT, D_MODEL), D_EXPERT ** -0.5),
        'b_dn': nrm(ks[28], (L, N_EXPERTS, D_MODEL), 0.02),
        'final_g': 1.0 + nrm(ks[29], (D_MODEL,), 0.05),
    }


def reference(x_prompt, x_sample, cache_ckv, cache_krope, c_prompt, c_sample,
              ada_w, ada_b, norm_mix_g, w_in, q_norm_g, w_uq, kv_norm_g, w_uk, w_uv, w_pa,
              sg_norm_g, sg_norm_b, w_spatial, b_spatial, w_pb, w_o, norm_ffn_g,
              router_w, router_b, w_gu, b_gu, w_dn, b_dn, final_g):
    pos_p = jnp.arange(x_prompt.shape[1], dtype=jnp.int32)
    past = cache_ckv.shape[2]
    pos_s = past + jnp.arange(x_sample.shape[1], dtype=jnp.int32)
    xp, xs = x_prompt, x_sample
    ckv_p, kr_p, ckv_s, kr_s, v_s = [], [], [], [], []
    for l in range(DEPTH):
        lw = dict(ada_w=ada_w[l], ada_b=ada_b[l], norm_mix_g=norm_mix_g[l], w_in=w_in[l],
                  q_norm_g=q_norm_g[l], w_uq=w_uq[l], kv_norm_g=kv_norm_g[l], w_uk=w_uk[l],
                  w_uv=w_uv[l], w_pa=w_pa[l], sg_norm_g=sg_norm_g[l], sg_norm_b=sg_norm_b[l],
                  w_spatial=w_spatial[l], b_spatial=b_spatial[l], w_pb=w_pb[l], w_o=w_o[l],
                  norm_ffn_g=norm_ffn_g[l], router_w=router_w[l], router_b=router_b[l],
                  w_gu=w_gu[l], b_gu=b_gu[l], w_dn=w_dn[l], b_dn=b_dn[l])
        xp, a_ckv, a_kr, _ = trunk_layer(xp, c_prompt, pos_p, None, None, lw)
        ckv_p.append(a_ckv)
        kr_p.append(a_kr)
        xs, b_ckv, b_kr, b_v = trunk_layer(xs, c_sample, pos_s, cache_ckv[l], cache_krope[l], lw)
        ckv_s.append(b_ckv)
        kr_s.append(b_kr)
        v_s.append(b_v)
    y_prompt = rmsnorm(xp, final_g)
    y_sample = rmsnorm(xs, final_g)
    return (y_prompt, y_sample, jnp.stack(ckv_p), jnp.stack(kr_p),
            jnp.stack(ckv_s), jnp.stack(kr_s), jnp.stack(v_s))
```

```python
import functools
import math

import jax
import jax.numpy as jnp
from jax import lax
from jax.experimental import pallas as pl
from jax.experimental.pallas import tpu as pltpu

F32 = jnp.float32
BF16 = jnp.bfloat16

LANES = 128
VMEM_LIMIT_BYTES = 56 * 1024 * 1024

CHUNK = 64
CHUNK_SHIFT = 6
MLA_HEADS = 8
QK_NOPE = 64
QK_ROPE = 32
V_HEAD = 64
V_HEAD_SHIFT = 6
Q_RANK = 384
KV_RANK = 256
ROPE_THETA = 10000.0
ATTN_SCALE = 1.0 / math.sqrt(QK_NOPE + QK_ROPE)
SG_CHUNK = 128
SG_GROUPS = 4
SG_WIDTH = 512
N_EXPERTS = 32
TOP_K = 4
SWIGLU_LIMIT = 7.0
SWIGLU_ALPHA = 1.702
EPS = 1e-6

HEAD_SLOT = LANES
QK_WIDTH = MLA_HEADS * HEAD_SLOT
V_WIDTH = MLA_HEADS * V_HEAD
MOE_ROWS = 256
ROW_TILE = 256
SAMPLE_TILE = 512
ATTN_TILE = 512
ROUTER_PAD = LANES
NEG_BIG = -1e30

_C_CQ = 0
_C_CKV = _C_CQ + Q_RANK
_C_U = _C_CKV + KV_RANK
_C_V = _C_U + SG_WIDTH
_C_GA = _C_V + SG_WIDTH


def _params(sem):
    return pltpu.CompilerParams(dimension_semantics=sem, vmem_limit_bytes=VMEM_LIMIT_BYTES)


def _dot(a, b):
    return jnp.dot(a, b, preferred_element_type=F32)


def _dot_nt(a, b):
    return lax.dot_general(a, b, (((1,), (1,)), ((), ())), preferred_element_type=F32)


def _rms(x, g):
    return x * lax.rsqrt(jnp.mean(x * x, axis=-1, keepdims=True) + EPS) * g


def _adaln_kernel(c_ref, w_ref, b_ref, o_ref):
    c = c_ref[...]
    s = (c * jax.nn.sigmoid(c)).astype(BF16)
    o_ref[...] = _dot(s, w_ref[...].astype(BF16)) + b_ref[...]


def _adaln(c_all, ada_w, ada_b):
    bp, d = c_all.shape
    n = ada_w.shape[1]
    return pl.pallas_call(
        _adaln_kernel,
        grid=(n // d,),
        in_specs=[pl.BlockSpec((bp, d), lambda j: (0, 0)),
                  pl.BlockSpec((d, d), lambda j: (0, j)),
                  pl.BlockSpec((1, d), lambda j: (0, j))],
        out_specs=pl.BlockSpec((bp, d), lambda j: (0, j)),
        out_shape=jax.ShapeDtypeStruct((bp, n), F32),
        compiler_params=_params(("arbitrary",)),
        name="adaln",
    )(c_all, ada_w, ada_b.reshape(1, n))


def _inproj_kernel(x_ref, sh_ref, sc_ref, gmix_ref, cc_ref, ss_ref, win_ref, gq_ref, gkv_ref,
                   wq_ref, wqs_ref, wk_ref, wv_ref, sgg_ref, sgb_ref,
                   q_ref, k_ref, v_ref, ckv_ref, kr_ref, u_ref, vn_ref, ga_ref, gb_ref, *, d_model):
    x = x_ref[...]
    h = (_rms(x, gmix_ref[...]) * (1.0 + sc_ref[...]) + sh_ref[...]).astype(BF16)

    def proj(lo, width):
        return _dot(h, win_ref[:, lo:lo + width])

    cc = cc_ref[...]
    ss = ss_ref[...]
    c_gb = _C_GA + d_model
    c_kra = c_gb + d_model
    c_krb = c_kra + LANES

    cqn = _rms(proj(_C_CQ, Q_RANK), gq_ref[...]).astype(BF16)
    qa = _dot(cqn, wq_ref[...])
    qb = _dot(cqn, wqs_ref[...])
    for hd in range(MLA_HEADS):
        sl = slice(hd * HEAD_SLOT, (hd + 1) * HEAD_SLOT)
        q_ref[:, sl] = (qa[:, sl] * cc + qb[:, sl] * ss).astype(BF16)

    ckvn = _rms(proj(_C_CKV, KV_RANK), gkv_ref[...])
    ckv_ref[...] = ckvn
    ckvb = ckvn.astype(BF16)
    krs = proj(c_kra, LANES) * cc + proj(c_krb, LANES) * ss
    kr_ref[...] = krs[:, :QK_ROPE]
    kn = _dot(ckvb, wk_ref[...])
    for hd in range(MLA_HEADS):
        sl = slice(hd * HEAD_SLOT, (hd + 1) * HEAD_SLOT)
        k_ref[:, sl] = (kn[:, sl] + krs).astype(BF16)
    v_ref[...] = _dot(ckvb, wv_ref[...]).astype(BF16)

    u_ref[...] = proj(_C_U, SG_WIDTH).astype(u_ref.dtype)
    vv = proj(_C_V, SG_WIDTH)
    mu = jnp.mean(vv, axis=-1, keepdims=True)
    vc = vv - mu
    var = jnp.mean(vc * vc, axis=-1, keepdims=True)
    vn_ref[...] = (vc * lax.rsqrt(var + EPS) * sgg_ref[...] + sgb_ref[...]).astype(vn_ref.dtype)
    ga_ref[...] = proj(_C_GA, d_model).astype(BF16)
    gb_ref[...] = proj(c_gb, d_model).astype(BF16)


def _mod_spec(per_row, tm, d, tiles_per_batch):
    if per_row:
        return pl.BlockSpec((tm, d), lambda i: (i, 0))
    return pl.BlockSpec((None, 1, d), lambda i: (i // tiles_per_batch, 0, 0))


def _const_spec(shape):
    nd = len(shape)
    return pl.BlockSpec(shape, lambda i: (0,) * nd)


def _inproj(x2d, shift, scale, cc, ss, wts, *, tm, per_row, tiles_per_batch, vn_dtype):
    t, d = x2d.shape
    n_tab = cc.shape[0] // tm
    row = lambda w: pl.BlockSpec((tm, w), lambda i: (i, 0))
    tab = pl.BlockSpec((tm, LANES), lambda i: (i % n_tab, 0))
    mod = _mod_spec(per_row, tm, d, tiles_per_batch)
    consts = [wts["w_in_r"], wts["gq"], wts["gkv"], wts["wq"], wts["wqs"], wts["wk"], wts["wv"],
              wts["sgg"], wts["sgb"]]
    out_shapes = [jax.ShapeDtypeStruct((t, QK_WIDTH), BF16), jax.ShapeDtypeStruct((t, QK_WIDTH), BF16),
                  jax.ShapeDtypeStruct((t, V_WIDTH), BF16), jax.ShapeDtypeStruct((t, KV_RANK), F32),
                  jax.ShapeDtypeStruct((t, QK_ROPE), F32), jax.ShapeDtypeStruct((t, SG_WIDTH), BF16),
                  jax.ShapeDtypeStruct((t, SG_WIDTH), vn_dtype), jax.ShapeDtypeStruct((t, d), BF16),
                  jax.ShapeDtypeStruct((t, d), BF16)]
    return pl.pallas_call(
        functools.partial(_inproj_kernel, d_model=d),
        grid=(t // tm,),
        in_specs=[row(d), mod, mod, _const_spec((1, d)), tab, tab] + [_const_spec(c.shape) for c in consts],
        out_specs=[row(s.shape[1]) for s in out_shapes],
        out_shape=out_shapes,
        compiler_params=_params(("arbitrary",)),
        name="inproj",
    )(x2d, shift, scale, wts["gmix"], cc, ss, *consts)


def _attn_kernel(qi_ref, kj_ref, last_ref, q_ref, k_ref, v_ref, o_ref, m_sc, l_sc, acc_sc, *, tq, tk):
    s_id = pl.program_id(1)
    qi = qi_ref[s_id]
    kj = kj_ref[s_id]

    @pl.when(kj == 0)
    def _():
        m_sc[...] = jnp.full(m_sc.shape, -jnp.inf, F32)
        l_sc[...] = jnp.zeros(l_sc.shape, F32)
        acc_sc[...] = jnp.zeros(acc_sc.shape, F32)

    row = lax.broadcasted_iota(jnp.int32, (tq, tk), 0) + qi * tq
    col = lax.broadcasted_iota(jnp.int32, (tq, tk), 1) + kj * tk
    visible = (col >> CHUNK_SHIFT) <= (row >> CHUNK_SHIFT)
    for hd in range(MLA_HEADS):
        sl = slice(hd * HEAD_SLOT, (hd + 1) * HEAD_SLOT)
        s = _dot_nt(q_ref[:, sl], k_ref[:, sl])
        s = jnp.where(visible, s, -jnp.inf)
        m_old = m_sc[hd]
        m_new = jnp.maximum(m_old, jnp.max(s, axis=-1, keepdims=True))
        alpha = jnp.exp(m_old - m_new)
        p = jnp.exp(s - m_new)
        l_sc[hd] = alpha * l_sc[hd] + jnp.sum(p, axis=-1, keepdims=True)
        vsl = slice((hd // 2) * LANES, (hd // 2 + 1) * LANES)
        acc_sc[hd] = alpha * acc_sc[hd] + _dot(p.astype(BF16), v_ref[:, vsl])
        m_sc[hd] = m_new

    @pl.when(last_ref[s_id] == 1)
    def _():
        lane = lax.broadcasted_iota(jnp.int32, (tq, LANES), 1)
        for pr in range(MLA_HEADS // 2):
            even = acc_sc[2 * pr] / l_sc[2 * pr]
            odd = acc_sc[2 * pr + 1] / l_sc[2 * pr + 1]
            o_ref[:, pr * LANES:(pr + 1) * LANES] = jnp.where(lane < V_HEAD, even, odd).astype(BF16)


def _attn_prompt(q, k, v, *, tq, tk):
    b, l, _ = q.shape
    nq = l // tq
    qi_l, kj_l, last_l = [], [], []
    for i in range(nq):
        n_kv = ((i + 1) * tq - 1) // tk + 1
        for j in range(n_kv):
            qi_l.append(i)
            kj_l.append(j)
            last_l.append(1 if j == n_kv - 1 else 0)
    steps = len(qi_l)
    grid_spec = pltpu.PrefetchScalarGridSpec(
        num_scalar_prefetch=3,
        grid=(b, steps),
        in_specs=[pl.BlockSpec((None, tq, QK_WIDTH), lambda bi, s, qi, kj, la: (bi, qi[s], 0)),
                  pl.BlockSpec((None, tk, QK_WIDTH), lambda bi, s, qi, kj, la: (bi, kj[s], 0)),
                  pl.BlockSpec((None, tk, V_WIDTH), lambda bi, s, qi, kj, la: (bi, kj[s], 0))],
        out_specs=pl.BlockSpec((None, tq, V_WIDTH), lambda bi, s, qi, kj, la: (bi, qi[s], 0)),
        scratch_shapes=[pltpu.VMEM((MLA_HEADS, tq, 1), F32), pltpu.VMEM((MLA_HEADS, tq, 1), F32),
                        pltpu.VMEM((MLA_HEADS, tq, LANES), F32)],
    )
    return pl.pallas_call(
        functools.partial(_attn_kernel, tq=tq, tk=tk),
        grid_spec=grid_spec,
        out_shape=jax.ShapeDtypeStruct((b, l, V_WIDTH), BF16),
        compiler_params=_params(("arbitrary", "arbitrary")),
        name="attn_prompt",
    )(jnp.asarray(qi_l, jnp.int32), jnp.asarray(kj_l, jnp.int32), jnp.asarray(last_l, jnp.int32), q, k, v)


def _attn_sample_kernel(q_ref, pckv_ref, pkr_ref, nckv_ref, nkr_ref, mabs_ref, wv_ref, o_ref, *, ls, past):
    hl = MLA_HEADS * ls
    qcat = jnp.concatenate(
        [_dot(q_ref[:, hd * HEAD_SLOT:(hd + 1) * HEAD_SLOT], mabs_ref[hd]) for hd in range(MLA_HEADS)],
        axis=0).astype(BF16)
    q_abs = qcat[:, :KV_RANK]
    q_rope = qcat[:, KV_RANK:]
    pckv = pckv_ref[...].astype(BF16)
    nckv = nckv_ref[...].astype(BF16)

    def pad_lanes(kr):
        return jnp.concatenate([kr, jnp.zeros((kr.shape[0], LANES - QK_ROPE), kr.dtype)], axis=1).astype(BF16)

    s_past = _dot_nt(q_abs, pckv) + _dot_nt(q_rope, pad_lanes(pkr_ref[...]))
    s_new = _dot_nt(q_abs, nckv) + _dot_nt(q_rope, pad_lanes(nkr_ref[...]))

    qpos_1 = lax.broadcasted_iota(jnp.int32, (ls, 1), 0) + past
    qchunk = jnp.concatenate([qpos_1] * MLA_HEADS, axis=0) >> CHUNK_SHIFT
    kchunk_past = lax.broadcasted_iota(jnp.int32, (hl, past), 1) >> CHUNK_SHIFT
    kchunk_new = (lax.broadcasted_iota(jnp.int32, (hl, ls), 1) + past) >> CHUNK_SHIFT
    s_past = jnp.where(kchunk_past <= qchunk, s_past, -jnp.inf)
    s_new = jnp.where(kchunk_new <= qchunk, s_new, -jnp.inf)

    m = jnp.maximum(jnp.max(s_past, axis=-1, keepdims=True), jnp.max(s_new, axis=-1, keepdims=True))
    p_past = jnp.exp(s_past - m)
    p_new = jnp.exp(s_new - m)
    denom = jnp.sum(p_past, axis=-1, keepdims=True) + jnp.sum(p_new, axis=-1, keepdims=True)
    olat = (_dot(p_past.astype(BF16), pckv) + _dot(p_new.astype(BF16), nckv)) / denom
    ofull = _dot(olat.astype(BF16), wv_ref[...])
    col_head = lax.broadcasted_iota(jnp.int32, (ls, V_WIDTH), 1) >> V_HEAD_SHIFT
    out = jnp.zeros((ls, V_WIDTH), F32)
    for hd in range(MLA_HEADS):
        out = out + jnp.where(col_head == hd, ofull[hd * ls:(hd + 1) * ls], 0.0)
    o_ref[...] = out.astype(BF16)


def _attn_sample(q, past_ckv, past_kr, new_ckv, new_kr, mabs, wv):
    b, ls, _ = q.shape
    past = past_ckv.shape[1]
    blk = lambda n, w: pl.BlockSpec((None, n, w), lambda i: (i, 0, 0))
    return pl.pallas_call(
        functools.partial(_attn_sample_kernel, ls=ls, past=past),
        grid=(b,),
        in_specs=[blk(ls, QK_WIDTH), blk(past, KV_RANK), blk(past, QK_ROPE), blk(ls, KV_RANK), blk(ls, QK_ROPE),
                  _const_spec(mabs.shape), _const_spec(wv.shape)],
        out_specs=blk(ls, V_WIDTH),
        out_shape=jax.ShapeDtypeStruct((b, ls, V_WIDTH), BF16),
        compiler_params=_params(("arbitrary",)),
        name="attn_sample",
    )(q, past_ckv, past_kr, new_ckv, new_kr, mabs, wv)


def _merge_kernel(o_ref, u_ref, vn_ref, ga_ref, gb_ref, x_ref, gate_ref, mix_ref, bias_ref,
                  wpa_ref, wpb_ref, wo_ref, x1_ref, sg_sc, *, chunk):
    tm = x_ref.shape[0]
    gw = SG_WIDTH // SG_GROUPS
    for c in range(tm // chunk):
        rows = slice(c * chunk, (c + 1) * chunk)
        for g in range(SG_GROUPS):
            cols = slice(g * gw, (g + 1) * gw)
            mixed = _dot(mix_ref[g], vn_ref[rows, cols].astype(BF16)) + bias_ref[:, cols]
            sg_sc[rows, cols] = (u_ref[rows, cols].astype(F32) * mixed).astype(BF16)
    ya = _dot(o_ref[...], wpa_ref[...])
    yb = _dot(sg_sc[...], wpb_ref[...])
    m = jax.nn.sigmoid(ga_ref[...].astype(F32)) * ya + jax.nn.sigmoid(gb_ref[...].astype(F32)) * yb
    x1_ref[...] = x_ref[...] + gate_ref[...] * _dot(m.astype(BF16), wo_ref[...])


def _merge(o, u, vn, ga, gb, x2d, gate, mixw, bias, wts, *, tm, chunk, per_row, tiles_per_batch):
    t, d = x2d.shape
    row = lambda w: pl.BlockSpec((tm, w), lambda i: (i, 0))
    consts = [mixw, bias, wts["w_pa"], wts["w_pb"], wts["w_o"]]
    return pl.pallas_call(
        functools.partial(_merge_kernel, chunk=chunk),
        grid=(t // tm,),
        in_specs=[row(V_WIDTH), row(SG_WIDTH), row(SG_WIDTH), row(d), row(d), row(d),
                  _mod_spec(per_row, tm, d, tiles_per_batch)] + [_const_spec(c.shape) for c in consts],
        out_specs=row(d),
        out_shape=jax.ShapeDtypeStruct((t, d), F32),
        scratch_shapes=[pltpu.VMEM((tm, SG_WIDTH), BF16)],
        compiler_params=_params(("arbitrary",)),
        name="merge",
    )(o, u, vn, ga, gb, x2d, gate, *consts)


def _router_kernel(xp_ref, shp_ref, scp_ref, xs_ref, shs_ref, scs_ref, g_ref, whi_ref, wlo_ref, rb_ref,
                   h2_ref, idx_ref, gate_ref, *, n_prompt_tiles):
    i = pl.program_id(0)
    out_refs = (g_ref, whi_ref, wlo_ref, rb_ref, h2_ref, idx_ref, gate_ref)

    @pl.when(i < n_prompt_tiles)
    def _():
        _route_rows(xp_ref, shp_ref, scp_ref, *out_refs)

    @pl.when(i >= n_prompt_tiles)
    def _():
        _route_rows(xs_ref, shs_ref, scs_ref, *out_refs)


def _route_rows(x1_ref, sh_ref, sc_ref, g_ref, whi_ref, wlo_ref, rb_ref, h2_ref, idx_ref, gate_ref):
    h2 = _rms(x1_ref[...], g_ref[...]) * (1.0 + sc_ref[...]) + sh_ref[...]
    h2_ref[...] = h2
    hi = h2.astype(BF16)
    lo = (h2 - hi.astype(F32)).astype(BF16)
    logits = _dot(hi, whi_ref[...]) + _dot(lo, whi_ref[...]) + _dot(hi, wlo_ref[...]) + rb_ref[...]
    lane = lax.broadcasted_iota(jnp.int32, logits.shape, 1)
    vals, idxs = [], []
    work = logits
    for _ in range(TOP_K):
        mx = jnp.max(work, axis=-1, keepdims=True)
        ix = jnp.min(jnp.where(work == mx, lane, ROUTER_PAD), axis=-1, keepdims=True)
        vals.append(mx)
        idxs.append(ix)
        work = jnp.where(lane == ix, -jnp.inf, work)
    es = [jnp.exp(v - vals[0]) for v in vals]
    tot = es[0]
    for e in es[1:]:
        tot = tot + e
    idx_w = jnp.zeros(logits.shape, jnp.int32)
    gate_w = jnp.zeros(logits.shape, F32)
    for j in range(TOP_K):
        idx_w = jnp.where(lane == j, idxs[j], idx_w)
        gate_w = jnp.where(lane == j, es[j] / tot, gate_w)
    idx_ref[...] = idx_w[:, :TOP_K]
    gate_ref[...] = gate_w[:, :TOP_K]


def _router(x1p, shift_p, scale_p, x1s, shift_s, scale_s, wts, *, tm, tiles_per_batch):
    tp, d = x1p.shape
    ts = x1s.shape[0]
    n_p, n_s = tp // tm, ts // tm
    t_all = tp + ts
    row = lambda w: pl.BlockSpec((tm, w), lambda i: (i, 0))
    p_row = pl.BlockSpec((tm, d), lambda i: (jnp.minimum(i, n_p - 1), 0))
    p_mod = pl.BlockSpec((None, 1, d), lambda i: (jnp.minimum(i, n_p - 1) // tiles_per_batch, 0, 0))
    s_row = pl.BlockSpec((tm, d), lambda i: (jnp.maximum(i - n_p, 0), 0))
    consts = [wts["gffn"], wts["rw_hi"], wts["rw_lo"], wts["rb"]]
    return pl.pallas_call(
        functools.partial(_router_kernel, n_prompt_tiles=n_p),
        grid=(n_p + n_s,),
        in_specs=[p_row, p_mod, p_mod, s_row, s_row, s_row] + [_const_spec(c.shape) for c in consts],
        out_specs=[row(d), row(TOP_K), row(TOP_K)],
        out_shape=[jax.ShapeDtypeStruct((t_all, d), F32), jax.ShapeDtypeStruct((t_all, TOP_K), jnp.int32),
                   jax.ShapeDtypeStruct((t_all, TOP_K), F32)],
        compiler_params=_params(("arbitrary",)),
        name="router",
    )(x1p, shift_p, scale_p, x1s, shift_s, scale_s, *consts)


def _row_gather_start(idx_smem, slot, src_hbm, dst_vmem, sem, n_rows):
    def body(r, carry):
        pltpu.make_async_copy(src_hbm.at[pl.ds(idx_smem[slot, r], 1)], dst_vmem.at[pl.ds(r, 1)], sem).start()
        return carry
    lax.fori_loop(0, n_rows, body, 0, unroll=8)


def _row_gather_wait(src_hbm, dst_vmem, sem, n_rows):
    pltpu.make_async_copy(src_hbm.at[pl.ds(0, n_rows)], dst_vmem, sem).wait()


def _gather_pipeline(i, n_steps, idx_hbm, idx_smem, isem, src_hbm, buf, gsem, n_rows):
    def idx_copy(blk, slot):
        return pltpu.make_async_copy(idx_hbm.at[blk], idx_smem.at[slot], isem.at[slot])

    @pl.when(i == 0)
    def _():
        idx_copy(0, 0).start()
        idx_copy(0, 0).wait()
        _row_gather_start(idx_smem, 0, src_hbm, buf.at[0], gsem.at[0], n_rows)

        @pl.when(n_steps > 1)
        def _():
            idx_copy(1, 1).start()

    nxt = (i + 1) % 2

    @pl.when(i + 1 < n_steps)
    def _():
        idx_copy(i + 1, nxt).wait()
        _row_gather_start(idx_smem, nxt, src_hbm, buf.at[nxt], gsem.at[nxt], n_rows)

    @pl.when(i + 2 < n_steps)
    def _():
        idx_copy(i + 2, i % 2).start()

    _row_gather_wait(src_hbm, buf.at[i % 2], gsem.at[i % 2], n_rows)


def _moe_kernel(be_ref, nused_ref, tok_ref, h2_ref, wgu_ref, bgu_ref, wdn_ref, bdn_ref, y_ref,
                xbuf, idx_smem, isem, gsem, wgu_bf, wdn_bf, *, d_model):
    i = pl.program_id(0)
    n_used = nused_ref[0]

    @pl.when(i < n_used)
    def _():
        _gather_pipeline(i, n_used, tok_ref, idx_smem, isem, h2_ref, xbuf, gsem, MOE_ROWS)
        prev = be_ref[jnp.maximum(i - 1, 0)]

        @pl.when((i == 0) | (be_ref[i] != prev))
        def _():
            wgu_bf[...] = wgu_ref[...].astype(BF16)
            wdn_bf[...] = wdn_ref[...].astype(BF16)

        xb = xbuf[i % 2].astype(BF16)
        gu = _dot(xb, wgu_bf[...]) + bgu_ref[...]
        g = jnp.minimum(gu[:, :d_model], SWIGLU_LIMIT)
        lin = jnp.clip(gu[:, d_model:], -SWIGLU_LIMIT, SWIGLU_LIMIT)
        act = g * jax.nn.sigmoid(SWIGLU_ALPHA * g) * (lin + 1.0)
        y_ref[...] = _dot(act.astype(BF16), wdn_bf[...]) + bdn_ref[...]

    @pl.when(i >= n_used)
    def _():
        y_ref[...] = jnp.zeros(y_ref.shape, F32)


def _moe_experts(block_e, n_used, slot_tok2d, h2_all, w_gu, b_gu, w_dn, b_dn):
    n_blocks = slot_tok2d.shape[0]
    e, d, d2 = w_gu.shape
    grid_spec = pltpu.PrefetchScalarGridSpec(
        num_scalar_prefetch=2,
        grid=(n_blocks,),
        in_specs=[pl.BlockSpec(memory_space=pl.ANY),
                  pl.BlockSpec(memory_space=pl.ANY),
                  pl.BlockSpec((None, d, d2), lambda i, be, nu: (be[i], 0, 0)),
                  pl.BlockSpec((None, 1, d2), lambda i, be, nu: (be[i], 0, 0)),
                  pl.BlockSpec((None, d, d), lambda i, be, nu: (be[i], 0, 0)),
                  pl.BlockSpec((None, 1, d), lambda i, be, nu: (be[i], 0, 0))],
        out_specs=pl.BlockSpec((MOE_ROWS, d), lambda i, be, nu: (i, 0)),
        scratch_shapes=[pltpu.VMEM((2, MOE_ROWS, d), F32),
                        pltpu.SMEM((2, MOE_ROWS), jnp.int32),
                        pltpu.SemaphoreType.DMA((2,)),
                        pltpu.SemaphoreType.DMA((2,)),
                        pltpu.VMEM((d, d2), BF16),
                        pltpu.VMEM((d, d), BF16)],
    )
    return pl.pallas_call(
        functools.partial(_moe_kernel, d_model=d),
        grid_spec=grid_spec,
        out_shape=jax.ShapeDtypeStruct((n_blocks * MOE_ROWS, d), F32),
        compiler_params=_params(("arbitrary",)),
        name="moe_experts",
    )(block_e, n_used, slot_tok2d, h2_all, w_gu, b_gu.reshape(e, 1, d2), w_dn, b_dn.reshape(e, 1, d))


def _combine_kernel(pos_ref, ys_ref, x1_ref, gate_ref, gm_ref, gfin_ref, y_ref, ybuf, idx_smem, isem, gsem, *,
                    tm, final_norm):
    i = pl.program_id(0)
    _gather_pipeline(i, pl.num_programs(0), pos_ref, idx_smem, isem, ys_ref, ybuf, gsem, TOP_K * tm)
    gate = gate_ref[...]
    f = jnp.zeros(x1_ref.shape, F32)
    for kk in range(TOP_K):
        f = f + gate[:, kk:kk + 1] * ybuf[i % 2, kk * tm:(kk + 1) * tm, :]
    x2 = x1_ref[...] + gm_ref[...] * f
    y_ref[...] = _rms(x2, gfin_ref[...]) if final_norm else x2


def _combine(pos_tiles, ys, x1, gate, g_m, final_g, *, tm, per_row, tiles_per_batch, final_norm):
    t, d = x1.shape
    row = lambda w: pl.BlockSpec((tm, w), lambda i: (i, 0))
    return pl.pallas_call(
        functools.partial(_combine_kernel, tm=tm, final_norm=final_norm),
        grid=(t // tm,),
        in_specs=[pl.BlockSpec(memory_space=pl.ANY), pl.BlockSpec(memory_space=pl.ANY), row(d), row(TOP_K),
                  _mod_spec(per_row, tm, d, tiles_per_batch), _const_spec((1, d))],
        out_specs=row(d),
        out_shape=jax.ShapeDtypeStruct((t, d), F32),
        scratch_shapes=[pltpu.VMEM((2, TOP_K * tm, d), F32),
                        pltpu.SMEM((2, TOP_K * tm), jnp.int32),
                        pltpu.SemaphoreType.DMA((2,)),
                        pltpu.SemaphoreType.DMA((2,))],
        compiler_params=_params(("arbitrary",)),
        name="combine",
    )(pos_tiles, ys, x1, gate, g_m, final_g)


def _rope_tables(pos):
    inv = ROPE_THETA ** (-jnp.arange(0, QK_ROPE, 2, dtype=F32) / QK_ROPE)
    ang = pos.astype(F32)[:, None] * inv[None, :]
    cos, sin = jnp.cos(ang), jnp.sin(ang)
    n = pos.shape[0]
    cc = jnp.concatenate([cos, cos, jnp.ones((n, LANES - QK_ROPE), F32)], axis=1)
    ss = jnp.concatenate([sin, sin, jnp.zeros((n, LANES - QK_ROPE), F32)], axis=1)
    return cc, ss


def _swap_halves(w):
    half = QK_ROPE // 2
    return jnp.concatenate([-w[..., half:], w[..., :half]], axis=-1)


def _layer_weights(l, w_in, norm_mix_g, q_norm_g, w_uq, kv_norm_g, w_uk, w_uv, w_pa, sg_norm_g, sg_norm_b,
                   w_pb, w_o, norm_ffn_g, router_w, router_b):
    d = w_in.shape[1]
    wi = w_in[l]
    o_kr = Q_RANK + KV_RANK
    o_u = o_kr + QK_ROPE
    o_v = o_u + SG_WIDTH
    o_ga = o_v + SG_WIDTH
    kr = wi[:, o_kr:o_u]
    zpad = jnp.zeros((d, LANES - QK_ROPE), F32)
    w_in_r = jnp.concatenate([wi[:, :o_kr], wi[:, o_u:o_ga], wi[:, o_ga:],
                              kr, zpad, _swap_halves(kr), zpad], axis=1).astype(BF16)
    uq = w_uq[l]
    nope, rope = uq[..., :QK_NOPE], uq[..., QK_NOPE:]
    z32 = jnp.zeros(rope.shape[:2] + (HEAD_SLOT - QK_NOPE - QK_ROPE,), F32)
    wq = jnp.concatenate([rope, nope, z32], axis=-1).reshape(Q_RANK, QK_WIDTH).astype(BF16)
    wqs = jnp.concatenate([_swap_halves(rope), jnp.zeros_like(nope), z32], axis=-1)
    wqs = wqs.reshape(Q_RANK, QK_WIDTH).astype(BF16)
    uk = w_uk[l]
    zk_lo = jnp.zeros(uk.shape[:2] + (QK_ROPE,), F32)
    zk_hi = jnp.zeros(uk.shape[:2] + (HEAD_SLOT - QK_NOPE - QK_ROPE,), F32)
    wk = jnp.concatenate([zk_lo, uk, zk_hi], axis=-1).reshape(KV_RANK, QK_WIDTH).astype(BF16)
    wv = w_uv[l].reshape(KV_RANK, V_WIDTH).astype(BF16)
    ukt = jnp.transpose(uk, (1, 2, 0))
    eye = jnp.broadcast_to(jnp.eye(QK_ROPE, LANES, dtype=F32), (MLA_HEADS, QK_ROPE, LANES))
    top = jnp.concatenate([jnp.zeros((MLA_HEADS, QK_ROPE, KV_RANK), F32), eye], axis=-1)
    mid = jnp.concatenate([ukt, jnp.zeros((MLA_HEADS, QK_NOPE, LANES), F32)], axis=-1)
    bot = jnp.zeros((MLA_HEADS, HEAD_SLOT - QK_NOPE - QK_ROPE, KV_RANK + LANES), F32)
    mabs = jnp.concatenate([top, mid, bot], axis=1).astype(BF16)
    rw = jnp.pad(router_w[l], ((0, 0), (0, ROUTER_PAD - N_EXPERTS)))
    rw_hi = rw.astype(BF16)
    rw_lo = (rw - rw_hi.astype(F32)).astype(BF16)
    rb = jnp.concatenate([router_b[l], jnp.full((ROUTER_PAD - N_EXPERTS,), NEG_BIG, F32)]).reshape(1, ROUTER_PAD)
    return dict(
        w_in_r=w_in_r, gmix=norm_mix_g[l].reshape(1, d), gq=(q_norm_g[l] * ATTN_SCALE).reshape(1, Q_RANK),
        gkv=kv_norm_g[l].reshape(1, KV_RANK), wq=wq, wqs=wqs, wk=wk, wv=wv, mabs=mabs,
        sgg=sg_norm_g[l].reshape(1, SG_WIDTH), sgb=sg_norm_b[l].reshape(1, SG_WIDTH),
        w_pa=w_pa[l].astype(BF16), w_pb=w_pb[l].astype(BF16), w_o=w_o[l].astype(BF16),
        gffn=norm_ffn_g[l].reshape(1, d), rw_hi=rw_hi, rw_lo=rw_lo, rb=rb)


def _spatial_mix_weights(w_s, b_s, seq, n_batch):
    gw = SG_WIDTH // SG_GROUPS
    tril = jnp.tril(jnp.ones((SG_CHUNK, SG_CHUNK), dtype=bool))
    w = jnp.where(tril[None], w_s, 0.0)
    if seq % SG_CHUNK == 0:
        mixw = w
        bias_t = b_s
    else:
        assert seq < SG_CHUNK
        blk = w[:, :seq, :seq]
        eye = jnp.eye(n_batch, dtype=F32)
        mixw = jnp.einsum("ab,gts->gatbs", eye, blk).reshape(SG_GROUPS, n_batch * seq, n_batch * seq)
        bias_t = jnp.tile(b_s[:, :seq], (1, n_batch))
    bias = jnp.repeat(jnp.transpose(bias_t), gw, axis=1)
    return mixw.astype(BF16), bias


def _routing_tables(idx, n_tok):
    a = n_tok * TOP_K
    flat_e = idx.reshape(a)
    order = jnp.argsort(flat_e)
    se = flat_e[order]
    counts = jnp.bincount(flat_e, length=N_EXPERTS).astype(jnp.int32)
    padded = (counts + MOE_ROWS - 1) // MOE_ROWS * MOE_ROWS
    start = jnp.cumsum(counts) - counts
    pend = jnp.cumsum(padded)
    pstart = pend - padded
    dest = pstart[se] + jnp.arange(a, dtype=jnp.int32) - start[se]
    n_blocks = -(-a // MOE_ROWS) + N_EXPERTS
    p = n_blocks * MOE_ROWS
    slot_tok = jnp.zeros((p,), jnp.int32).at[dest].set((order // TOP_K).astype(jnp.int32))
    pos = jnp.zeros((a,), jnp.int32).at[order].set(dest.astype(jnp.int32)).reshape(n_tok, TOP_K)
    block_e = jnp.minimum(jnp.searchsorted(pend, jnp.arange(n_blocks, dtype=jnp.int32) * MOE_ROWS, side="right"),
                          N_EXPERTS - 1).astype(jnp.int32)
    n_used = (pend[-1] // MOE_ROWS).astype(jnp.int32).reshape(1)
    return slot_tok.reshape(n_blocks, MOE_ROWS), block_e, n_used, pos


def _pos_tiles(pos, tm):
    t = pos.shape[0]
    return jnp.transpose(pos.reshape(t // tm, tm, TOP_K), (0, 2, 1)).reshape(t // tm, TOP_K * tm)


def _pick_tile(n, pref):
    t = min(n, pref)
    assert n % t == 0 and t % 8 == 0
    return t


def kernel(x_prompt, x_sample, cache_ckv, cache_krope, c_prompt, c_sample, ada_w, ada_b, norm_mix_g, w_in, q_norm_g, w_uq, kv_norm_g, w_uk, w_uv, w_pa, sg_norm_g, sg_norm_b, w_spatial, b_spatial, w_pb, w_o, norm_ffn_g, router_w, router_b, w_gu, b_gu, w_dn, b_dn, final_g):
    bp, lp, d = x_prompt.shape
    bs, ls, _ = x_sample.shape
    depth = w_in.shape[0]
    past = cache_ckv.shape[2]
    tp, ts = bp * lp, bs * ls
    assert lp % SG_CHUNK == 0 and ls <= SG_CHUNK

    tm_p = _pick_tile(lp, ROW_TILE)
    tm_s = _pick_tile(ts, SAMPLE_TILE)
    t_attn = _pick_tile(lp, ATTN_TILE)
    tpb = lp // tm_p

    cc_p, ss_p = _rope_tables(jnp.arange(lp, dtype=jnp.int32))
    cc_s, ss_s = _rope_tables(past + jnp.arange(ls, dtype=jnp.int32))
    cc_s, ss_s = jnp.tile(cc_s, (bs, 1)), jnp.tile(ss_s, (bs, 1))

    b_all = bp + bs
    b_pad = -(-b_all // 8) * 8
    c_all = jnp.concatenate([c_prompt, c_sample, jnp.zeros((b_pad - b_all, d), F32)], axis=0)

    xp = x_prompt.reshape(tp, d)
    xs = x_sample.reshape(ts, d)
    outs = dict(ckv_p=[], kr_p=[], ckv_s=[], kr_s=[], v_s=[])
    final_g2 = final_g.reshape(1, d)
    for l in range(depth):
        wts = _layer_weights(l, w_in, norm_mix_g, q_norm_g, w_uq, kv_norm_g, w_uk, w_uv, w_pa, sg_norm_g,
                             sg_norm_b, w_pb, w_o, norm_ffn_g, router_w, router_b)
        mod = _adaln(c_all, ada_w[l], ada_b[l])
        mod_p = [mod[:bp, j * d:(j + 1) * d].reshape(bp, 1, d) for j in range(6)]
        mod_s = [jnp.repeat(mod[bp:b_all, j * d:(j + 1) * d], ls, axis=0) for j in range(6)]

        q, k, v, ckv, kr, u, vn, ga, gb = _inproj(xp, mod_p[0], mod_p[1], cc_p, ss_p, wts, tm=tm_p, per_row=False,
                                                  tiles_per_batch=tpb, vn_dtype=BF16)
        o = _attn_prompt(q.reshape(bp, lp, QK_WIDTH), k.reshape(bp, lp, QK_WIDTH), v.reshape(bp, lp, V_WIDTH),
                         tq=t_attn, tk=t_attn).reshape(tp, V_WIDTH)
        mixw, bias = _spatial_mix_weights(w_spatial[l], b_spatial[l], lp, bp)
        x1p = _merge(o, u, vn, ga, gb, xp, mod_p[2], mixw, bias, wts, tm=tm_p, chunk=SG_CHUNK, per_row=False,
                     tiles_per_batch=tpb)
        outs["ckv_p"].append(ckv.reshape(bp, lp, KV_RANK))
        outs["kr_p"].append(kr.reshape(bp, lp, QK_ROPE))

        q, k, v, ckv, kr, u, vn, ga, gb = _inproj(xs, mod_s[0], mod_s[1], cc_s, ss_s, wts, tm=tm_s, per_row=True,
                                                  tiles_per_batch=1, vn_dtype=F32)
        ckv3, kr3 = ckv.reshape(bs, ls, KV_RANK), kr.reshape(bs, ls, QK_ROPE)
        o = _attn_sample(q.reshape(bs, ls, QK_WIDTH), cache_ckv[l], cache_krope[l], ckv3, kr3,
                         wts["mabs"], wts["wv"]).reshape(ts, V_WIDTH)
        mixw, bias = _spatial_mix_weights(w_spatial[l], b_spatial[l], ls, tm_s // ls)
        x1s = _merge(o, u, vn, ga, gb, xs, mod_s[2], mixw, bias, wts, tm=tm_s, chunk=tm_s, per_row=True,
                     tiles_per_batch=1)
        outs["ckv_s"].append(ckv3)
        outs["kr_s"].append(kr3)
        outs["v_s"].append(vn.reshape(bs, ls, SG_WIDTH))

        t_all = tp + ts
        tm_r = _pick_tile(math.gcd(tp, ts), ROW_TILE)
        h2_all, idx, gate = _router(x1p, mod_p[3], mod_p[4], x1s, mod_s[3], mod_s[4], wts, tm=tm_r,
                                    tiles_per_batch=lp // tm_r)
        gate_p, gate_s = gate[:tp], gate[tp:]
        slot_tok, block_e, n_used, pos = _routing_tables(idx, t_all)
        ys = _moe_experts(block_e, n_used, slot_tok, h2_all, w_gu[l], b_gu[l], w_dn[l], b_dn[l])
        last = l == depth - 1
        xp = _combine(_pos_tiles(pos[:tp], tm_r), ys, x1p, gate_p, mod_p[5], final_g2, tm=tm_r, per_row=False,
                      tiles_per_batch=lp // tm_r, final_norm=last)
        xs = _combine(_pos_tiles(pos[tp:], tm_r), ys, x1s, gate_s, mod_s[5], final_g2, tm=tm_r, per_row=True,
                      tiles_per_batch=1, final_norm=last)
    return (xp.reshape(bp, lp, d), xs.reshape(bs, ls, d),
            jnp.stack(outs["ckv_p"]), jnp.stack(outs["kr_p"]),
            jnp.stack(outs["ckv_s"]), jnp.stack(outs["kr_s"]), jnp.stack(outs["v_s"]))
```

```python
import functools
import math

import jax
import jax.numpy as jnp
from jax import lax
from jax.experimental import pallas as pl
from jax.experimental.pallas import tpu as pltpu

F32 = jnp.float32
BF16 = jnp.bfloat16

LANES = 128
VMEM_LIMIT_BYTES = 56 * 1024 * 1024

CHUNK = 64
CHUNK_SHIFT = 6
MLA_HEADS = 8
QK_NOPE = 64
QK_ROPE = 32
V_HEAD = 64
V_HEAD_SHIFT = 6
Q_RANK = 384
KV_RANK = 256
ROPE_THETA = 10000.0
ATTN_SCALE = 1.0 / math.sqrt(QK_NOPE + QK_ROPE)
LOG2_E = math.log2(math.e)
SG_CHUNK = 128
SG_GROUPS = 4
SG_WIDTH = 512
N_EXPERTS = 32
TOP_K = 4
SWIGLU_LIMIT = 7.0
SWIGLU_ALPHA = 1.702
EPS = 1e-6

HEAD_SLOT = LANES
QK_WIDTH = MLA_HEADS * HEAD_SLOT
V_WIDTH = MLA_HEADS * V_HEAD
MOE_ROWS = 256
ROW_TILE = 256
SAMPLE_TILE = 512
ATTN_TILE = 512
ROUTER_PAD = LANES
NEG_BIG = -1e30

_C_CQ = 0
_C_CKV = _C_CQ + Q_RANK
_C_U = _C_CKV + KV_RANK
_C_V = _C_U + SG_WIDTH
_C_GA = _C_V + SG_WIDTH


def _params(sem):
    return pltpu.CompilerParams(dimension_semantics=sem, vmem_limit_bytes=VMEM_LIMIT_BYTES)


def _dot(a, b):
    return jnp.dot(a, b, preferred_element_type=F32)


def _dot_nt(a, b):
    return lax.dot_general(a, b, (((1,), (1,)), ((), ())), preferred_element_type=F32)


def _rms(x, g):
    return x * lax.rsqrt(jnp.mean(x * x, axis=-1, keepdims=True) + EPS) * g


def _adaln_kernel(c_ref, w_ref, b_ref, o_ref):
    c = c_ref[...]
    s = (c * jax.nn.sigmoid(c)).astype(BF16)
    o_ref[...] = _dot(s, w_ref[...].astype(BF16)) + b_ref[...]


def _adaln(c_all, ada_w, ada_b):
    bp, d = c_all.shape
    n = ada_w.shape[1]
    return pl.pallas_call(
        _adaln_kernel,
        grid=(n // d,),
        in_specs=[pl.BlockSpec((bp, d), lambda j: (0, 0)),
                  pl.BlockSpec((d, d), lambda j: (0, j)),
                  pl.BlockSpec((1, d), lambda j: (0, j))],
        out_specs=pl.BlockSpec((bp, d), lambda j: (0, j)),
        out_shape=jax.ShapeDtypeStruct((bp, n), F32),
        compiler_params=_params(("arbitrary",)),
        name="adaln",
    )(c_all, ada_w, ada_b.reshape(1, n))


def _inproj_kernel(x_ref, sh_ref, sc_ref, gmix_ref, cc_ref, ss_ref, win_ref, gq_ref, gkv_ref,
                   wq_ref, wqs_ref, wk_ref, wv_ref, vone_ref, sgg_ref, sgb_ref,
                   q_ref, k_ref, v_ref, ckv_ref, kr_ref, u_ref, vn_ref, ga_ref, gb_ref, *, d_model):
    x = x_ref[...]
    h = (_rms(x, gmix_ref[...]) * (1.0 + sc_ref[...]) + sh_ref[...]).astype(BF16)

    def proj(lo, width):
        return _dot(h, win_ref[:, lo:lo + width])

    cc = cc_ref[...]
    ss = ss_ref[...]
    c_gb = _C_GA + d_model
    c_kra = c_gb + d_model
    c_krb = c_kra + LANES

    cqn = _rms(proj(_C_CQ, Q_RANK), gq_ref[...]).astype(BF16)
    qa = _dot(cqn, wq_ref[...])
    qb = _dot(cqn, wqs_ref[...])
    for hd in range(MLA_HEADS):
        sl = slice(hd * HEAD_SLOT, (hd + 1) * HEAD_SLOT)
        q_ref[:, sl] = (qa[:, sl] * cc + qb[:, sl] * ss).astype(BF16)

    ckvn = _rms(proj(_C_CKV, KV_RANK), gkv_ref[...])
    ckv_ref[...] = ckvn
    ckvb = ckvn.astype(BF16)
    krs = proj(c_kra, LANES) * cc + proj(c_krb, LANES) * ss
    kr_ref[...] = krs[:, :QK_ROPE]
    kn = _dot(ckvb, wk_ref[...])
    for hd in range(MLA_HEADS):
        sl = slice(hd * HEAD_SLOT, (hd + 1) * HEAD_SLOT)
        k_ref[:, sl] = (kn[:, sl] + krs).astype(BF16)
    v_ref[...] = (_dot(ckvb, wv_ref[...]) + vone_ref[...]).astype(BF16)

    u_ref[...] = proj(_C_U, SG_WIDTH).astype(u_ref.dtype)
    vv = proj(_C_V, SG_WIDTH)
    mu = jnp.mean(vv, axis=-1, keepdims=True)
    vc = vv - mu
    var = jnp.mean(vc * vc, axis=-1, keepdims=True)
    vn_ref[...] = (vc * lax.rsqrt(var + EPS) * sgg_ref[...] + sgb_ref[...]).astype(vn_ref.dtype)
    ga_ref[...] = proj(_C_GA, d_model).astype(BF16)
    gb_ref[...] = proj(c_gb, d_model).astype(BF16)


def _mod_spec(per_row, tm, d, tiles_per_batch):
    if per_row:
        return pl.BlockSpec((tm, d), lambda i: (i, 0))
    return pl.BlockSpec((None, 1, d), lambda i: (i // tiles_per_batch, 0, 0))


def _const_spec(shape):
    nd = len(shape)
    return pl.BlockSpec(shape, lambda i: (0,) * nd)


def _inproj(x2d, shift, scale, cc, ss, wts, *, tm, per_row, tiles_per_batch, vn_dtype):
    t, d = x2d.shape
    n_tab = cc.shape[0] // tm
    row = lambda w: pl.BlockSpec((tm, w), lambda i: (i, 0))
    tab = pl.BlockSpec((tm, LANES), lambda i: (i % n_tab, 0))
    mod = _mod_spec(per_row, tm, d, tiles_per_batch)
    consts = [wts["w_in_r"], wts["gq"], wts["gkv"], wts["wq"], wts["wqs"], wts["wk"], wts["wv_slot"],
              wts["vone"], wts["sgg"], wts["sgb"]]
    out_shapes = [jax.ShapeDtypeStruct((t, QK_WIDTH), BF16), jax.ShapeDtypeStruct((t, QK_WIDTH), BF16),
                  jax.ShapeDtypeStruct((t, QK_WIDTH), BF16), jax.ShapeDtypeStruct((t, KV_RANK), F32),
                  jax.ShapeDtypeStruct((t, QK_ROPE), F32), jax.ShapeDtypeStruct((t, SG_WIDTH), BF16),
                  jax.ShapeDtypeStruct((t, SG_WIDTH), vn_dtype), jax.ShapeDtypeStruct((t, d), BF16),
                  jax.ShapeDtypeStruct((t, d), BF16)]
    return pl.pallas_call(
        functools.partial(_inproj_kernel, d_model=d),
        grid=(t // tm,),
        in_specs=[row(d), mod, mod, _const_spec((1, d)), tab, tab] + [_const_spec(c.shape) for c in consts],
        out_specs=[row(s.shape[1]) for s in out_shapes],
        out_shape=out_shapes,
        compiler_params=_params(("arbitrary",)),
        name="inproj",
    )(x2d, shift, scale, wts["gmix"], cc, ss, *consts)


def _attn_kernel(qi_ref, kj_ref, flag_ref, q_ref, k_ref, v_ref, o_ref, m_sc, acc_sc, *, tq, tk):
    s_id = pl.program_id(1)
    qi = qi_ref[s_id]
    kj = kj_ref[s_id]
    flags = flag_ref[s_id]

    @pl.when(kj == 0)
    def _():
        m_sc[...] = jnp.full(m_sc.shape, -jnp.inf, F32)
        acc_sc[...] = jnp.zeros(acc_sc.shape, F32)

    def sweep(bias):
        for hd in range(MLA_HEADS):
            sl = slice(hd * HEAD_SLOT, (hd + 1) * HEAD_SLOT)
            s = _dot_nt(q_ref[:, sl], k_ref[:, sl])
            if bias is not None:
                s = s + bias
            tiles = [s[:, c * LANES:(c + 1) * LANES] for c in range(tk // LANES)]
            m_tile = tiles[0]
            for t in tiles[1:]:
                m_tile = jnp.maximum(m_tile, t)
            m_old = m_sc[hd]
            m_new = jnp.maximum(m_old, jnp.max(m_tile, axis=-1, keepdims=True))
            alpha = jnp.exp2(m_old - m_new)
            p = jnp.concatenate([jnp.exp2(t - m_new).astype(BF16) for t in tiles], axis=1)
            acc_sc[hd] = alpha * acc_sc[hd] + _dot(p, v_ref[:, sl])
            m_sc[hd] = m_new

    @pl.when((flags & 2) == 0)
    def _():
        sweep(None)

    @pl.when((flags & 2) != 0)
    def _():
        row = lax.broadcasted_iota(jnp.int32, (tq, tk), 0) + qi * tq
        col = lax.broadcasted_iota(jnp.int32, (tq, tk), 1) + kj * tk
        sweep(jnp.where((col >> CHUNK_SHIFT) <= (row >> CHUNK_SHIFT), 0.0, -jnp.inf))

    @pl.when((flags & 1) != 0)
    def _():
        lane = lax.broadcasted_iota(jnp.int32, (tq, LANES), 1)
        for pr in range(MLA_HEADS // 2):
            outs = []
            for hd in (2 * pr, 2 * pr + 1):
                acc = acc_sc[hd]
                outs.append(acc / pltpu.roll(acc, V_HEAD, axis=1))
            pair = jnp.where(lane < V_HEAD, outs[0], pltpu.roll(outs[1], V_HEAD, axis=1))
            o_ref[:, pr * LANES:(pr + 1) * LANES] = pair.astype(BF16)


def _attn_prompt(q, k, v, *, tq, tk):
    b, l, _ = q.shape
    nq = l // tq
    qi_l, kj_l, flag_l = [], [], []
    for i in range(nq):
        n_kv = ((i + 1) * tq - 1) // tk + 1
        for j in range(n_kv):
            qi_l.append(i)
            kj_l.append(j)
            masked = ((j + 1) * tk - 1) // CHUNK > (i * tq) // CHUNK
            flag_l.append((1 if j == n_kv - 1 else 0) | (2 if masked else 0))
    steps = len(qi_l)
    grid_spec = pltpu.PrefetchScalarGridSpec(
        num_scalar_prefetch=3,
        grid=(b, steps),
        in_specs=[pl.BlockSpec((None, tq, QK_WIDTH), lambda bi, s, qi, kj, fl: (bi, qi[s], 0)),
                  pl.BlockSpec((None, tk, QK_WIDTH), lambda bi, s, qi, kj, fl: (bi, kj[s], 0)),
                  pl.BlockSpec((None, tk, QK_WIDTH), lambda bi, s, qi, kj, fl: (bi, kj[s], 0))],
        out_specs=pl.BlockSpec((None, tq, V_WIDTH), lambda bi, s, qi, kj, fl: (bi, qi[s], 0)),
        scratch_shapes=[pltpu.VMEM((MLA_HEADS, tq, LANES), F32), pltpu.VMEM((MLA_HEADS, tq, LANES), F32)],
    )
    return pl.pallas_call(
        functools.partial(_attn_kernel, tq=tq, tk=tk),
        grid_spec=grid_spec,
        out_shape=jax.ShapeDtypeStruct((b, l, V_WIDTH), BF16),
        compiler_params=_params(("arbitrary", "arbitrary")),
        name="attn_prompt",
    )(jnp.asarray(qi_l, jnp.int32), jnp.asarray(kj_l, jnp.int32), jnp.asarray(flag_l, jnp.int32), q, k, v)


def _attn_sample_kernel(q_ref, pckv_ref, pkr_ref, nckv_ref, nkr_ref, mabs_ref, wv_ref, o_ref, *, ls, past):
    hl = MLA_HEADS * ls
    qcat = jnp.concatenate(
        [_dot(q_ref[:, hd * HEAD_SLOT:(hd + 1) * HEAD_SLOT], mabs_ref[hd]) for hd in range(MLA_HEADS)],
        axis=0).astype(BF16)
    q_abs = qcat[:, :KV_RANK]
    q_rope = qcat[:, KV_RANK:]
    pckv = pckv_ref[...].astype(BF16)
    nckv = nckv_ref[...].astype(BF16)

    def pad_lanes(kr):
        return jnp.concatenate([kr, jnp.zeros((kr.shape[0], LANES - QK_ROPE), kr.dtype)], axis=1).astype(BF16)

    s_past = _dot_nt(q_abs, pckv) + _dot_nt(q_rope, pad_lanes(pkr_ref[...]))
    s_new = _dot_nt(q_abs, nckv) + _dot_nt(q_rope, pad_lanes(nkr_ref[...]))

    qpos_1 = lax.broadcasted_iota(jnp.int32, (ls, 1), 0) + past
    qchunk = jnp.concatenate([qpos_1] * MLA_HEADS, axis=0) >> CHUNK_SHIFT
    kchunk_past = lax.broadcasted_iota(jnp.int32, (hl, past), 1) >> CHUNK_SHIFT
    kchunk_new = (lax.broadcasted_iota(jnp.int32, (hl, ls), 1) + past) >> CHUNK_SHIFT
    s_past = jnp.where(kchunk_past <= qchunk, s_past, -jnp.inf)
    s_new = jnp.where(kchunk_new <= qchunk, s_new, -jnp.inf)

    m = jnp.maximum(jnp.max(s_past, axis=-1, keepdims=True), jnp.max(s_new, axis=-1, keepdims=True))
    p_past = jnp.exp2(s_past - m)
    p_new = jnp.exp2(s_new - m)
    denom = jnp.sum(p_past, axis=-1, keepdims=True) + jnp.sum(p_new, axis=-1, keepdims=True)
    olat = (_dot(p_past.astype(BF16), pckv) + _dot(p_new.astype(BF16), nckv)) / denom
    ofull = _dot(olat.astype(BF16), wv_ref[...])
    col_head = lax.broadcasted_iota(jnp.int32, (ls, V_WIDTH), 1) >> V_HEAD_SHIFT
    out = jnp.zeros((ls, V_WIDTH), F32)
    for hd in range(MLA_HEADS):
        out = out + jnp.where(col_head == hd, ofull[hd * ls:(hd + 1) * ls], 0.0)
    o_ref[...] = out.astype(BF16)


def _attn_sample(q, past_ckv, past_kr, new_ckv, new_kr, mabs, wv):
    b, ls, _ = q.shape
    past = past_ckv.shape[1]
    blk = lambda n, w: pl.BlockSpec((None, n, w), lambda i: (i, 0, 0))
    return pl.pallas_call(
        functools.partial(_attn_sample_kernel, ls=ls, past=past),
        grid=(b,),
        in_specs=[blk(ls, QK_WIDTH), blk(past, KV_RANK), blk(past, QK_ROPE), blk(ls, KV_RANK), blk(ls, QK_ROPE),
                  _const_spec(mabs.shape), _const_spec(wv.shape)],
        out_specs=blk(ls, V_WIDTH),
        out_shape=jax.ShapeDtypeStruct((b, ls, V_WIDTH), BF16),
        compiler_params=_params(("arbitrary",)),
        name="attn_sample",
    )(q, past_ckv, past_kr, new_ckv, new_kr, mabs, wv)


def _merge_kernel(o_ref, u_ref, vn_ref, ga_ref, gb_ref, x_ref, gate_ref, mix_ref, bias_ref,
                  wpa_ref, wpb_ref, wo_ref, x1_ref, sg_sc, *, chunk):
    tm = x_ref.shape[0]
    gw = SG_WIDTH // SG_GROUPS
    for c in range(tm // chunk):
        rows = slice(c * chunk, (c + 1) * chunk)
        for g in range(SG_GROUPS):
            cols = slice(g * gw, (g + 1) * gw)
            mixed = _dot(mix_ref[g], vn_ref[rows, cols].astype(BF16)) + bias_ref[:, cols]
            sg_sc[rows, cols] = (u_ref[rows, cols].astype(F32) * mixed).astype(BF16)
    ya = _dot(o_ref[...], wpa_ref[...])
    yb = _dot(sg_sc[...], wpb_ref[...])
    m = jax.nn.sigmoid(ga_ref[...].astype(F32)) * ya + jax.nn.sigmoid(gb_ref[...].astype(F32)) * yb
    x1_ref[...] = x_ref[...] + gate_ref[...] * _dot(m.astype(BF16), wo_ref[...])


def _merge(o, u, vn, ga, gb, x2d, gate, mixw, bias, wts, *, tm, chunk, per_row, tiles_per_batch):
    t, d = x2d.shape
    row = lambda w: pl.BlockSpec((tm, w), lambda i: (i, 0))
    consts = [mixw, bias, wts["w_pa"], wts["w_pb"], wts["w_o"]]
    return pl.pallas_call(
        functools.partial(_merge_kernel, chunk=chunk),
        grid=(t // tm,),
        in_specs=[row(V_WIDTH), row(SG_WIDTH), row(SG_WIDTH), row(d), row(d), row(d),
                  _mod_spec(per_row, tm, d, tiles_per_batch)] + [_const_spec(c.shape) for c in consts],
        out_specs=row(d),
        out_shape=jax.ShapeDtypeStruct((t, d), F32),
        scratch_shapes=[pltpu.VMEM((tm, SG_WIDTH), BF16)],
        compiler_params=_params(("arbitrary",)),
        name="merge",
    )(o, u, vn, ga, gb, x2d, gate, *consts)


def _router_kernel(xp_ref, shp_ref, scp_ref, xs_ref, shs_ref, scs_ref, g_ref, whi_ref, wlo_ref, rb_ref,
                   h2_ref, idx_ref, gate_ref, rank_ref, cnt_ref, carry_sc, *, n_prompt_tiles):
    i = pl.program_id(0)
    out_refs = (g_ref, whi_ref, wlo_ref, rb_ref, h2_ref, idx_ref, gate_ref, rank_ref, cnt_ref, carry_sc)

    @pl.when(i == 0)
    def _():
        carry_sc[...] = jnp.zeros(carry_sc.shape, F32)

    @pl.when(i < n_prompt_tiles)
    def _():
        _route_rows(xp_ref, shp_ref, scp_ref, *out_refs)

    @pl.when(i >= n_prompt_tiles)
    def _():
        _route_rows(xs_ref, shs_ref, scs_ref, *out_refs)


def _route_rows(x1_ref, sh_ref, sc_ref, g_ref, whi_ref, wlo_ref, rb_ref, h2_ref, idx_ref, gate_ref, rank_ref,
                cnt_ref, carry_sc):
    h2 = _rms(x1_ref[...], g_ref[...]) * (1.0 + sc_ref[...]) + sh_ref[...]
    h2_ref[...] = h2
    hi = h2.astype(BF16)
    lo = (h2 - hi.astype(F32)).astype(BF16)
    logits = _dot(hi, whi_ref[...]) + _dot(lo, whi_ref[...]) + _dot(hi, wlo_ref[...]) + rb_ref[...]
    lane = lax.broadcasted_iota(jnp.int32, logits.shape, 1)
    vals, idxs = [], []
    work = logits
    for _ in range(TOP_K):
        mx = jnp.max(work, axis=-1, keepdims=True)
        ix = jnp.min(jnp.where(work == mx, lane, ROUTER_PAD), axis=-1, keepdims=True)
        vals.append(mx)
        idxs.append(ix)
        work = jnp.where(lane == ix, -jnp.inf, work)
    es = [jnp.exp(v - vals[0]) for v in vals]
    tot = es[0]
    for e in es[1:]:
        tot = tot + e
    idx_w = jnp.zeros(logits.shape, jnp.int32)
    gate_w = jnp.zeros(logits.shape, F32)
    for j in range(TOP_K):
        idx_w = jnp.where(lane == j, idxs[j], idx_w)
        gate_w = jnp.where(lane == j, es[j] / tot, gate_w)
    idx_ref[...] = idx_w[:, :TOP_K]
    gate_ref[...] = gate_w[:, :TOP_K]

    tm = logits.shape[0]
    onehot = jnp.zeros(logits.shape, F32)
    for j in range(TOP_K):
        onehot = jnp.where(lane == idxs[j], 1.0, onehot)
    earlier = (lax.broadcasted_iota(jnp.int32, (tm, tm), 1) < lax.broadcasted_iota(jnp.int32, (tm, tm), 0))
    within = _dot(jnp.where(earlier, 1.0, 0.0).astype(BF16), onehot.astype(BF16))
    rank_full = within + carry_sc[...]
    rank_w = jnp.zeros(logits.shape, jnp.int32)
    for j in range(TOP_K):
        rj = jnp.sum(jnp.where(lane == idxs[j], rank_full, 0.0), axis=-1, keepdims=True)
        rank_w = jnp.where(lane == j, rj.astype(jnp.int32), rank_w)
    rank_ref[...] = rank_w[:, :TOP_K]
    carry_sc[...] = carry_sc[...] + jnp.sum(onehot, axis=0, keepdims=True)
    cnt_ref[...] = carry_sc[...]


def _router(x1p, shift_p, scale_p, x1s, shift_s, scale_s, wts, *, tm, tiles_per_batch):
    tp, d = x1p.shape
    ts = x1s.shape[0]
    n_p, n_s = tp // tm, ts // tm
    t_all = tp + ts
    row = lambda w: pl.BlockSpec((tm, w), lambda i: (i, 0))
    p_row = pl.BlockSpec((tm, d), lambda i: (jnp.minimum(i, n_p - 1), 0))
    p_mod = pl.BlockSpec((None, 1, d), lambda i: (jnp.minimum(i, n_p - 1) // tiles_per_batch, 0, 0))
    s_row = pl.BlockSpec((tm, d), lambda i: (jnp.maximum(i - n_p, 0), 0))
    consts = [wts["gffn"], wts["rw_hi"], wts["rw_lo"], wts["rb"]]
    return pl.pallas_call(
        functools.partial(_router_kernel, n_prompt_tiles=n_p),
        grid=(n_p + n_s,),
        in_specs=[p_row, p_mod, p_mod, s_row, s_row, s_row] + [_const_spec(c.shape) for c in consts],
        out_specs=[row(d), row(TOP_K), row(TOP_K), row(TOP_K), _const_spec((1, ROUTER_PAD))],
        out_shape=[jax.ShapeDtypeStruct((t_all, d), F32), jax.ShapeDtypeStruct((t_all, TOP_K), jnp.int32),
                   jax.ShapeDtypeStruct((t_all, TOP_K), F32), jax.ShapeDtypeStruct((t_all, TOP_K), jnp.int32),
                   jax.ShapeDtypeStruct((1, ROUTER_PAD), F32)],
        scratch_shapes=[pltpu.VMEM((1, ROUTER_PAD), F32)],
        compiler_params=_params(("arbitrary",)),
        name="router",
    )(x1p, shift_p, scale_p, x1s, shift_s, scale_s, *consts)


def _row_gather_start(idx_smem, slot, src_hbm, dst_vmem, sem, n_rows):
    def body(r, carry):
        pltpu.make_async_copy(src_hbm.at[pl.ds(idx_smem[slot, r], 1)], dst_vmem.at[pl.ds(r, 1)], sem).start()
        return carry
    lax.fori_loop(0, n_rows, body, 0, unroll=8)


def _row_gather_wait(src_hbm, dst_vmem, sem, n_rows):
    pltpu.make_async_copy(src_hbm.at[pl.ds(0, n_rows)], dst_vmem, sem).wait()


def _gather_pipeline(i, n_steps, idx_hbm, idx_smem, isem, src_hbm, buf, gsem, n_rows):
    def idx_copy(blk, slot):
        return pltpu.make_async_copy(idx_hbm.at[blk], idx_smem.at[slot], isem.at[slot])

    @pl.when(i == 0)
    def _():
        idx_copy(0, 0).start()
        idx_copy(0, 0).wait()
        _row_gather_start(idx_smem, 0, src_hbm, buf.at[0], gsem.at[0], n_rows)

        @pl.when(n_steps > 1)
        def _():
            idx_copy(1, 1).start()

    nxt = (i + 1) % 2

    @pl.when(i + 1 < n_steps)
    def _():
        idx_copy(i + 1, nxt).wait()
        _row_gather_start(idx_smem, nxt, src_hbm, buf.at[nxt], gsem.at[nxt], n_rows)

    @pl.when(i + 2 < n_steps)
    def _():
        idx_copy(i + 2, i % 2).start()

    _row_gather_wait(src_hbm, buf.at[i % 2], gsem.at[i % 2], n_rows)


def _dispatch_kernel(pend_ref, nused_ref, dest_ref, h2_ref, xs_ref, zbuf, idx_smem, isem, csem, zsem, *,
                     tm, n_blocks):
    i = pl.program_id(0)
    n = pl.num_programs(0)
    rows = TOP_K * tm

    def idx_copy(blk, slot):
        return pltpu.make_async_copy(dest_ref.at[blk], idx_smem.at[slot], isem.at[slot])

    def zero_copy(block_start):
        start = pl.multiple_of(block_start, MOE_ROWS)
        return pltpu.make_async_copy(zbuf, xs_ref.at[pl.ds(start, MOE_ROWS)], zsem)

    def rows_done():
        return pltpu.make_async_copy(h2_ref.at[pl.ds(0, rows)], xs_ref.at[pl.ds(0, rows)], csem)

    @pl.when(i == 0)
    def _():
        idx_copy(0, 0).start()
        zbuf[...] = jnp.zeros(zbuf.shape, F32)
        n_used = nused_ref[0]

        def last_block(e, carry):
            zero_copy(jnp.maximum(pend_ref[e] - MOE_ROWS, 0)).start()
            return carry
        lax.fori_loop(0, N_EXPERTS, last_block, 0)

        def tail_block(b, carry):
            zero_copy(b * MOE_ROWS).start()
            return carry
        lax.fori_loop(n_used, n_blocks, tail_block, 0)

        def drain(b, carry):
            zero_copy(0).wait()
            return carry
        lax.fori_loop(0, N_EXPERTS + n_blocks - n_used, drain, 0)

    slot = i % 2
    idx_copy(i, slot).wait()

    @pl.when(i + 1 < n)
    def _():
        idx_copy(i + 1, 1 - slot).start()

    def body(r, carry):
        src = h2_ref.at[pl.ds(i * tm + r, 1)]
        for kk in range(TOP_K):
            pltpu.make_async_copy(src, xs_ref.at[pl.ds(idx_smem[slot, kk * tm + r], 1)], csem).start()
        return carry
    lax.fori_loop(0, tm, body, 0, unroll=4)

    @pl.when(i > 0)
    def _():
        rows_done().wait()

    @pl.when(i == n - 1)
    def _():
        rows_done().wait()


def _dispatch(pend, n_used, dest_tiles, h2_all, *, tm, n_blocks):
    t_all, d = h2_all.shape
    grid_spec = pltpu.PrefetchScalarGridSpec(
        num_scalar_prefetch=2,
        grid=(t_all // tm,),
        in_specs=[pl.BlockSpec(memory_space=pl.ANY), pl.BlockSpec(memory_space=pl.ANY)],
        out_specs=pl.BlockSpec(memory_space=pl.ANY),
        scratch_shapes=[pltpu.VMEM((MOE_ROWS, d), F32),
                        pltpu.SMEM((2, TOP_K * tm), jnp.int32),
                        pltpu.SemaphoreType.DMA((2,)),
                        pltpu.SemaphoreType.DMA(()),
                        pltpu.SemaphoreType.DMA(())],
    )
    return pl.pallas_call(
        functools.partial(_dispatch_kernel, tm=tm, n_blocks=n_blocks),
        grid_spec=grid_spec,
        out_shape=jax.ShapeDtypeStruct((n_blocks * MOE_ROWS, d), F32),
        compiler_params=_params(("arbitrary",)),
        name="dispatch",
    )(pend, n_used, dest_tiles, h2_all)


def _moe_kernel(be_ref, nused_ref, xs_ref, wgu_ref, bgu_ref, wdn_ref, bdn_ref, y_ref, wgu_bf, wdn_bf, *, d_model):
    i = pl.program_id(0)
    n_used = nused_ref[0]

    @pl.when(i < n_used)
    def _():
        prev = be_ref[jnp.maximum(i - 1, 0)]

        @pl.when((i == 0) | (be_ref[i] != prev))
        def _():
            wgu_bf[...] = wgu_ref[...].astype(BF16)
            wdn_bf[...] = wdn_ref[...].astype(BF16)

        xb = xs_ref[...].astype(BF16)
        gu = _dot(xb, wgu_bf[...]) + bgu_ref[...]
        g = jnp.minimum(gu[:, :d_model], SWIGLU_LIMIT)
        lin = jnp.clip(gu[:, d_model:], -SWIGLU_LIMIT, SWIGLU_LIMIT)
        act = g * jax.nn.sigmoid(SWIGLU_ALPHA * g) * (lin + 1.0)
        y_ref[...] = _dot(act.astype(BF16), wdn_bf[...]) + bdn_ref[...]

    @pl.when(i >= n_used)
    def _():
        y_ref[...] = jnp.zeros(y_ref.shape, F32)


def _moe_experts(block_e, n_used, xs, w_gu, b_gu, w_dn, b_dn):
    n_blocks = xs.shape[0] // MOE_ROWS
    e, d, d2 = w_gu.shape
    grid_spec = pltpu.PrefetchScalarGridSpec(
        num_scalar_prefetch=2,
        grid=(n_blocks,),
        in_specs=[pl.BlockSpec((MOE_ROWS, d), lambda i, be, nu: (jnp.minimum(i, nu[0] - 1), 0)),
                  pl.BlockSpec((None, d, d2), lambda i, be, nu: (be[i], 0, 0)),
                  pl.BlockSpec((None, 1, d2), lambda i, be, nu: (be[i], 0, 0)),
                  pl.BlockSpec((None, d, d), lambda i, be, nu: (be[i], 0, 0)),
                  pl.BlockSpec((None, 1, d), lambda i, be, nu: (be[i], 0, 0))],
        out_specs=pl.BlockSpec((MOE_ROWS, d), lambda i, be, nu: (i, 0)),
        scratch_shapes=[pltpu.VMEM((d, d2), BF16),
                        pltpu.VMEM((d, d), BF16)],
    )
    return pl.pallas_call(
        functools.partial(_moe_kernel, d_model=d),
        grid_spec=grid_spec,
        out_shape=jax.ShapeDtypeStruct((n_blocks * MOE_ROWS, d), F32),
        compiler_params=_params(("arbitrary",)),
        name="moe_experts",
    )(block_e, n_used, xs, w_gu, b_gu.reshape(e, 1, d2), w_dn, b_dn.reshape(e, 1, d))


def _combine_kernel(pos_ref, ys_ref, x1_ref, gate_ref, gm_ref, gfin_ref, y_ref, ybuf, idx_smem, isem, gsem, *,
                    tm, final_norm):
    i = pl.program_id(0)
    _gather_pipeline(i, pl.num_programs(0), pos_ref, idx_smem, isem, ys_ref, ybuf, gsem, TOP_K * tm)
    gate = gate_ref[...]
    f = jnp.zeros(x1_ref.shape, F32)
    for kk in range(TOP_K):
        f = f + gate[:, kk:kk + 1] * ybuf[i % 2, kk * tm:(kk + 1) * tm, :]
    x2 = x1_ref[...] + gm_ref[...] * f
    y_ref[...] = _rms(x2, gfin_ref[...]) if final_norm else x2


def _combine(pos_tiles, ys, x1, gate, g_m, final_g, *, tm, per_row, tiles_per_batch, final_norm):
    t, d = x1.shape
    row = lambda w: pl.BlockSpec((tm, w), lambda i: (i, 0))
    return pl.pallas_call(
        functools.partial(_combine_kernel, tm=tm, final_norm=final_norm),
        grid=(t // tm,),
        in_specs=[pl.BlockSpec(memory_space=pl.ANY), pl.BlockSpec(memory_space=pl.ANY), row(d), row(TOP_K),
                  _mod_spec(per_row, tm, d, tiles_per_batch), _const_spec((1, d))],
        out_specs=row(d),
        out_shape=jax.ShapeDtypeStruct((t, d), F32),
        scratch_shapes=[pltpu.VMEM((2, TOP_K * tm, d), F32),
                        pltpu.SMEM((2, TOP_K * tm), jnp.int32),
                        pltpu.SemaphoreType.DMA((2,)),
                        pltpu.SemaphoreType.DMA((2,))],
        compiler_params=_params(("arbitrary",)),
        name="combine",
    )(pos_tiles, ys, x1, gate, g_m, final_g)


def _rope_tables(pos):
    inv = ROPE_THETA ** (-jnp.arange(0, QK_ROPE, 2, dtype=F32) / QK_ROPE)
    ang = pos.astype(F32)[:, None] * inv[None, :]
    cos, sin = jnp.cos(ang), jnp.sin(ang)
    n = pos.shape[0]
    cc = jnp.concatenate([cos, cos, jnp.ones((n, LANES - QK_ROPE), F32)], axis=1)
    ss = jnp.concatenate([sin, sin, jnp.zeros((n, LANES - QK_ROPE), F32)], axis=1)
    return cc, ss


def _swap_halves(w):
    half = QK_ROPE // 2
    return jnp.concatenate([-w[..., half:], w[..., :half]], axis=-1)


def _layer_weights(l, w_in, norm_mix_g, q_norm_g, w_uq, kv_norm_g, w_uk, w_uv, w_pa, sg_norm_g, sg_norm_b,
                   w_pb, w_o, norm_ffn_g, router_w, router_b):
    d = w_in.shape[1]
    wi = w_in[l]
    o_kr = Q_RANK + KV_RANK
    o_u = o_kr + QK_ROPE
    o_v = o_u + SG_WIDTH
    o_ga = o_v + SG_WIDTH
    kr = wi[:, o_kr:o_u]
    zpad = jnp.zeros((d, LANES - QK_ROPE), F32)
    w_in_r = jnp.concatenate([wi[:, :o_kr], wi[:, o_u:o_ga], wi[:, o_ga:],
                              kr, zpad, _swap_halves(kr), zpad], axis=1).astype(BF16)
    uq = w_uq[l]
    nope, rope = uq[..., :QK_NOPE], uq[..., QK_NOPE:]
    z32 = jnp.zeros(rope.shape[:2] + (HEAD_SLOT - QK_NOPE - QK_ROPE,), F32)
    wq = jnp.concatenate([rope, nope, z32], axis=-1).reshape(Q_RANK, QK_WIDTH).astype(BF16)
    wqs = jnp.concatenate([_swap_halves(rope), jnp.zeros_like(nope), z32], axis=-1)
    wqs = wqs.reshape(Q_RANK, QK_WIDTH).astype(BF16)
    uk = w_uk[l]
    zk_lo = jnp.zeros(uk.shape[:2] + (QK_ROPE,), F32)
    zk_hi = jnp.zeros(uk.shape[:2] + (HEAD_SLOT - QK_NOPE - QK_ROPE,), F32)
    wk = jnp.concatenate([zk_lo, uk, zk_hi], axis=-1).reshape(KV_RANK, QK_WIDTH).astype(BF16)
    wv = w_uv[l].reshape(KV_RANK, V_WIDTH).astype(BF16)
    wv_slot = jnp.concatenate([w_uv[l], jnp.zeros_like(w_uv[l])], axis=-1).reshape(KV_RANK, QK_WIDTH).astype(BF16)
    vone = jnp.tile(jnp.concatenate([jnp.zeros((V_HEAD,), F32), jnp.ones((HEAD_SLOT - V_HEAD,), F32)]),
                    MLA_HEADS).reshape(1, QK_WIDTH)
    ukt = jnp.transpose(uk, (1, 2, 0))
    eye = jnp.broadcast_to(jnp.eye(QK_ROPE, LANES, dtype=F32), (MLA_HEADS, QK_ROPE, LANES))
    top = jnp.concatenate([jnp.zeros((MLA_HEADS, QK_ROPE, KV_RANK), F32), eye], axis=-1)
    mid = jnp.concatenate([ukt, jnp.zeros((MLA_HEADS, QK_NOPE, LANES), F32)], axis=-1)
    bot = jnp.zeros((MLA_HEADS, HEAD_SLOT - QK_NOPE - QK_ROPE, KV_RANK + LANES), F32)
    mabs = jnp.concatenate([top, mid, bot], axis=1).astype(BF16)
    rw = jnp.pad(router_w[l], ((0, 0), (0, ROUTER_PAD - N_EXPERTS)))
    rw_hi = rw.astype(BF16)
    rw_lo = (rw - rw_hi.astype(F32)).astype(BF16)
    rb = jnp.concatenate([router_b[l], jnp.full((ROUTER_PAD - N_EXPERTS,), NEG_BIG, F32)]).reshape(1, ROUTER_PAD)
    return dict(
        w_in_r=w_in_r, gmix=norm_mix_g[l].reshape(1, d),
        gq=(q_norm_g[l] * (ATTN_SCALE * LOG2_E)).reshape(1, Q_RANK),
        gkv=kv_norm_g[l].reshape(1, KV_RANK), wq=wq, wqs=wqs, wk=wk, wv=wv, wv_slot=wv_slot, vone=vone, mabs=mabs,
        sgg=sg_norm_g[l].reshape(1, SG_WIDTH), sgb=sg_norm_b[l].reshape(1, SG_WIDTH),
        w_pa=w_pa[l].astype(BF16), w_pb=w_pb[l].astype(BF16), w_o=w_o[l].astype(BF16),
        gffn=norm_ffn_g[l].reshape(1, d), rw_hi=rw_hi, rw_lo=rw_lo, rb=rb)


def _spatial_mix_weights(w_s, b_s, seq, n_batch):
    gw = SG_WIDTH // SG_GROUPS
    tril = jnp.tril(jnp.ones((SG_CHUNK, SG_CHUNK), dtype=bool))
    w = jnp.where(tril[None], w_s, 0.0)
    if seq % SG_CHUNK == 0:
        mixw = w
        bias_t = b_s
    else:
        assert seq < SG_CHUNK
        blk = w[:, :seq, :seq]
        eye = jnp.eye(n_batch, dtype=F32)
        mixw = jnp.einsum("ab,gts->gatbs", eye, blk).reshape(SG_GROUPS, n_batch * seq, n_batch * seq)
        bias_t = jnp.tile(b_s[:, :seq], (1, n_batch))
    bias = jnp.repeat(jnp.transpose(bias_t), gw, axis=1)
    return mixw.astype(BF16), bias


def _routing_tables(idx, rank, counts_f, n_blocks):
    counts = counts_f[0, :N_EXPERTS].astype(jnp.int32)
    padded = (counts + MOE_ROWS - 1) // MOE_ROWS * MOE_ROWS
    pend = jnp.cumsum(padded).astype(jnp.int32)
    pstart = pend - padded
    experts = jnp.arange(N_EXPERTS, dtype=jnp.int32)
    dest = rank + jnp.sum(jnp.where(idx[..., None] == experts, pstart, 0), axis=-1)
    block_start = jnp.arange(n_blocks, dtype=jnp.int32) * MOE_ROWS
    block_e = jnp.minimum(jnp.sum((pend[None, :] <= block_start[:, None]).astype(jnp.int32), axis=1), N_EXPERTS - 1)
    n_used = (pend[-1:] // MOE_ROWS).astype(jnp.int32)
    return dest.astype(jnp.int32), pend, block_e.astype(jnp.int32), n_used


def _pos_tiles(pos, tm):
    t = pos.shape[0]
    return jnp.transpose(pos.reshape(t // tm, tm, TOP_K), (0, 2, 1)).reshape(t // tm, TOP_K * tm)


def _pick_tile(n, pref):
    t = min(n, pref)
    assert n % t == 0 and t % 8 == 0
    return t


def kernel(x_prompt, x_sample, cache_ckv, cache_krope, c_prompt, c_sample, ada_w, ada_b, norm_mix_g, w_in, q_norm_g, w_uq, kv_norm_g, w_uk, w_uv, w_pa, sg_norm_g, sg_norm_b, w_spatial, b_spatial, w_pb, w_o, norm_ffn_g, router_w, router_b, w_gu, b_gu, w_dn, b_dn, final_g):
    bp, lp, d = x_prompt.shape
    bs, ls, _ = x_sample.shape
    depth = w_in.shape[0]
    past = cache_ckv.shape[2]
    tp, ts = bp * lp, bs * ls
    assert lp % SG_CHUNK == 0 and ls <= SG_CHUNK

    tm_p = _pick_tile(lp, ROW_TILE)
    tm_s = _pick_tile(ts, SAMPLE_TILE)
    t_attn = _pick_tile(lp, ATTN_TILE)
    tpb = lp // tm_p

    cc_p, ss_p = _rope_tables(jnp.arange(lp, dtype=jnp.int32))
    cc_s, ss_s = _rope_tables(past + jnp.arange(ls, dtype=jnp.int32))
    cc_s, ss_s = jnp.tile(cc_s, (bs, 1)), jnp.tile(ss_s, (bs, 1))

    b_all = bp + bs
    b_pad = -(-b_all // 8) * 8
    c_all = jnp.concatenate([c_prompt, c_sample, jnp.zeros((b_pad - b_all, d), F32)], axis=0)

    xp = x_prompt.reshape(tp, d)
    xs = x_sample.reshape(ts, d)
    outs = dict(ckv_p=[], kr_p=[], ckv_s=[], kr_s=[], v_s=[])
    final_g2 = final_g.reshape(1, d)
    for l in range(depth):
        wts = _layer_weights(l, w_in, norm_mix_g, q_norm_g, w_uq, kv_norm_g, w_uk, w_uv, w_pa, sg_norm_g,
                             sg_norm_b, w_pb, w_o, norm_ffn_g, router_w, router_b)
        mod = _adaln(c_all, ada_w[l], ada_b[l])
        mod_p = [mod[:bp, j * d:(j + 1) * d].reshape(bp, 1, d) for j in range(6)]
        mod_s = [jnp.repeat(mod[bp:b_all, j * d:(j + 1) * d], ls, axis=0) for j in range(6)]

        q, k, v, ckv, kr, u, vn, ga, gb = _inproj(xp, mod_p[0], mod_p[1], cc_p, ss_p, wts, tm=tm_p, per_row=False,
                                                  tiles_per_batch=tpb, vn_dtype=BF16)
        o = _attn_prompt(q.reshape(bp, lp, QK_WIDTH), k.reshape(bp, lp, QK_WIDTH), v.reshape(bp, lp, QK_WIDTH),
                         tq=t_attn, tk=t_attn).reshape(tp, V_WIDTH)
        mixw, bias = _spatial_mix_weights(w_spatial[l], b_spatial[l], lp, bp)
        x1p = _merge(o, u, vn, ga, gb, xp, mod_p[2], mixw, bias, wts, tm=tm_p, chunk=SG_CHUNK, per_row=False,
                     tiles_per_batch=tpb)
        outs["ckv_p"].append(ckv.reshape(bp, lp, KV_RANK))
        outs["kr_p"].append(kr.reshape(bp, lp, QK_ROPE))

        q, k, v, ckv, kr, u, vn, ga, gb = _inproj(xs, mod_s[0], mod_s[1], cc_s, ss_s, wts, tm=tm_s, per_row=True,
                                                  tiles_per_batch=1, vn_dtype=F32)
        ckv3, kr3 = ckv.reshape(bs, ls, KV_RANK), kr.reshape(bs, ls, QK_ROPE)
        o = _attn_sample(q.reshape(bs, ls, QK_WIDTH), cache_ckv[l], cache_krope[l], ckv3, kr3,
                         wts["mabs"], wts["wv"]).reshape(ts, V_WIDTH)
        mixw, bias = _spatial_mix_weights(w_spatial[l], b_spatial[l], ls, tm_s // ls)
        x1s = _merge(o, u, vn, ga, gb, xs, mod_s[2], mixw, bias, wts, tm=tm_s, chunk=tm_s, per_row=True,
                     tiles_per_batch=1)
        outs["ckv_s"].append(ckv3)
        outs["kr_s"].append(kr3)
        outs["v_s"].append(vn.reshape(bs, ls, SG_WIDTH))

        t_all = tp + ts
        tm_r = _pick_tile(math.gcd(tp, ts), ROW_TILE)
        h2_all, idx, gate, rank, counts = _router(x1p, mod_p[3], mod_p[4], x1s, mod_s[3], mod_s[4], wts, tm=tm_r,
                                                  tiles_per_batch=lp // tm_r)
        gate_p, gate_s = gate[:tp], gate[tp:]
        n_blocks = -(-(t_all * TOP_K) // MOE_ROWS) + N_EXPERTS
        pos, pend, block_e, n_used = _routing_tables(idx, rank, counts, n_blocks)
        x_sorted = _dispatch(pend, n_used, _pos_tiles(pos, tm_r), h2_all, tm=tm_r, n_blocks=n_blocks)
        ys = _moe_experts(block_e, n_used, x_sorted, w_gu[l], b_gu[l], w_dn[l], b_dn[l])
        last = l == depth - 1
        xp = _combine(_pos_tiles(pos[:tp], tm_r), ys, x1p, gate_p, mod_p[5], final_g2, tm=tm_r, per_row=False,
                      tiles_per_batch=lp // tm_r, final_norm=last)
        xs = _combine(_pos_tiles(pos[tp:], tm_r), ys, x1s, gate_s, mod_s[5], final_g2, tm=tm_r, per_row=True,
                      tiles_per_batch=1, final_norm=last)
    return (xp.reshape(bp, lp, d), xs.reshape(bs, ls, d),
            jnp.stack(outs["ckv_p"]), jnp.stack(outs["kr_p"]),
            jnp.stack(outs["ckv_s"]), jnp.stack(outs["kr_s"]), jnp.stack(outs["v_s"]))
```

```python
import functools
import math

import jax
import jax.numpy as jnp
from jax import lax
from jax.experimental import pallas as pl
from jax.experimental.pallas import tpu as pltpu

F32 = jnp.float32
BF16 = jnp.bfloat16

LANES = 128
VMEM_LIMIT_BYTES = 56 * 1024 * 1024

CHUNK = 64
CHUNK_SHIFT = 6
MLA_HEADS = 8
QK_NOPE = 64
QK_ROPE = 32
V_HEAD = 64
V_HEAD_SHIFT = 6
Q_RANK = 384
KV_RANK = 256
ROPE_THETA = 10000.0
ATTN_SCALE = 1.0 / math.sqrt(QK_NOPE + QK_ROPE)
LOG2_E = math.log2(math.e)
SG_CHUNK = 128
SG_GROUPS = 4
SG_WIDTH = 512
N_EXPERTS = 32
TOP_K = 4
SWIGLU_LIMIT = 7.0
SWIGLU_ALPHA = 1.702
EPS = 1e-6

HEAD_SLOT = LANES
QK_WIDTH = MLA_HEADS * HEAD_SLOT
V_WIDTH = MLA_HEADS * V_HEAD
MOE_ROWS = 256
ROW_TILE = 256
SAMPLE_TILE = 512
ATTN_TILE = 512
ATTN_SUB_KEYS = 256
ROUTER_PAD = LANES
NEG_BIG = -1e30

_C_CQ = 0
_C_CKV = _C_CQ + Q_RANK
_C_U = _C_CKV + KV_RANK
_C_V = _C_U + SG_WIDTH
_C_GA = _C_V + SG_WIDTH


def _params(sem):
    return pltpu.CompilerParams(dimension_semantics=sem, vmem_limit_bytes=VMEM_LIMIT_BYTES)


def _dot(a, b):
    return jnp.dot(a, b, preferred_element_type=F32)


def _dot_nt(a, b):
    return lax.dot_general(a, b, (((1,), (1,)), ((), ())), preferred_element_type=F32)


def _rms(x, g):
    return x * lax.rsqrt(jnp.mean(x * x, axis=-1, keepdims=True) + EPS) * g


def _adaln_kernel(c_ref, w_ref, b_ref, o_ref):
    c = c_ref[...]
    s = (c * jax.nn.sigmoid(c)).astype(BF16)
    o_ref[...] = _dot(s, w_ref[...].astype(BF16)) + b_ref[...]


def _adaln(c_all, ada_w, ada_b):
    bp, d = c_all.shape
    n = ada_w.shape[1]
    return pl.pallas_call(
        _adaln_kernel,
        grid=(n // d,),
        in_specs=[pl.BlockSpec((bp, d), lambda j: (0, 0)),
                  pl.BlockSpec((d, d), lambda j: (0, j)),
                  pl.BlockSpec((1, d), lambda j: (0, j))],
        out_specs=pl.BlockSpec((bp, d), lambda j: (0, j)),
        out_shape=jax.ShapeDtypeStruct((bp, n), F32),
        compiler_params=_params(("arbitrary",)),
        name="adaln",
    )(c_all, ada_w, ada_b.reshape(1, n))


def _inproj_kernel(x_ref, sh_ref, sc_ref, gmix_ref, cc_ref, ss_ref, win_ref, gq_ref, gkv_ref,
                   wq_ref, wqs_ref, wk_ref, wv_ref, vone_ref, sgg_ref, sgb_ref,
                   q_ref, k_ref, v_ref, ckv_ref, kr_ref, u_ref, vn_ref, ga_ref, gb_ref, *, d_model):
    x = x_ref[...]
    h = (_rms(x, gmix_ref[...]) * (1.0 + sc_ref[...]) + sh_ref[...]).astype(BF16)

    def proj(lo, width):
        return _dot(h, win_ref[:, lo:lo + width])

    cc = cc_ref[...]
    ss = ss_ref[...]
    c_gb = _C_GA + d_model
    c_kra = c_gb + d_model
    c_krb = c_kra + LANES

    cqn = _rms(proj(_C_CQ, Q_RANK), gq_ref[...]).astype(BF16)
    qa = _dot(cqn, wq_ref[...])
    qb = _dot(cqn, wqs_ref[...])
    for hd in range(MLA_HEADS):
        sl = slice(hd * HEAD_SLOT, (hd + 1) * HEAD_SLOT)
        q_ref[:, sl] = (qa[:, sl] * cc + qb[:, sl] * ss).astype(BF16)

    ckvn = _rms(proj(_C_CKV, KV_RANK), gkv_ref[...])
    ckv_ref[...] = ckvn
    ckvb = ckvn.astype(BF16)
    krs = proj(c_kra, LANES) * cc + proj(c_krb, LANES) * ss
    kr_ref[...] = krs[:, :QK_ROPE]
    kn = _dot(ckvb, wk_ref[...])
    for hd in range(MLA_HEADS):
        sl = slice(hd * HEAD_SLOT, (hd + 1) * HEAD_SLOT)
        k_ref[:, sl] = (kn[:, sl] + krs).astype(BF16)
    v_ref[...] = (_dot(ckvb, wv_ref[...]) + vone_ref[...]).astype(BF16)

    u_ref[...] = proj(_C_U, SG_WIDTH).astype(u_ref.dtype)
    vv = proj(_C_V, SG_WIDTH)
    mu = jnp.mean(vv, axis=-1, keepdims=True)
    vc = vv - mu
    var = jnp.mean(vc * vc, axis=-1, keepdims=True)
    vn_ref[...] = (vc * lax.rsqrt(var + EPS) * sgg_ref[...] + sgb_ref[...]).astype(vn_ref.dtype)
    ga_ref[...] = proj(_C_GA, d_model).astype(BF16)
    gb_ref[...] = proj(c_gb, d_model).astype(BF16)


def _mod_spec(per_row, tm, d, tiles_per_batch):
    if per_row:
        return pl.BlockSpec((tm, d), lambda i: (i, 0))
    return pl.BlockSpec((None, 1, d), lambda i: (i // tiles_per_batch, 0, 0))


def _const_spec(shape):
    nd = len(shape)
    return pl.BlockSpec(shape, lambda i: (0,) * nd)


def _inproj(x2d, shift, scale, cc, ss, wts, *, tm, per_row, tiles_per_batch, vn_dtype):
    t, d = x2d.shape
    n_tab = cc.shape[0] // tm
    row = lambda w: pl.BlockSpec((tm, w), lambda i: (i, 0))
    tab = pl.BlockSpec((tm, LANES), lambda i: (i % n_tab, 0))
    mod = _mod_spec(per_row, tm, d, tiles_per_batch)
    consts = [wts["w_in_r"], wts["gq"], wts["gkv"], wts["wq"], wts["wqs"], wts["wk"], wts["wv_slot"],
              wts["vone"], wts["sgg"], wts["sgb"]]
    out_shapes = [jax.ShapeDtypeStruct((t, QK_WIDTH), BF16), jax.ShapeDtypeStruct((t, QK_WIDTH), BF16),
                  jax.ShapeDtypeStruct((t, QK_WIDTH), BF16), jax.ShapeDtypeStruct((t, KV_RANK), F32),
                  jax.ShapeDtypeStruct((t, QK_ROPE), F32), jax.ShapeDtypeStruct((t, SG_WIDTH), BF16),
                  jax.ShapeDtypeStruct((t, SG_WIDTH), vn_dtype), jax.ShapeDtypeStruct((t, d), BF16),
                  jax.ShapeDtypeStruct((t, d), BF16)]
    return pl.pallas_call(
        functools.partial(_inproj_kernel, d_model=d),
        grid=(t // tm,),
        in_specs=[row(d), mod, mod, _const_spec((1, d)), tab, tab] + [_const_spec(c.shape) for c in consts],
        out_specs=[row(s.shape[1]) for s in out_shapes],
        out_shape=out_shapes,
        compiler_params=_params(("arbitrary",)),
        name="inproj",
    )(x2d, shift, scale, wts["gmix"], cc, ss, *consts)


def _attn_kernel(qi_ref, kj_ref, flag_ref, q_ref, k_ref, v_ref, o_ref, m_sc, acc_sc, *, tq, tk, sub):
    s_id = pl.program_id(1)
    qi = qi_ref[s_id]
    kj = kj_ref[s_id]
    flags = flag_ref[s_id]

    @pl.when(kj == 0)
    def _():
        m_sc[...] = jnp.full(m_sc.shape, -jnp.inf, F32)
        acc_sc[...] = jnp.zeros(acc_sc.shape, F32)

    def sweep(bias):
        for kb in range(tk // sub):
            keys = slice(kb * sub, (kb + 1) * sub)
            for hd in range(MLA_HEADS):
                sl = slice(hd * HEAD_SLOT, (hd + 1) * HEAD_SLOT)
                s = _dot_nt(q_ref[:, sl], k_ref[keys, sl])
                if bias is not None:
                    s = s + bias[:, keys]
                tiles = [s[:, c * LANES:(c + 1) * LANES] for c in range(sub // LANES)]
                m_tile = tiles[0]
                for t in tiles[1:]:
                    m_tile = jnp.maximum(m_tile, t)
                m_old = m_sc[hd]
                m_new = jnp.maximum(m_old, jnp.max(m_tile, axis=-1, keepdims=True))
                alpha = jnp.exp2(m_old - m_new)
                p = jnp.concatenate([jnp.exp2(t - m_new).astype(BF16) for t in tiles], axis=1)
                acc_sc[hd] = alpha * acc_sc[hd] + _dot(p, v_ref[keys, sl])
                m_sc[hd] = m_new

    @pl.when((flags & 2) == 0)
    def _():
        sweep(None)

    @pl.when((flags & 2) != 0)
    def _():
        row = lax.broadcasted_iota(jnp.int32, (tq, tk), 0) + qi * tq
        col = lax.broadcasted_iota(jnp.int32, (tq, tk), 1) + kj * tk
        sweep(jnp.where((col >> CHUNK_SHIFT) <= (row >> CHUNK_SHIFT), 0.0, -jnp.inf))

    @pl.when((flags & 1) != 0)
    def _():
        lane = lax.broadcasted_iota(jnp.int32, (tq, LANES), 1)
        for pr in range(MLA_HEADS // 2):
            outs = []
            for hd in (2 * pr, 2 * pr + 1):
                acc = acc_sc[hd]
                outs.append(acc / pltpu.roll(acc, V_HEAD, axis=1))
            pair = jnp.where(lane < V_HEAD, outs[0], pltpu.roll(outs[1], V_HEAD, axis=1))
            o_ref[:, pr * LANES:(pr + 1) * LANES] = pair.astype(BF16)


def _attn_prompt(q, k, v, *, tq, tk):
    b, l, _ = q.shape
    nq = l // tq
    qi_l, kj_l, flag_l = [], [], []
    for i in range(nq):
        n_kv = ((i + 1) * tq - 1) // tk + 1
        for j in range(n_kv):
            qi_l.append(i)
            kj_l.append(j)
            masked = ((j + 1) * tk - 1) // CHUNK > (i * tq) // CHUNK
            flag_l.append((1 if j == n_kv - 1 else 0) | (2 if masked else 0))
    steps = len(qi_l)
    grid_spec = pltpu.PrefetchScalarGridSpec(
        num_scalar_prefetch=3,
        grid=(b, steps),
        in_specs=[pl.BlockSpec((None, tq, QK_WIDTH), lambda bi, s, qi, kj, fl: (bi, qi[s], 0)),
                  pl.BlockSpec((None, tk, QK_WIDTH), lambda bi, s, qi, kj, fl: (bi, kj[s], 0)),
                  pl.BlockSpec((None, tk, QK_WIDTH), lambda bi, s, qi, kj, fl: (bi, kj[s], 0))],
        out_specs=pl.BlockSpec((None, tq, V_WIDTH), lambda bi, s, qi, kj, fl: (bi, qi[s], 0)),
        scratch_shapes=[pltpu.VMEM((MLA_HEADS, tq, LANES), F32), pltpu.VMEM((MLA_HEADS, tq, LANES), F32)],
    )
    return pl.pallas_call(
        functools.partial(_attn_kernel, tq=tq, tk=tk, sub=min(tk, ATTN_SUB_KEYS)),
        grid_spec=grid_spec,
        out_shape=jax.ShapeDtypeStruct((b, l, V_WIDTH), BF16),
        compiler_params=_params(("arbitrary", "arbitrary")),
        name="attn_prompt",
    )(jnp.asarray(qi_l, jnp.int32), jnp.asarray(kj_l, jnp.int32), jnp.asarray(flag_l, jnp.int32), q, k, v)


def _attn_sample_kernel(q_ref, pckv_ref, pkr_ref, nckv_ref, nkr_ref, mabs_ref, wv_ref, o_ref, *, ls, past):
    hl = MLA_HEADS * ls
    qcat = jnp.concatenate(
        [_dot(q_ref[:, hd * HEAD_SLOT:(hd + 1) * HEAD_SLOT], mabs_ref[hd]) for hd in range(MLA_HEADS)],
        axis=0).astype(BF16)
    q_abs = qcat[:, :KV_RANK]
    q_rope = qcat[:, KV_RANK:]
    pckv = pckv_ref[...].astype(BF16)
    nckv = nckv_ref[...].astype(BF16)

    def pad_lanes(kr):
        return jnp.concatenate([kr, jnp.zeros((kr.shape[0], LANES - QK_ROPE), kr.dtype)], axis=1).astype(BF16)

    s_past = _dot_nt(q_abs, pckv) + _dot_nt(q_rope, pad_lanes(pkr_ref[...]))
    s_new = _dot_nt(q_abs, nckv) + _dot_nt(q_rope, pad_lanes(nkr_ref[...]))

    qpos_1 = lax.broadcasted_iota(jnp.int32, (ls, 1), 0) + past
    qchunk = jnp.concatenate([qpos_1] * MLA_HEADS, axis=0) >> CHUNK_SHIFT
    kchunk_past = lax.broadcasted_iota(jnp.int32, (hl, past), 1) >> CHUNK_SHIFT
    kchunk_new = (lax.broadcasted_iota(jnp.int32, (hl, ls), 1) + past) >> CHUNK_SHIFT
    s_past = jnp.where(kchunk_past <= qchunk, s_past, -jnp.inf)
    s_new = jnp.where(kchunk_new <= qchunk, s_new, -jnp.inf)

    m = jnp.maximum(jnp.max(s_past, axis=-1, keepdims=True), jnp.max(s_new, axis=-1, keepdims=True))
    p_past = jnp.exp2(s_past - m)
    p_new = jnp.exp2(s_new - m)
    denom = jnp.sum(p_past, axis=-1, keepdims=True) + jnp.sum(p_new, axis=-1, keepdims=True)
    olat = (_dot(p_past.astype(BF16), pckv) + _dot(p_new.astype(BF16), nckv)) / denom
    ofull = _dot(olat.astype(BF16), wv_ref[...])
    col_head = lax.broadcasted_iota(jnp.int32, (ls, V_WIDTH), 1) >> V_HEAD_SHIFT
    out = jnp.zeros((ls, V_WIDTH), F32)
    for hd in range(MLA_HEADS):
        out = out + jnp.where(col_head == hd, ofull[hd * ls:(hd + 1) * ls], 0.0)
    o_ref[...] = out.astype(BF16)


def _attn_sample(q, past_ckv, past_kr, new_ckv, new_kr, mabs, wv):
    b, ls, _ = q.shape
    past = past_ckv.shape[1]
    blk = lambda n, w: pl.BlockSpec((None, n, w), lambda i: (i, 0, 0))
    return pl.pallas_call(
        functools.partial(_attn_sample_kernel, ls=ls, past=past),
        grid=(b,),
        in_specs=[blk(ls, QK_WIDTH), blk(past, KV_RANK), blk(past, QK_ROPE), blk(ls, KV_RANK), blk(ls, QK_ROPE),
                  _const_spec(mabs.shape), _const_spec(wv.shape)],
        out_specs=blk(ls, V_WIDTH),
        out_shape=jax.ShapeDtypeStruct((b, ls, V_WIDTH), BF16),
        compiler_params=_params(("arbitrary",)),
        name="attn_sample",
    )(q, past_ckv, past_kr, new_ckv, new_kr, mabs, wv)


def _merge_kernel(o_ref, u_ref, vn_ref, ga_ref, gb_ref, x_ref, gate_ref, mix_ref, bias_ref,
                  wpa_ref, wpb_ref, wo_ref, x1_ref, sg_sc, *, chunk):
    tm = x_ref.shape[0]
    gw = SG_WIDTH // SG_GROUPS
    for c in range(tm // chunk):
        rows = slice(c * chunk, (c + 1) * chunk)
        for g in range(SG_GROUPS):
            cols = slice(g * gw, (g + 1) * gw)
            mixed = _dot(mix_ref[g], vn_ref[rows, cols].astype(BF16)) + bias_ref[:, cols]
            sg_sc[rows, cols] = (u_ref[rows, cols].astype(F32) * mixed).astype(BF16)
    ya = _dot(o_ref[...], wpa_ref[...])
    yb = _dot(sg_sc[...], wpb_ref[...])
    m = jax.nn.sigmoid(ga_ref[...].astype(F32)) * ya + jax.nn.sigmoid(gb_ref[...].astype(F32)) * yb
    x1_ref[...] = x_ref[...] + gate_ref[...] * _dot(m.astype(BF16), wo_ref[...])


def _merge(o, u, vn, ga, gb, x2d, gate, mixw, bias, wts, *, tm, chunk, per_row, tiles_per_batch):
    t, d = x2d.shape
    row = lambda w: pl.BlockSpec((tm, w), lambda i: (i, 0))
    consts = [mixw, bias, wts["w_pa"], wts["w_pb"], wts["w_o"]]
    return pl.pallas_call(
        functools.partial(_merge_kernel, chunk=chunk),
        grid=(t // tm,),
        in_specs=[row(V_WIDTH), row(SG_WIDTH), row(SG_WIDTH), row(d), row(d), row(d),
                  _mod_spec(per_row, tm, d, tiles_per_batch)] + [_const_spec(c.shape) for c in consts],
        out_specs=row(d),
        out_shape=jax.ShapeDtypeStruct((t, d), F32),
        scratch_shapes=[pltpu.VMEM((tm, SG_WIDTH), BF16)],
        compiler_params=_params(("arbitrary",)),
        name="merge",
    )(o, u, vn, ga, gb, x2d, gate, *consts)


def _router_kernel(xp_ref, shp_ref, scp_ref, xs_ref, shs_ref, scs_ref, g_ref, whi_ref, wlo_ref, rb_ref,
                   h2_ref, idx_ref, gate_ref, rank_ref, cnt_ref, carry_sc, *, n_prompt_tiles):
    i = pl.program_id(0)
    out_refs = (g_ref, whi_ref, wlo_ref, rb_ref, h2_ref, idx_ref, gate_ref, rank_ref, cnt_ref, carry_sc)

    @pl.when(i == 0)
    def _():
        carry_sc[...] = jnp.zeros(carry_sc.shape, F32)

    @pl.when(i < n_prompt_tiles)
    def _():
        _route_rows(xp_ref, shp_ref, scp_ref, *out_refs)

    @pl.when(i >= n_prompt_tiles)
    def _():
        _route_rows(xs_ref, shs_ref, scs_ref, *out_refs)


def _route_rows(x1_ref, sh_ref, sc_ref, g_ref, whi_ref, wlo_ref, rb_ref, h2_ref, idx_ref, gate_ref, rank_ref,
                cnt_ref, carry_sc):
    h2 = _rms(x1_ref[...], g_ref[...]) * (1.0 + sc_ref[...]) + sh_ref[...]
    h2_ref[...] = h2
    hi = h2.astype(BF16)
    lo = (h2 - hi.astype(F32)).astype(BF16)
    logits = _dot(hi, whi_ref[...]) + _dot(lo, whi_ref[...]) + _dot(hi, wlo_ref[...]) + rb_ref[...]
    lane = lax.broadcasted_iota(jnp.int32, logits.shape, 1)
    vals, idxs = [], []
    work = logits
    for _ in range(TOP_K):
        mx = jnp.max(work, axis=-1, keepdims=True)
        ix = jnp.min(jnp.where(work == mx, lane, ROUTER_PAD), axis=-1, keepdims=True)
        vals.append(mx)
        idxs.append(ix)
        work = jnp.where(lane == ix, -jnp.inf, work)
    es = [jnp.exp(v - vals[0]) for v in vals]
    tot = es[0]
    for e in es[1:]:
        tot = tot + e
    idx_w = jnp.zeros(logits.shape, jnp.int32)
    gate_w = jnp.zeros(logits.shape, F32)
    for j in range(TOP_K):
        idx_w = jnp.where(lane == j, idxs[j], idx_w)
        gate_w = jnp.where(lane == j, es[j] / tot, gate_w)
    idx_ref[...] = idx_w[:, :TOP_K]
    gate_ref[...] = gate_w[:, :TOP_K]

    tm = logits.shape[0]
    onehot = jnp.zeros(logits.shape, F32)
    for j in range(TOP_K):
        onehot = jnp.where(lane == idxs[j], 1.0, onehot)
    earlier = (lax.broadcasted_iota(jnp.int32, (tm, tm), 1) < lax.broadcasted_iota(jnp.int32, (tm, tm), 0))
    within = _dot(jnp.where(earlier, 1.0, 0.0).astype(BF16), onehot.astype(BF16))
    rank_full = within + carry_sc[...]
    rank_w = jnp.zeros(logits.shape, jnp.int32)
    for j in range(TOP_K):
        rj = jnp.sum(jnp.where(lane == idxs[j], rank_full, 0.0), axis=-1, keepdims=True)
        rank_w = jnp.where(lane == j, rj.astype(jnp.int32), rank_w)
    rank_ref[...] = rank_w[:, :TOP_K]
    carry_sc[...] = carry_sc[...] + jnp.sum(onehot, axis=0, keepdims=True)
    cnt_ref[...] = carry_sc[...]


def _router(x1p, shift_p, scale_p, x1s, shift_s, scale_s, wts, *, tm, tiles_per_batch):
    tp, d = x1p.shape
    ts = x1s.shape[0]
    n_p, n_s = tp // tm, ts // tm
    t_all = tp + ts
    row = lambda w: pl.BlockSpec((tm, w), lambda i: (i, 0))
    p_row = pl.BlockSpec((tm, d), lambda i: (jnp.minimum(i, n_p - 1), 0))
    p_mod = pl.BlockSpec((None, 1, d), lambda i: (jnp.minimum(i, n_p - 1) // tiles_per_batch, 0, 0))
    s_row = pl.BlockSpec((tm, d), lambda i: (jnp.maximum(i - n_p, 0), 0))
    consts = [wts["gffn"], wts["rw_hi"], wts["rw_lo"], wts["rb"]]
    return pl.pallas_call(
        functools.partial(_router_kernel, n_prompt_tiles=n_p),
        grid=(n_p + n_s,),
        in_specs=[p_row, p_mod, p_mod, s_row, s_row, s_row] + [_const_spec(c.shape) for c in consts],
        out_specs=[row(d), row(TOP_K), row(TOP_K), row(TOP_K), _const_spec((1, ROUTER_PAD))],
        out_shape=[jax.ShapeDtypeStruct((t_all, d), F32), jax.ShapeDtypeStruct((t_all, TOP_K), jnp.int32),
                   jax.ShapeDtypeStruct((t_all, TOP_K), F32), jax.ShapeDtypeStruct((t_all, TOP_K), jnp.int32),
                   jax.ShapeDtypeStruct((1, ROUTER_PAD), F32)],
        scratch_shapes=[pltpu.VMEM((1, ROUTER_PAD), F32)],
        compiler_params=_params(("arbitrary",)),
        name="router",
    )(x1p, shift_p, scale_p, x1s, shift_s, scale_s, *consts)


ROW_GROUP = 8


def _slot_offset(slot, n_rows):
    return slot * n_rows if isinstance(slot, int) else pl.multiple_of(slot * n_rows, n_rows)


def _row_gather_start(idx_smem, slot, src_hbm, dst_vmem, sem, n_rows):
    base = _slot_offset(slot, n_rows)

    def group(g, carry):
        r0 = pl.multiple_of(g * ROW_GROUP, ROW_GROUP)
        for j in range(ROW_GROUP):
            pltpu.make_async_copy(src_hbm.at[pl.ds(idx_smem[base + r0 + j], 1)], dst_vmem.at[pl.ds(r0 + j, 1)],
                                  sem).start()
        return carry
    lax.fori_loop(0, n_rows // ROW_GROUP, group, 0)


def _row_gather_wait(src_hbm, dst_vmem, sem, n_rows):
    pltpu.make_async_copy(src_hbm.at[pl.ds(0, n_rows)], dst_vmem, sem).wait()


def _gather_pipeline(i, n_steps, idx_hbm, idx_smem, isem, src_hbm, buf, gsem, n_rows):
    def idx_copy(blk, slot):
        return pltpu.make_async_copy(idx_hbm.at[blk], idx_smem.at[pl.ds(_slot_offset(slot, n_rows), n_rows)],
                                     isem.at[slot])

    @pl.when(i == 0)
    def _():
        idx_copy(0, 0).start()
        idx_copy(0, 0).wait()
        _row_gather_start(idx_smem, 0, src_hbm, buf.at[0], gsem.at[0], n_rows)

        @pl.when(n_steps > 1)
        def _():
            idx_copy(1, 1).start()

    nxt = (i + 1) % 2

    @pl.when(i + 1 < n_steps)
    def _():
        idx_copy(i + 1, nxt).wait()
        _row_gather_start(idx_smem, nxt, src_hbm, buf.at[nxt], gsem.at[nxt], n_rows)

    @pl.when(i + 2 < n_steps)
    def _():
        idx_copy(i + 2, i % 2).start()

    _row_gather_wait(src_hbm, buf.at[i % 2], gsem.at[i % 2], n_rows)


def _dispatch_kernel(pend_ref, nused_ref, dest_ref, h2_ref, xs_ref, zbuf, idx_smem, isem, csem, zsem, *,
                     tm, n_blocks):
    i = pl.program_id(0)
    n = pl.num_programs(0)
    rows = TOP_K * tm

    def idx_copy(blk, slot):
        return pltpu.make_async_copy(dest_ref.at[blk], idx_smem.at[pl.ds(_slot_offset(slot, rows), rows)],
                                     isem.at[slot])

    def zero_copy(block_start):
        start = pl.multiple_of(block_start, MOE_ROWS)
        return pltpu.make_async_copy(zbuf, xs_ref.at[pl.ds(start, MOE_ROWS)], zsem)

    def rows_done():
        return pltpu.make_async_copy(xs_ref.at[pl.ds(0, rows)], xs_ref.at[pl.ds(0, rows)], csem)

    @pl.when(i == 0)
    def _():
        idx_copy(0, 0).start()
        zbuf[...] = jnp.zeros(zbuf.shape, F32)
        n_used = nused_ref[0]

        def last_block(e, carry):
            zero_copy(jnp.maximum(pend_ref[e] - MOE_ROWS, 0)).start()
            return carry
        lax.fori_loop(0, N_EXPERTS, last_block, 0)

        def tail_block(b, carry):
            zero_copy(b * MOE_ROWS).start()
            return carry
        lax.fori_loop(n_used, n_blocks, tail_block, 0)

        def drain(b, carry):
            zero_copy(0).wait()
            return carry
        lax.fori_loop(0, N_EXPERTS + n_blocks - n_used, drain, 0)

    slot = i % 2
    idx_copy(i, slot).wait()

    @pl.when(i + 1 < n)
    def _():
        idx_copy(i + 1, 1 - slot).start()

    base = _slot_offset(slot, rows)

    def group(g, carry):
        r0 = pl.multiple_of(g * ROW_GROUP, ROW_GROUP)
        for j in range(ROW_GROUP):
            src = h2_ref.at[pl.ds(r0 + j, 1)]
            for kk in range(TOP_K):
                dst = xs_ref.at[pl.ds(idx_smem[base + kk * tm + r0 + j], 1)]
                pltpu.make_async_copy(src, dst, csem).start()
        return carry
    lax.fori_loop(0, tm // ROW_GROUP, group, 0)
    rows_done().wait()


def _dispatch(pend, n_used, dest_tiles, h2_all, *, tm, n_blocks):
    t_all, d = h2_all.shape
    grid_spec = pltpu.PrefetchScalarGridSpec(
        num_scalar_prefetch=2,
        grid=(t_all // tm,),
        in_specs=[pl.BlockSpec(memory_space=pl.ANY), pl.BlockSpec((tm, d), lambda i, pe, nu: (i, 0))],
        out_specs=pl.BlockSpec(memory_space=pl.ANY),
        scratch_shapes=[pltpu.VMEM((MOE_ROWS, d), F32),
                        pltpu.SMEM((2 * TOP_K * tm,), jnp.int32),
                        pltpu.SemaphoreType.DMA((2,)),
                        pltpu.SemaphoreType.DMA(()),
                        pltpu.SemaphoreType.DMA(())],
    )
    return pl.pallas_call(
        functools.partial(_dispatch_kernel, tm=tm, n_blocks=n_blocks),
        grid_spec=grid_spec,
        out_shape=jax.ShapeDtypeStruct((n_blocks * MOE_ROWS, d), F32),
        compiler_params=_params(("arbitrary",)),
        name="dispatch",
    )(pend, n_used, dest_tiles, h2_all)


def _moe_kernel(be_ref, nused_ref, xs_ref, wgu_ref, bgu_ref, wdn_ref, bdn_ref, y_ref, wgu_bf, wdn_bf, *, d_model):
    i = pl.program_id(0)
    n_used = nused_ref[0]

    @pl.when(i < n_used)
    def _():
        prev = be_ref[jnp.maximum(i - 1, 0)]

        @pl.when((i == 0) | (be_ref[i] != prev))
        def _():
            wgu_bf[...] = wgu_ref[...].astype(BF16)
            wdn_bf[...] = wdn_ref[...].astype(BF16)

        xb = xs_ref[...].astype(BF16)
        gu = _dot(xb, wgu_bf[...]) + bgu_ref[...]
        g = jnp.minimum(gu[:, :d_model], SWIGLU_LIMIT)
        lin = jnp.clip(gu[:, d_model:], -SWIGLU_LIMIT, SWIGLU_LIMIT)
        act = g * jax.nn.sigmoid(SWIGLU_ALPHA * g) * (lin + 1.0)
        y_ref[...] = _dot(act.astype(BF16), wdn_bf[...]) + bdn_ref[...]

    @pl.when(i >= n_used)
    def _():
        y_ref[...] = jnp.zeros(y_ref.shape, F32)


def _moe_experts(block_e, n_used, xs, w_gu, b_gu, w_dn, b_dn):
    n_blocks = xs.shape[0] // MOE_ROWS
    e, d, d2 = w_gu.shape
    grid_spec = pltpu.PrefetchScalarGridSpec(
        num_scalar_prefetch=2,
        grid=(n_blocks,),
        in_specs=[pl.BlockSpec((MOE_ROWS, d), lambda i, be, nu: (jnp.minimum(i, nu[0] - 1), 0)),
                  pl.BlockSpec((None, d, d2), lambda i, be, nu: (be[i], 0, 0)),
                  pl.BlockSpec((None, 1, d2), lambda i, be, nu: (be[i], 0, 0)),
                  pl.BlockSpec((None, d, d), lambda i, be, nu: (be[i], 0, 0)),
                  pl.BlockSpec((None, 1, d), lambda i, be, nu: (be[i], 0, 0))],
        out_specs=pl.BlockSpec((MOE_ROWS, d), lambda i, be, nu: (i, 0)),
        scratch_shapes=[pltpu.VMEM((d, d2), BF16),
                        pltpu.VMEM((d, d), BF16)],
    )
    return pl.pallas_call(
        functools.partial(_moe_kernel, d_model=d),
        grid_spec=grid_spec,
        out_shape=jax.ShapeDtypeStruct((n_blocks * MOE_ROWS, d), F32),
        compiler_params=_params(("arbitrary",)),
        name="moe_experts",
    )(block_e, n_used, xs, w_gu, b_gu.reshape(e, 1, d2), w_dn, b_dn.reshape(e, 1, d))


def _combine_kernel(pos_ref, ys_ref, x1_ref, gate_ref, gm_ref, gfin_ref, y_ref, ybuf, idx_smem, isem, gsem, *,
                    tm, final_norm):
    i = pl.program_id(0)
    _gather_pipeline(i, pl.num_programs(0), pos_ref, idx_smem, isem, ys_ref, ybuf, gsem, TOP_K * tm)
    gate = gate_ref[...]
    f = jnp.zeros(x1_ref.shape, F32)
    for kk in range(TOP_K):
        f = f + gate[:, kk:kk + 1] * ybuf[i % 2, kk * tm:(kk + 1) * tm, :]
    x2 = x1_ref[...] + gm_ref[...] * f
    y_ref[...] = _rms(x2, gfin_ref[...]) if final_norm else x2


def _combine(pos_tiles, ys, x1, gate, g_m, final_g, *, tm, per_row, tiles_per_batch, final_norm):
    t, d = x1.shape
    row = lambda w: pl.BlockSpec((tm, w), lambda i: (i, 0))
    return pl.pallas_call(
        functools.partial(_combine_kernel, tm=tm, final_norm=final_norm),
        grid=(t // tm,),
        in_specs=[pl.BlockSpec(memory_space=pl.ANY), pl.BlockSpec(memory_space=pl.ANY), row(d), row(TOP_K),
                  _mod_spec(per_row, tm, d, tiles_per_batch), _const_spec((1, d))],
        out_specs=row(d),
        out_shape=jax.ShapeDtypeStruct((t, d), F32),
        scratch_shapes=[pltpu.VMEM((2, TOP_K * tm, d), F32),
                        pltpu.SMEM((2 * TOP_K * tm,), jnp.int32),
                        pltpu.SemaphoreType.DMA((2,)),
                        pltpu.SemaphoreType.DMA((2,))],
        compiler_params=_params(("arbitrary",)),
        name="combine",
    )(pos_tiles, ys, x1, gate, g_m, final_g)


def _rope_tables(pos):
    inv = ROPE_THETA ** (-jnp.arange(0, QK_ROPE, 2, dtype=F32) / QK_ROPE)
    ang = pos.astype(F32)[:, None] * inv[None, :]
    cos, sin = jnp.cos(ang), jnp.sin(ang)
    n = pos.shape[0]
    cc = jnp.concatenate([cos, cos, jnp.ones((n, LANES - QK_ROPE), F32)], axis=1)
    ss = jnp.concatenate([sin, sin, jnp.zeros((n, LANES - QK_ROPE), F32)], axis=1)
    return cc, ss


def _swap_halves(w):
    half = QK_ROPE // 2
    return jnp.concatenate([-w[..., half:], w[..., :half]], axis=-1)


def _layer_weights(l, w_in, norm_mix_g, q_norm_g, w_uq, kv_norm_g, w_uk, w_uv, w_pa, sg_norm_g, sg_norm_b,
                   w_pb, w_o, norm_ffn_g, router_w, router_b):
    d = w_in.shape[1]
    wi = w_in[l]
    o_kr = Q_RANK + KV_RANK
    o_u = o_kr + QK_ROPE
    o_v = o_u + SG_WIDTH
    o_ga = o_v + SG_WIDTH
    kr = wi[:, o_kr:o_u]
    zpad = jnp.zeros((d, LANES - QK_ROPE), F32)
    w_in_r = jnp.concatenate([wi[:, :o_kr], wi[:, o_u:o_ga], wi[:, o_ga:],
                              kr, zpad, _swap_halves(kr), zpad], axis=1).astype(BF16)
    uq = w_uq[l]
    nope, rope = uq[..., :QK_NOPE], uq[..., QK_NOPE:]
    z32 = jnp.zeros(rope.shape[:2] + (HEAD_SLOT - QK_NOPE - QK_ROPE,), F32)
    wq = jnp.concatenate([rope, nope, z32], axis=-1).reshape(Q_RANK, QK_WIDTH).astype(BF16)
    wqs = jnp.concatenate([_swap_halves(rope), jnp.zeros_like(nope), z32], axis=-1)
    wqs = wqs.reshape(Q_RANK, QK_WIDTH).astype(BF16)
    uk = w_uk[l]
    zk_lo = jnp.zeros(uk.shape[:2] + (QK_ROPE,), F32)
    zk_hi = jnp.zeros(uk.shape[:2] + (HEAD_SLOT - QK_NOPE - QK_ROPE,), F32)
    wk = jnp.concatenate([zk_lo, uk, zk_hi], axis=-1).reshape(KV_RANK, QK_WIDTH).astype(BF16)
    wv = w_uv[l].reshape(KV_RANK, V_WIDTH).astype(BF16)
    wv_slot = jnp.concatenate([w_uv[l], jnp.zeros_like(w_uv[l])], axis=-1).reshape(KV_RANK, QK_WIDTH).astype(BF16)
    vone = jnp.tile(jnp.concatenate([jnp.zeros((V_HEAD,), F32), jnp.ones((HEAD_SLOT - V_HEAD,), F32)]),
                    MLA_HEADS).reshape(1, QK_WIDTH)
    ukt = jnp.transpose(uk, (1, 2, 0))
    eye = jnp.broadcast_to(jnp.eye(QK_ROPE, LANES, dtype=F32), (MLA_HEADS, QK_ROPE, LANES))
    top = jnp.concatenate([jnp.zeros((MLA_HEADS, QK_ROPE, KV_RANK), F32), eye], axis=-1)
    mid = jnp.concatenate([ukt, jnp.zeros((MLA_HEADS, QK_NOPE, LANES), F32)], axis=-1)
    bot = jnp.zeros((MLA_HEADS, HEAD_SLOT - QK_NOPE - QK_ROPE, KV_RANK + LANES), F32)
    mabs = jnp.concatenate([top, mid, bot], axis=1).astype(BF16)
    rw = jnp.pad(router_w[l], ((0, 0), (0, ROUTER_PAD - N_EXPERTS)))
    rw_hi = rw.astype(BF16)
    rw_lo = (rw - rw_hi.astype(F32)).astype(BF16)
    rb = jnp.concatenate([router_b[l], jnp.full((ROUTER_PAD - N_EXPERTS,), NEG_BIG, F32)]).reshape(1, ROUTER_PAD)
    return dict(
        w_in_r=w_in_r, gmix=norm_mix_g[l].reshape(1, d),
        gq=(q_norm_g[l] * (ATTN_SCALE * LOG2_E)).reshape(1, Q_RANK),
        gkv=kv_norm_g[l].reshape(1, KV_RANK), wq=wq, wqs=wqs, wk=wk, wv=wv, wv_slot=wv_slot, vone=vone, mabs=mabs,
        sgg=sg_norm_g[l].reshape(1, SG_WIDTH), sgb=sg_norm_b[l].reshape(1, SG_WIDTH),
        w_pa=w_pa[l].astype(BF16), w_pb=w_pb[l].astype(BF16), w_o=w_o[l].astype(BF16),
        gffn=norm_ffn_g[l].reshape(1, d), rw_hi=rw_hi, rw_lo=rw_lo, rb=rb)


def _spatial_mix_weights(w_s, b_s, seq, n_batch):
    gw = SG_WIDTH // SG_GROUPS
    tril = jnp.tril(jnp.ones((SG_CHUNK, SG_CHUNK), dtype=bool))
    w = jnp.where(tril[None], w_s, 0.0)
    if seq % SG_CHUNK == 0:
        mixw = w
        bias_t = b_s
    else:
        assert seq < SG_CHUNK
        blk = w[:, :seq, :seq]
        eye = jnp.eye(n_batch, dtype=F32)
        mixw = jnp.einsum("ab,gts->gatbs", eye, blk).reshape(SG_GROUPS, n_batch * seq, n_batch * seq)
        bias_t = jnp.tile(b_s[:, :seq], (1, n_batch))
    bias = jnp.repeat(jnp.transpose(bias_t), gw, axis=1)
    return mixw.astype(BF16), bias


def _routing_tables(idx, rank, counts_f, n_blocks):
    counts = counts_f[0, :N_EXPERTS].astype(jnp.int32)
    padded = (counts + MOE_ROWS - 1) // MOE_ROWS * MOE_ROWS
    pend = jnp.cumsum(padded).astype(jnp.int32)
    pstart = pend - padded
    experts = jnp.arange(N_EXPERTS, dtype=jnp.int32)
    dest = rank + jnp.sum(jnp.where(idx[..., None] == experts, pstart, 0), axis=-1)
    block_start = jnp.arange(n_blocks, dtype=jnp.int32) * MOE_ROWS
    block_e = jnp.minimum(jnp.sum((pend[None, :] <= block_start[:, None]).astype(jnp.int32), axis=1), N_EXPERTS - 1)
    n_used = (pend[-1:] // MOE_ROWS).astype(jnp.int32)
    return dest.astype(jnp.int32), pend, block_e.astype(jnp.int32), n_used


def _pos_tiles(pos, tm):
    t = pos.shape[0]
    return jnp.transpose(pos.reshape(t // tm, tm, TOP_K), (0, 2, 1)).reshape(t // tm, TOP_K * tm)


def _pick_tile(n, pref):
    t = min(n, pref)
    assert n % t == 0 and t % 8 == 0
    return t


def kernel(x_prompt, x_sample, cache_ckv, cache_krope, c_prompt, c_sample, ada_w, ada_b, norm_mix_g, w_in, q_norm_g, w_uq, kv_norm_g, w_uk, w_uv, w_pa, sg_norm_g, sg_norm_b, w_spatial, b_spatial, w_pb, w_o, norm_ffn_g, router_w, router_b, w_gu, b_gu, w_dn, b_dn, final_g):
    bp, lp, d = x_prompt.shape
    bs, ls, _ = x_sample.shape
    depth = w_in.shape[0]
    past = cache_ckv.shape[2]
    tp, ts = bp * lp, bs * ls
    assert lp % SG_CHUNK == 0 and ls <= SG_CHUNK

    tm_p = _pick_tile(lp, ROW_TILE)
    tm_s = _pick_tile(ts, SAMPLE_TILE)
    t_attn = _pick_tile(lp, ATTN_TILE)
    tpb = lp // tm_p

    cc_p, ss_p = _rope_tables(jnp.arange(lp, dtype=jnp.int32))
    cc_s, ss_s = _rope_tables(past + jnp.arange(ls, dtype=jnp.int32))
    cc_s, ss_s = jnp.tile(cc_s, (bs, 1)), jnp.tile(ss_s, (bs, 1))

    b_all = bp + bs
    b_pad = -(-b_all // 8) * 8
    c_all = jnp.concatenate([c_prompt, c_sample, jnp.zeros((b_pad - b_all, d), F32)], axis=0)

    xp = x_prompt.reshape(tp, d)
    xs = x_sample.reshape(ts, d)
    outs = dict(ckv_p=[], kr_p=[], ckv_s=[], kr_s=[], v_s=[])
    final_g2 = final_g.reshape(1, d)
    for l in range(depth):
        wts = _layer_weights(l, w_in, norm_mix_g, q_norm_g, w_uq, kv_norm_g, w_uk, w_uv, w_pa, sg_norm_g,
                             sg_norm_b, w_pb, w_o, norm_ffn_g, router_w, router_b)
        mod = _adaln(c_all, ada_w[l], ada_b[l])
        mod_p = [mod[:bp, j * d:(j + 1) * d].reshape(bp, 1, d) for j in range(6)]
        mod_s = [jnp.repeat(mod[bp:b_all, j * d:(j + 1) * d], ls, axis=0) for j in range(6)]

        q, k, v, ckv, kr, u, vn, ga, gb = _inproj(xp, mod_p[0], mod_p[1], cc_p, ss_p, wts, tm=tm_p, per_row=False,
                                                  tiles_per_batch=tpb, vn_dtype=BF16)
        o = _attn_prompt(q.reshape(bp, lp, QK_WIDTH), k.reshape(bp, lp, QK_WIDTH), v.reshape(bp, lp, QK_WIDTH),
                         tq=t_attn, tk=t_attn).reshape(tp, V_WIDTH)
        mixw, bias = _spatial_mix_weights(w_spatial[l], b_spatial[l], lp, bp)
        x1p = _merge(o, u, vn, ga, gb, xp, mod_p[2], mixw, bias, wts, tm=tm_p, chunk=SG_CHUNK, per_row=False,
                     tiles_per_batch=tpb)
        outs["ckv_p"].append(ckv.reshape(bp, lp, KV_RANK))
        outs["kr_p"].append(kr.reshape(bp, lp, QK_ROPE))

        q, k, v, ckv, kr, u, vn, ga, gb = _inproj(xs, mod_s[0], mod_s[1], cc_s, ss_s, wts, tm=tm_s, per_row=True,
                                                  tiles_per_batch=1, vn_dtype=F32)
        ckv3, kr3 = ckv.reshape(bs, ls, KV_RANK), kr.reshape(bs, ls, QK_ROPE)
        o = _attn_sample(q.reshape(bs, ls, QK_WIDTH), cache_ckv[l], cache_krope[l], ckv3, kr3,
                         wts["mabs"], wts["wv"]).reshape(ts, V_WIDTH)
        mixw, bias = _spatial_mix_weights(w_spatial[l], b_spatial[l], ls, tm_s // ls)
        x1s = _merge(o, u, vn, ga, gb, xs, mod_s[2], mixw, bias, wts, tm=tm_s, chunk=tm_s, per_row=True,
                     tiles_per_batch=1)
        outs["ckv_s"].append(ckv3)
        outs["kr_s"].append(kr3)
        outs["v_s"].append(vn.reshape(bs, ls, SG_WIDTH))

        t_all = tp + ts
        tm_r = _pick_tile(math.gcd(tp, ts), ROW_TILE)
        h2_all, idx, gate, rank, counts = _router(x1p, mod_p[3], mod_p[4], x1s, mod_s[3], mod_s[4], wts, tm=tm_r,
                                                  tiles_per_batch=lp // tm_r)
        gate_p, gate_s = gate[:tp], gate[tp:]
        n_blocks = -(-(t_all * TOP_K) // MOE_ROWS) + N_EXPERTS
        pos, pend, block_e, n_used = _routing_tables(idx, rank, counts, n_blocks)
        x_sorted = _dispatch(pend, n_used, _pos_tiles(pos, tm_r), h2_all, tm=tm_r, n_blocks=n_blocks)
        ys = _moe_experts(block_e, n_used, x_sorted, w_gu[l], b_gu[l], w_dn[l], b_dn[l])
        last = l == depth - 1
        xp = _combine(_pos_tiles(pos[:tp], tm_r), ys, x1p, gate_p, mod_p[5], final_g2, tm=tm_r, per_row=False,
                      tiles_per_batch=lp // tm_r, final_norm=last)
        xs = _combine(_pos_tiles(pos[tp:], tm_r), ys, x1s, gate_s, mod_s[5], final_g2, tm=tm_r, per_row=True,
                      tiles_per_batch=1, final_norm=last)
    return (xp.reshape(bp, lp, d), xs.reshape(bs, ls, d),
            jnp.stack(outs["ckv_p"]), jnp.stack(outs["kr_p"]),
            jnp.stack(outs["ckv_s"]), jnp.stack(outs["kr_s"]), jnp.stack(outs["v_s"]))
```

```python
import functools
import math

import jax
import jax.numpy as jnp
from jax import lax
from jax.experimental import pallas as pl
from jax.experimental.pallas import tpu as pltpu

F32 = jnp.float32
BF16 = jnp.bfloat16

LANES = 128
VMEM_LIMIT_BYTES = 56 * 1024 * 1024

CHUNK = 64
CHUNK_SHIFT = 6
MLA_HEADS = 8
QK_NOPE = 64
QK_ROPE = 32
V_HEAD = 64
V_HEAD_SHIFT = 6
Q_RANK = 384
KV_RANK = 256
ROPE_THETA = 10000.0
ATTN_SCALE = 1.0 / math.sqrt(QK_NOPE + QK_ROPE)
LOG2_E = math.log2(math.e)
SG_CHUNK = 128
SG_GROUPS = 4
SG_WIDTH = 512
N_EXPERTS = 32
TOP_K = 4
SWIGLU_LIMIT = 7.0
SWIGLU_ALPHA = 1.702
EPS = 1e-6

HEAD_SLOT = LANES
QK_WIDTH = MLA_HEADS * HEAD_SLOT
V_WIDTH = MLA_HEADS * V_HEAD
MOE_ROWS = 512
ROW_TILE = 256
SAMPLE_TILE = 512
ATTN_TILE = 512
ATTN_SUB_KEYS = 256
ROUTER_PAD = LANES
NEG_BIG = -1e30

_C_CQ = 0
_C_CKV = _C_CQ + Q_RANK
_C_U = _C_CKV + KV_RANK
_C_V = _C_U + SG_WIDTH
_C_GA = _C_V + SG_WIDTH


def _params(sem):
    return pltpu.CompilerParams(dimension_semantics=sem, vmem_limit_bytes=VMEM_LIMIT_BYTES)


def _dot(a, b):
    return jnp.dot(a, b, preferred_element_type=F32)


def _dot_nt(a, b):
    return lax.dot_general(a, b, (((1,), (1,)), ((), ())), preferred_element_type=F32)


def _rms(x, g):
    return x * lax.rsqrt(jnp.mean(x * x, axis=-1, keepdims=True) + EPS) * g


def _adaln_kernel(c_ref, w_ref, b_ref, o_ref):
    c = c_ref[...]
    s = (c * jax.nn.sigmoid(c)).astype(BF16)
    o_ref[...] = _dot(s, w_ref[...].astype(BF16)) + b_ref[...]


def _adaln(c_all, ada_w, ada_b):
    bp, d = c_all.shape
    n = ada_w.shape[1]
    return pl.pallas_call(
        _adaln_kernel,
        grid=(n // d,),
        in_specs=[pl.BlockSpec((bp, d), lambda j: (0, 0)),
                  pl.BlockSpec((d, d), lambda j: (0, j)),
                  pl.BlockSpec((1, d), lambda j: (0, j))],
        out_specs=pl.BlockSpec((bp, d), lambda j: (0, j)),
        out_shape=jax.ShapeDtypeStruct((bp, n), F32),
        compiler_params=_params(("arbitrary",)),
        name="adaln",
    )(c_all, ada_w, ada_b.reshape(1, n))


def _inproj_kernel(x_ref, sh_ref, sc_ref, gmix_ref, cc_ref, ss_ref, win_ref, gq_ref, gkv_ref,
                   wq_ref, wqs_ref, wk_ref, wv_ref, vone_ref, sgg_ref, sgb_ref,
                   q_ref, k_ref, v_ref, ckv_ref, kr_ref, u_ref, vn_ref, ga_ref, gb_ref, *, d_model):
    x = x_ref[...]
    h = (_rms(x, gmix_ref[...]) * (1.0 + sc_ref[...]) + sh_ref[...]).astype(BF16)

    def proj(lo, width):
        return _dot(h, win_ref[:, lo:lo + width])

    cc = cc_ref[...]
    ss = ss_ref[...]
    c_gb = _C_GA + d_model
    c_kra = c_gb + d_model
    c_krb = c_kra + LANES

    cqn = _rms(proj(_C_CQ, Q_RANK), gq_ref[...]).astype(BF16)
    qa = _dot(cqn, wq_ref[...])
    qb = _dot(cqn, wqs_ref[...])
    for hd in range(MLA_HEADS):
        sl = slice(hd * HEAD_SLOT, (hd + 1) * HEAD_SLOT)
        q_ref[:, sl] = (qa[:, sl] * cc + qb[:, sl] * ss).astype(BF16)

    ckvn = _rms(proj(_C_CKV, KV_RANK), gkv_ref[...])
    ckv_ref[...] = ckvn
    ckvb = ckvn.astype(BF16)
    krs = proj(c_kra, LANES) * cc + proj(c_krb, LANES) * ss
    kr_ref[...] = krs[:, :QK_ROPE]
    kn = _dot(ckvb, wk_ref[...])
    for hd in range(MLA_HEADS):
        sl = slice(hd * HEAD_SLOT, (hd + 1) * HEAD_SLOT)
        k_ref[:, sl] = (kn[:, sl] + krs).astype(BF16)
    v_ref[...] = (_dot(ckvb, wv_ref[...]) + vone_ref[...]).astype(BF16)

    u_ref[...] = proj(_C_U, SG_WIDTH).astype(u_ref.dtype)
    vv = proj(_C_V, SG_WIDTH)
    mu = jnp.mean(vv, axis=-1, keepdims=True)
    vc = vv - mu
    var = jnp.mean(vc * vc, axis=-1, keepdims=True)
    vn_ref[...] = (vc * lax.rsqrt(var + EPS) * sgg_ref[...] + sgb_ref[...]).astype(vn_ref.dtype)
    ga_ref[...] = proj(_C_GA, d_model).astype(BF16)
    gb_ref[...] = proj(c_gb, d_model).astype(BF16)


def _mod_spec(per_row, tm, d, tiles_per_batch):
    if per_row:
        return pl.BlockSpec((tm, d), lambda i: (i, 0))
    return pl.BlockSpec((None, 1, d), lambda i: (i // tiles_per_batch, 0, 0))


def _const_spec(shape):
    nd = len(shape)
    return pl.BlockSpec(shape, lambda i: (0,) * nd)


def _inproj(x2d, shift, scale, cc, ss, wts, *, tm, per_row, tiles_per_batch, vn_dtype):
    t, d = x2d.shape
    n_tab = cc.shape[0] // tm
    row = lambda w: pl.BlockSpec((tm, w), lambda i: (i, 0))
    tab = pl.BlockSpec((tm, LANES), lambda i: (i % n_tab, 0))
    mod = _mod_spec(per_row, tm, d, tiles_per_batch)
    consts = [wts["w_in_r"], wts["gq"], wts["gkv"], wts["wq"], wts["wqs"], wts["wk"], wts["wv_slot"],
              wts["vone"], wts["sgg"], wts["sgb"]]
    out_shapes = [jax.ShapeDtypeStruct((t, QK_WIDTH), BF16), jax.ShapeDtypeStruct((t, QK_WIDTH), BF16),
                  jax.ShapeDtypeStruct((t, QK_WIDTH), BF16), jax.ShapeDtypeStruct((t, KV_RANK), F32),
                  jax.ShapeDtypeStruct((t, QK_ROPE), F32), jax.ShapeDtypeStruct((t, SG_WIDTH), BF16),
                  jax.ShapeDtypeStruct((t, SG_WIDTH), vn_dtype), jax.ShapeDtypeStruct((t, d), BF16),
                  jax.ShapeDtypeStruct((t, d), BF16)]
    return pl.pallas_call(
        functools.partial(_inproj_kernel, d_model=d),
        grid=(t // tm,),
        in_specs=[row(d), mod, mod, _const_spec((1, d)), tab, tab] + [_const_spec(c.shape) for c in consts],
        out_specs=[row(s.shape[1]) for s in out_shapes],
        out_shape=out_shapes,
        compiler_params=_params(("arbitrary",)),
        name="inproj",
    )(x2d, shift, scale, wts["gmix"], cc, ss, *consts)


def _attn_kernel(qi_ref, kj_ref, flag_ref, q_ref, k_ref, v_ref, o_ref, m_sc, acc_sc, *, tq, tk, sub):
    s_id = pl.program_id(1)
    qi = qi_ref[s_id]
    kj = kj_ref[s_id]
    flags = flag_ref[s_id]

    @pl.when(kj == 0)
    def _():
        m_sc[...] = jnp.full(m_sc.shape, -jnp.inf, F32)
        acc_sc[...] = jnp.zeros(acc_sc.shape, F32)

    def sweep(bias):
        for kb in range(tk // sub):
            keys = slice(kb * sub, (kb + 1) * sub)
            for hd in range(MLA_HEADS):
                sl = slice(hd * HEAD_SLOT, (hd + 1) * HEAD_SLOT)
                s = _dot_nt(q_ref[:, sl], k_ref[keys, sl])
                if bias is not None:
                    s = s + bias[:, keys]
                tiles = [s[:, c * LANES:(c + 1) * LANES] for c in range(sub // LANES)]
                m_tile = tiles[0]
                for t in tiles[1:]:
                    m_tile = jnp.maximum(m_tile, t)
                m_old = m_sc[hd]
                m_new = jnp.maximum(m_old, jnp.max(m_tile, axis=-1, keepdims=True))
                alpha = jnp.exp2(m_old - m_new)
                p = jnp.concatenate([jnp.exp2(t - m_new).astype(BF16) for t in tiles], axis=1)
                acc_sc[hd] = alpha * acc_sc[hd] + _dot(p, v_ref[keys, sl])
                m_sc[hd] = m_new

    @pl.when((flags & 2) == 0)
    def _():
        sweep(None)

    @pl.when((flags & 2) != 0)
    def _():
        row = lax.broadcasted_iota(jnp.int32, (tq, tk), 0) + qi * tq
        col = lax.broadcasted_iota(jnp.int32, (tq, tk), 1) + kj * tk
        sweep(jnp.where((col >> CHUNK_SHIFT) <= (row >> CHUNK_SHIFT), 0.0, -jnp.inf))

    @pl.when((flags & 1) != 0)
    def _():
        lane = lax.broadcasted_iota(jnp.int32, (tq, LANES), 1)
        for pr in range(MLA_HEADS // 2):
            outs = []
            for hd in (2 * pr, 2 * pr + 1):
                acc = acc_sc[hd]
                outs.append(acc / pltpu.roll(acc, V_HEAD, axis=1))
            pair = jnp.where(lane < V_HEAD, outs[0], pltpu.roll(outs[1], V_HEAD, axis=1))
            o_ref[:, pr * LANES:(pr + 1) * LANES] = pair.astype(BF16)


def _attn_prompt(q, k, v, *, tq, tk):
    b, l, _ = q.shape
    nq = l // tq
    qi_l, kj_l, flag_l = [], [], []
    for i in range(nq):
        n_kv = ((i + 1) * tq - 1) // tk + 1
        for j in range(n_kv):
            qi_l.append(i)
            kj_l.append(j)
            masked = ((j + 1) * tk - 1) // CHUNK > (i * tq) // CHUNK
            flag_l.append((1 if j == n_kv - 1 else 0) | (2 if masked else 0))
    steps = len(qi_l)
    grid_spec = pltpu.PrefetchScalarGridSpec(
        num_scalar_prefetch=3,
        grid=(b, steps),
        in_specs=[pl.BlockSpec((None, tq, QK_WIDTH), lambda bi, s, qi, kj, fl: (bi, qi[s], 0)),
                  pl.BlockSpec((None, tk, QK_WIDTH), lambda bi, s, qi, kj, fl: (bi, kj[s], 0)),
                  pl.BlockSpec((None, tk, QK_WIDTH), lambda bi, s, qi, kj, fl: (bi, kj[s], 0))],
        out_specs=pl.BlockSpec((None, tq, V_WIDTH), lambda bi, s, qi, kj, fl: (bi, qi[s], 0)),
        scratch_shapes=[pltpu.VMEM((MLA_HEADS, tq, LANES), F32), pltpu.VMEM((MLA_HEADS, tq, LANES), F32)],
    )
    return pl.pallas_call(
        functools.partial(_attn_kernel, tq=tq, tk=tk, sub=min(tk, ATTN_SUB_KEYS)),
        grid_spec=grid_spec,
        out_shape=jax.ShapeDtypeStruct((b, l, V_WIDTH), BF16),
        compiler_params=_params(("arbitrary", "arbitrary")),
        name="attn_prompt",
    )(jnp.asarray(qi_l, jnp.int32), jnp.asarray(kj_l, jnp.int32), jnp.asarray(flag_l, jnp.int32), q, k, v)


def _attn_sample_kernel(q_ref, pckv_ref, pkr_ref, nckv_ref, nkr_ref, mabs_ref, wv_ref, o_ref, *, ls, past):
    hl = MLA_HEADS * ls
    qcat = jnp.concatenate(
        [_dot(q_ref[:, hd * HEAD_SLOT:(hd + 1) * HEAD_SLOT], mabs_ref[hd]) for hd in range(MLA_HEADS)],
        axis=0).astype(BF16)
    q_abs = qcat[:, :KV_RANK]
    q_rope = qcat[:, KV_RANK:]
    pckv = pckv_ref[...].astype(BF16)
    nckv = nckv_ref[...].astype(BF16)

    def pad_lanes(kr):
        return jnp.concatenate([kr, jnp.zeros((kr.shape[0], LANES - QK_ROPE), kr.dtype)], axis=1).astype(BF16)

    s_past = _dot_nt(q_abs, pckv) + _dot_nt(q_rope, pad_lanes(pkr_ref[...]))
    s_new = _dot_nt(q_abs, nckv) + _dot_nt(q_rope, pad_lanes(nkr_ref[...]))

    qpos_1 = lax.broadcasted_iota(jnp.int32, (ls, 1), 0) + past
    qchunk = jnp.concatenate([qpos_1] * MLA_HEADS, axis=0) >> CHUNK_SHIFT
    kchunk_past = lax.broadcasted_iota(jnp.int32, (hl, past), 1) >> CHUNK_SHIFT
    kchunk_new = (lax.broadcasted_iota(jnp.int32, (hl, ls), 1) + past) >> CHUNK_SHIFT
    s_past = jnp.where(kchunk_past <= qchunk, s_past, -jnp.inf)
    s_new = jnp.where(kchunk_new <= qchunk, s_new, -jnp.inf)

    m = jnp.maximum(jnp.max(s_past, axis=-1, keepdims=True), jnp.max(s_new, axis=-1, keepdims=True))
    p_past = jnp.exp2(s_past - m)
    p_new = jnp.exp2(s_new - m)
    denom = jnp.sum(p_past, axis=-1, keepdims=True) + jnp.sum(p_new, axis=-1, keepdims=True)
    olat = (_dot(p_past.astype(BF16), pckv) + _dot(p_new.astype(BF16), nckv)) / denom
    ofull = _dot(olat.astype(BF16), wv_ref[...])
    col_head = lax.broadcasted_iota(jnp.int32, (ls, V_WIDTH), 1) >> V_HEAD_SHIFT
    out = jnp.zeros((ls, V_WIDTH), F32)
    for hd in range(MLA_HEADS):
        out = out + jnp.where(col_head == hd, ofull[hd * ls:(hd + 1) * ls], 0.0)
    o_ref[...] = out.astype(BF16)


def _attn_sample(q, past_ckv, past_kr, new_ckv, new_kr, mabs, wv):
    b, ls, _ = q.shape
    past = past_ckv.shape[1]
    blk = lambda n, w: pl.BlockSpec((None, n, w), lambda i: (i, 0, 0))
    return pl.pallas_call(
        functools.partial(_attn_sample_kernel, ls=ls, past=past),
        grid=(b,),
        in_specs=[blk(ls, QK_WIDTH), blk(past, KV_RANK), blk(past, QK_ROPE), blk(ls, KV_RANK), blk(ls, QK_ROPE),
                  _const_spec(mabs.shape), _const_spec(wv.shape)],
        out_specs=blk(ls, V_WIDTH),
        out_shape=jax.ShapeDtypeStruct((b, ls, V_WIDTH), BF16),
        compiler_params=_params(("arbitrary",)),
        name="attn_sample",
    )(q, past_ckv, past_kr, new_ckv, new_kr, mabs, wv)


def _merge_kernel(o_ref, u_ref, vn_ref, ga_ref, gb_ref, x_ref, gate_ref, mix_ref, bias_ref,
                  wpa_ref, wpb_ref, wo_ref, x1_ref, sg_sc, *, chunk):
    tm = x_ref.shape[0]
    gw = SG_WIDTH // SG_GROUPS
    for c in range(tm // chunk):
        rows = slice(c * chunk, (c + 1) * chunk)
        for g in range(SG_GROUPS):
            cols = slice(g * gw, (g + 1) * gw)
            mixed = _dot(mix_ref[g], vn_ref[rows, cols].astype(BF16)) + bias_ref[:, cols]
            sg_sc[rows, cols] = (u_ref[rows, cols].astype(F32) * mixed).astype(BF16)
    ya = _dot(o_ref[...], wpa_ref[...])
    yb = _dot(sg_sc[...], wpb_ref[...])
    m = jax.nn.sigmoid(ga_ref[...].astype(F32)) * ya + jax.nn.sigmoid(gb_ref[...].astype(F32)) * yb
    x1_ref[...] = x_ref[...] + gate_ref[...] * _dot(m.astype(BF16), wo_ref[...])


def _merge(o, u, vn, ga, gb, x2d, gate, mixw, bias, wts, *, tm, chunk, per_row, tiles_per_batch):
    t, d = x2d.shape
    row = lambda w: pl.BlockSpec((tm, w), lambda i: (i, 0))
    consts = [mixw, bias, wts["w_pa"], wts["w_pb"], wts["w_o"]]
    return pl.pallas_call(
        functools.partial(_merge_kernel, chunk=chunk),
        grid=(t // tm,),
        in_specs=[row(V_WIDTH), row(SG_WIDTH), row(SG_WIDTH), row(d), row(d), row(d),
                  _mod_spec(per_row, tm, d, tiles_per_batch)] + [_const_spec(c.shape) for c in consts],
        out_specs=row(d),
        out_shape=jax.ShapeDtypeStruct((t, d), F32),
        scratch_shapes=[pltpu.VMEM((tm, SG_WIDTH), BF16)],
        compiler_params=_params(("arbitrary",)),
        name="merge",
    )(o, u, vn, ga, gb, x2d, gate, *consts)


def _router_kernel(xp_ref, shp_ref, scp_ref, xs_ref, shs_ref, scs_ref, g_ref, whi_ref, wlo_ref, rb_ref,
                   h2_ref, idx_ref, gate_ref, rank_ref, cnt_ref, carry_sc, *, n_prompt_tiles):
    i = pl.program_id(0)
    out_refs = (g_ref, whi_ref, wlo_ref, rb_ref, h2_ref, idx_ref, gate_ref, rank_ref, cnt_ref, carry_sc)

    @pl.when(i == 0)
    def _():
        carry_sc[...] = jnp.zeros(carry_sc.shape, F32)

    @pl.when(i < n_prompt_tiles)
    def _():
        _route_rows(xp_ref, shp_ref, scp_ref, *out_refs)

    @pl.when(i >= n_prompt_tiles)
    def _():
        _route_rows(xs_ref, shs_ref, scs_ref, *out_refs)


def _route_rows(x1_ref, sh_ref, sc_ref, g_ref, whi_ref, wlo_ref, rb_ref, h2_ref, idx_ref, gate_ref, rank_ref,
                cnt_ref, carry_sc):
    h2 = _rms(x1_ref[...], g_ref[...]) * (1.0 + sc_ref[...]) + sh_ref[...]
    h2_ref[...] = h2
    hi = h2.astype(BF16)
    lo = (h2 - hi.astype(F32)).astype(BF16)
    logits = _dot(hi, whi_ref[...]) + _dot(lo, whi_ref[...]) + _dot(hi, wlo_ref[...]) + rb_ref[...]
    lane = lax.broadcasted_iota(jnp.int32, logits.shape, 1)
    vals, idxs = [], []
    work = logits
    for _ in range(TOP_K):
        mx = jnp.max(work, axis=-1, keepdims=True)
        ix = jnp.min(jnp.where(work == mx, lane, ROUTER_PAD), axis=-1, keepdims=True)
        vals.append(mx)
        idxs.append(ix)
        work = jnp.where(lane == ix, -jnp.inf, work)
    es = [jnp.exp(v - vals[0]) for v in vals]
    tot = es[0]
    for e in es[1:]:
        tot = tot + e
    idx_w = jnp.zeros(logits.shape, jnp.int32)
    gate_w = jnp.zeros(logits.shape, F32)
    for j in range(TOP_K):
        idx_w = jnp.where(lane == j, idxs[j], idx_w)
        gate_w = jnp.where(lane == j, es[j] / tot, gate_w)
    idx_ref[...] = idx_w[:, :TOP_K]
    gate_ref[...] = gate_w[:, :TOP_K]

    tm = logits.shape[0]
    onehot = jnp.zeros(logits.shape, F32)
    for j in range(TOP_K):
        onehot = jnp.where(lane == idxs[j], 1.0, onehot)
    earlier = (lax.broadcasted_iota(jnp.int32, (tm, tm), 1) < lax.broadcasted_iota(jnp.int32, (tm, tm), 0))
    within = _dot(jnp.where(earlier, 1.0, 0.0).astype(BF16), onehot.astype(BF16))
    rank_full = within + carry_sc[...]
    rank_w = jnp.zeros(logits.shape, jnp.int32)
    for j in range(TOP_K):
        rj = jnp.sum(jnp.where(lane == idxs[j], rank_full, 0.0), axis=-1, keepdims=True)
        rank_w = jnp.where(lane == j, rj.astype(jnp.int32), rank_w)
    rank_ref[...] = rank_w[:, :TOP_K]
    carry_sc[...] = carry_sc[...] + jnp.sum(onehot, axis=0, keepdims=True)
    cnt_ref[...] = carry_sc[...]


def _router(x1p, shift_p, scale_p, x1s, shift_s, scale_s, wts, *, tm, tiles_per_batch):
    tp, d = x1p.shape
    ts = x1s.shape[0]
    n_p, n_s = tp // tm, ts // tm
    t_all = tp + ts
    row = lambda w: pl.BlockSpec((tm, w), lambda i: (i, 0))
    p_row = pl.BlockSpec((tm, d), lambda i: (jnp.minimum(i, n_p - 1), 0))
    p_mod = pl.BlockSpec((None, 1, d), lambda i: (jnp.minimum(i, n_p - 1) // tiles_per_batch, 0, 0))
    s_row = pl.BlockSpec((tm, d), lambda i: (jnp.maximum(i - n_p, 0), 0))
    consts = [wts["gffn"], wts["rw_hi"], wts["rw_lo"], wts["rb"]]
    return pl.pallas_call(
        functools.partial(_router_kernel, n_prompt_tiles=n_p),
        grid=(n_p + n_s,),
        in_specs=[p_row, p_mod, p_mod, s_row, s_row, s_row] + [_const_spec(c.shape) for c in consts],
        out_specs=[row(d), row(TOP_K), row(TOP_K), row(TOP_K), _const_spec((1, ROUTER_PAD))],
        out_shape=[jax.ShapeDtypeStruct((t_all, d), F32), jax.ShapeDtypeStruct((t_all, TOP_K), jnp.int32),
                   jax.ShapeDtypeStruct((t_all, TOP_K), F32), jax.ShapeDtypeStruct((t_all, TOP_K), jnp.int32),
                   jax.ShapeDtypeStruct((1, ROUTER_PAD), F32)],
        scratch_shapes=[pltpu.VMEM((1, ROUTER_PAD), F32)],
        compiler_params=_params(("arbitrary",)),
        name="router",
    )(x1p, shift_p, scale_p, x1s, shift_s, scale_s, *consts)


ROW_GROUP = 8


def _slot_offset(slot, n_rows):
    return slot * n_rows if isinstance(slot, int) else pl.multiple_of(slot * n_rows, n_rows)


def _row_gather_start(idx_smem, slot, src_hbm, dst_vmem, sem, n_rows):
    base = _slot_offset(slot, n_rows)

    def group(g, carry):
        r0 = g * ROW_GROUP
        for j in range(ROW_GROUP):
            pltpu.make_async_copy(src_hbm.at[pl.ds(idx_smem[base + r0 + j], 1)], dst_vmem.at[g, pl.ds(j, 1)],
                                  sem).start()
        return carry
    lax.fori_loop(0, n_rows // ROW_GROUP, group, 0)


def _row_gather_wait(dst_vmem, sem):
    pltpu.make_async_copy(dst_vmem, dst_vmem, sem).wait()


def _gather_pipeline(i, n_steps, idx_hbm, idx_smem, isem, src_hbm, buf, gsem, n_rows):
    def idx_copy(blk, slot):
        return pltpu.make_async_copy(idx_hbm.at[blk], idx_smem.at[pl.ds(_slot_offset(slot, n_rows), n_rows)],
                                     isem.at[slot])

    @pl.when(i == 0)
    def _():
        idx_copy(0, 0).start()
        idx_copy(0, 0).wait()
        _row_gather_start(idx_smem, 0, src_hbm, buf.at[0], gsem.at[0], n_rows)

        @pl.when(n_steps > 1)
        def _():
            idx_copy(1, 1).start()

    nxt = (i + 1) % 2

    @pl.when(i + 1 < n_steps)
    def _():
        idx_copy(i + 1, nxt).wait()
        _row_gather_start(idx_smem, nxt, src_hbm, buf.at[nxt], gsem.at[nxt], n_rows)

    @pl.when(i + 2 < n_steps)
    def _():
        idx_copy(i + 2, i % 2).start()

    _row_gather_wait(buf.at[i % 2], gsem.at[i % 2])


def _dispatch_kernel(pend_ref, nused_ref, dest_ref, h2_ref, xs_ref, zbuf, idx_smem, isem, csem, zsem, *,
                     tm, n_blocks):
    i = pl.program_id(0)
    n = pl.num_programs(0)
    rows = TOP_K * tm

    def idx_copy(blk, slot):
        return pltpu.make_async_copy(dest_ref.at[blk], idx_smem.at[pl.ds(_slot_offset(slot, rows), rows)],
                                     isem.at[slot])

    def zero_copy(block_start):
        start = pl.multiple_of(block_start, MOE_ROWS)
        return pltpu.make_async_copy(zbuf, xs_ref.at[pl.ds(start, MOE_ROWS)], zsem)

    def rows_done():
        return pltpu.make_async_copy(xs_ref.at[pl.ds(0, rows)], xs_ref.at[pl.ds(0, rows)], csem)

    @pl.when(i == 0)
    def _():
        idx_copy(0, 0).start()
        zbuf[...] = jnp.zeros(zbuf.shape, F32)
        n_used = nused_ref[0]

        def last_block(e, carry):
            zero_copy(jnp.maximum(pend_ref[e] - MOE_ROWS, 0)).start()
            return carry
        lax.fori_loop(0, N_EXPERTS, last_block, 0)

        def tail_block(b, carry):
            zero_copy(b * MOE_ROWS).start()
            return carry
        lax.fori_loop(n_used, n_blocks, tail_block, 0)

        def drain(b, carry):
            zero_copy(0).wait()
            return carry
        lax.fori_loop(0, N_EXPERTS + n_blocks - n_used, drain, 0)

    slot = i % 2
    idx_copy(i, slot).wait()

    @pl.when(i + 1 < n)
    def _():
        idx_copy(i + 1, 1 - slot).start()

    base = _slot_offset(slot, rows)

    def group(g, carry):
        r0 = g * ROW_GROUP
        for j in range(ROW_GROUP):
            src = h2_ref.at[g, pl.ds(j, 1)]
            for kk in range(TOP_K):
                dst = xs_ref.at[pl.ds(idx_smem[base + kk * tm + r0 + j], 1)]
                pltpu.make_async_copy(src, dst, csem).start()
        return carry
    lax.fori_loop(0, tm // ROW_GROUP, group, 0)
    rows_done().wait()


def _dispatch(pend, n_used, dest_tiles, h2_all, *, tm, n_blocks):
    t_all, d = h2_all.shape
    grid_spec = pltpu.PrefetchScalarGridSpec(
        num_scalar_prefetch=2,
        grid=(t_all // tm,),
        in_specs=[pl.BlockSpec(memory_space=pl.ANY),
                  pl.BlockSpec((tm // ROW_GROUP, ROW_GROUP, d), lambda i, pe, nu: (i, 0, 0))],
        out_specs=pl.BlockSpec(memory_space=pl.ANY),
        scratch_shapes=[pltpu.VMEM((MOE_ROWS, d), F32),
                        pltpu.SMEM((2 * TOP_K * tm,), jnp.int32),
                        pltpu.SemaphoreType.DMA((2,)),
                        pltpu.SemaphoreType.DMA(()),
                        pltpu.SemaphoreType.DMA(())],
    )
    return pl.pallas_call(
        functools.partial(_dispatch_kernel, tm=tm, n_blocks=n_blocks),
        grid_spec=grid_spec,
        out_shape=jax.ShapeDtypeStruct((n_blocks * MOE_ROWS, d), F32),
        compiler_params=_params(("arbitrary",)),
        name="dispatch",
    )(pend, n_used, dest_tiles, h2_all.reshape(t_all // ROW_GROUP, ROW_GROUP, d))


def _moe_kernel(be_ref, nused_ref, xs_ref, wgu_ref, bgu_ref, wdn_ref, bdn_ref, y_ref, wgu_bf, wdn_bf, *, d_model):
    i = pl.program_id(0)
    n_used = nused_ref[0]

    @pl.when(i < n_used)
    def _():
        prev = be_ref[jnp.maximum(i - 1, 0)]

        @pl.when((i == 0) | (be_ref[i] != prev))
        def _():
            wgu_bf[...] = wgu_ref[...].astype(BF16)
            wdn_bf[...] = wdn_ref[...].astype(BF16)

        xb = xs_ref[...].astype(BF16)
        gu = _dot(xb, wgu_bf[...]) + bgu_ref[...]
        g = jnp.minimum(gu[:, :d_model], SWIGLU_LIMIT)
        lin = jnp.clip(gu[:, d_model:], -SWIGLU_LIMIT, SWIGLU_LIMIT)
        act = g * jax.nn.sigmoid(SWIGLU_ALPHA * g) * (lin + 1.0)
        y_ref[...] = _dot(act.astype(BF16), wdn_bf[...]) + bdn_ref[...]

    @pl.when(i >= n_used)
    def _():
        y_ref[...] = jnp.zeros(y_ref.shape, F32)


def _moe_experts(block_e, n_used, xs, w_gu, b_gu, w_dn, b_dn):
    n_blocks = xs.shape[0] // MOE_ROWS
    e, d, d2 = w_gu.shape
    grid_spec = pltpu.PrefetchScalarGridSpec(
        num_scalar_prefetch=2,
        grid=(n_blocks,),
        in_specs=[pl.BlockSpec((MOE_ROWS, d), lambda i, be, nu: (jnp.minimum(i, nu[0] - 1), 0)),
                  pl.BlockSpec((None, d, d2), lambda i, be, nu: (be[i], 0, 0)),
                  pl.BlockSpec((None, 1, d2), lambda i, be, nu: (be[i], 0, 0)),
                  pl.BlockSpec((None, d, d), lambda i, be, nu: (be[i], 0, 0)),
                  pl.BlockSpec((None, 1, d), lambda i, be, nu: (be[i], 0, 0))],
        out_specs=pl.BlockSpec((MOE_ROWS, d), lambda i, be, nu: (i, 0)),
        scratch_shapes=[pltpu.VMEM((d, d2), BF16),
                        pltpu.VMEM((d, d), BF16)],
    )
    return pl.pallas_call(
        functools.partial(_moe_kernel, d_model=d),
        grid_spec=grid_spec,
        out_shape=jax.ShapeDtypeStruct((n_blocks * MOE_ROWS, d), F32),
        compiler_params=_params(("arbitrary",)),
        name="moe_experts",
    )(block_e, n_used, xs, w_gu, b_gu.reshape(e, 1, d2), w_dn, b_dn.reshape(e, 1, d))


def _combine_kernel(pos_ref, ys_ref, x1_ref, gate_ref, gm_ref, gfin_ref, y_ref, ybuf, idx_smem, isem, gsem, *,
                    tm, final_norm):
    i = pl.program_id(0)
    _gather_pipeline(i, pl.num_programs(0), pos_ref, idx_smem, isem, ys_ref, ybuf, gsem, TOP_K * tm)
    gate = gate_ref[...]
    f = jnp.zeros(x1_ref.shape, F32)
    groups = tm // ROW_GROUP
    for kk in range(TOP_K):
        rows = ybuf[i % 2, kk * groups:(kk + 1) * groups].reshape(x1_ref.shape)
        f = f + gate[:, kk:kk + 1] * rows
    x2 = x1_ref[...] + gm_ref[...] * f
    y_ref[...] = _rms(x2, gfin_ref[...]) if final_norm else x2


def _combine(pos_tiles, ys, x1, gate, g_m, final_g, *, tm, per_row, tiles_per_batch, final_norm):
    t, d = x1.shape
    row = lambda w: pl.BlockSpec((tm, w), lambda i: (i, 0))
    return pl.pallas_call(
        functools.partial(_combine_kernel, tm=tm, final_norm=final_norm),
        grid=(t // tm,),
        in_specs=[pl.BlockSpec(memory_space=pl.ANY), pl.BlockSpec(memory_space=pl.ANY), row(d), row(TOP_K),
                  _mod_spec(per_row, tm, d, tiles_per_batch), _const_spec((1, d))],
        out_specs=row(d),
        out_shape=jax.ShapeDtypeStruct((t, d), F32),
        scratch_shapes=[pltpu.VMEM((2, TOP_K * tm // ROW_GROUP, ROW_GROUP, d), F32),
                        pltpu.SMEM((2 * TOP_K * tm,), jnp.int32),
                        pltpu.SemaphoreType.DMA((2,)),
                        pltpu.SemaphoreType.DMA((2,))],
        compiler_params=_params(("arbitrary",)),
        name="combine",
    )(pos_tiles, ys, x1, gate, g_m, final_g)


def _rope_tables(pos):
    inv = ROPE_THETA ** (-jnp.arange(0, QK_ROPE, 2, dtype=F32) / QK_ROPE)
    ang = pos.astype(F32)[:, None] * inv[None, :]
    cos, sin = jnp.cos(ang), jnp.sin(ang)
    n = pos.shape[0]
    cc = jnp.concatenate([cos, cos, jnp.ones((n, LANES - QK_ROPE), F32)], axis=1)
    ss = jnp.concatenate([sin, sin, jnp.zeros((n, LANES - QK_ROPE), F32)], axis=1)
    return cc, ss


def _swap_halves(w):
    half = QK_ROPE // 2
    return jnp.concatenate([-w[..., half:], w[..., :half]], axis=-1)


def _layer_weights(l, w_in, norm_mix_g, q_norm_g, w_uq, kv_norm_g, w_uk, w_uv, w_pa, sg_norm_g, sg_norm_b,
                   w_pb, w_o, norm_ffn_g, router_w, router_b):
    d = w_in.shape[1]
    wi = w_in[l]
    o_kr = Q_RANK + KV_RANK
    o_u = o_kr + QK_ROPE
    o_v = o_u + SG_WIDTH
    o_ga = o_v + SG_WIDTH
    kr = wi[:, o_kr:o_u]
    zpad = jnp.zeros((d, LANES - QK_ROPE), F32)
    w_in_r = jnp.concatenate([wi[:, :o_kr], wi[:, o_u:o_ga], wi[:, o_ga:],
                              kr, zpad, _swap_halves(kr), zpad], axis=1).astype(BF16)
    uq = w_uq[l]
    nope, rope = uq[..., :QK_NOPE], uq[..., QK_NOPE:]
    z32 = jnp.zeros(rope.shape[:2] + (HEAD_SLOT - QK_NOPE - QK_ROPE,), F32)
    wq = jnp.concatenate([rope, nope, z32], axis=-1).reshape(Q_RANK, QK_WIDTH).astype(BF16)
    wqs = jnp.concatenate([_swap_halves(rope), jnp.zeros_like(nope), z32], axis=-1)
    wqs = wqs.reshape(Q_RANK, QK_WIDTH).astype(BF16)
    uk = w_uk[l]
    zk_lo = jnp.zeros(uk.shape[:2] + (QK_ROPE,), F32)
    zk_hi = jnp.zeros(uk.shape[:2] + (HEAD_SLOT - QK_NOPE - QK_ROPE,), F32)
    wk = jnp.concatenate([zk_lo, uk, zk_hi], axis=-1).reshape(KV_RANK, QK_WIDTH).astype(BF16)
    wv = w_uv[l].reshape(KV_RANK, V_WIDTH).astype(BF16)
    wv_slot = jnp.concatenate([w_uv[l], jnp.zeros_like(w_uv[l])], axis=-1).reshape(KV_RANK, QK_WIDTH).astype(BF16)
    vone = jnp.tile(jnp.concatenate([jnp.zeros((V_HEAD,), F32), jnp.ones((HEAD_SLOT - V_HEAD,), F32)]),
                    MLA_HEADS).reshape(1, QK_WIDTH)
    ukt = jnp.transpose(uk, (1, 2, 0))
    eye = jnp.broadcast_to(jnp.eye(QK_ROPE, LANES, dtype=F32), (MLA_HEADS, QK_ROPE, LANES))
    top = jnp.concatenate([jnp.zeros((MLA_HEADS, QK_ROPE, KV_RANK), F32), eye], axis=-1)
    mid = jnp.concatenate([ukt, jnp.zeros((MLA_HEADS, QK_NOPE, LANES), F32)], axis=-1)
    bot = jnp.zeros((MLA_HEADS, HEAD_SLOT - QK_NOPE - QK_ROPE, KV_RANK + LANES), F32)
    mabs = jnp.concatenate([top, mid, bot], axis=1).astype(BF16)
    rw = jnp.pad(router_w[l], ((0, 0), (0, ROUTER_PAD - N_EXPERTS)))
    rw_hi = rw.astype(BF16)
    rw_lo = (rw - rw_hi.astype(F32)).astype(BF16)
    rb = jnp.concatenate([router_b[l], jnp.full((ROUTER_PAD - N_EXPERTS,), NEG_BIG, F32)]).reshape(1, ROUTER_PAD)
    return dict(
        w_in_r=w_in_r, gmix=norm_mix_g[l].reshape(1, d),
        gq=(q_norm_g[l] * (ATTN_SCALE * LOG2_E)).reshape(1, Q_RANK),
        gkv=kv_norm_g[l].reshape(1, KV_RANK), wq=wq, wqs=wqs, wk=wk, wv=wv, wv_slot=wv_slot, vone=vone, mabs=mabs,
        sgg=sg_norm_g[l].reshape(1, SG_WIDTH), sgb=sg_norm_b[l].reshape(1, SG_WIDTH),
        w_pa=w_pa[l].astype(BF16), w_pb=w_pb[l].astype(BF16), w_o=w_o[l].astype(BF16),
        gffn=norm_ffn_g[l].reshape(1, d), rw_hi=rw_hi, rw_lo=rw_lo, rb=rb)


def _spatial_mix_weights(w_s, b_s, seq, n_batch):
    gw = SG_WIDTH // SG_GROUPS
    tril = jnp.tril(jnp.ones((SG_CHUNK, SG_CHUNK), dtype=bool))
    w = jnp.where(tril[None], w_s, 0.0)
    if seq % SG_CHUNK == 0:
        mixw = w
        bias_t = b_s
    else:
        assert seq < SG_CHUNK
        blk = w[:, :seq, :seq]
        eye = jnp.eye(n_batch, dtype=F32)
        mixw = jnp.einsum("ab,gts->gatbs", eye, blk).reshape(SG_GROUPS, n_batch * seq, n_batch * seq)
        bias_t = jnp.tile(b_s[:, :seq], (1, n_batch))
    bias = jnp.repeat(jnp.transpose(bias_t), gw, axis=1)
    return mixw.astype(BF16), bias


def _routing_tables(idx, rank, counts_f, n_blocks):
    counts = counts_f[0, :N_EXPERTS].astype(jnp.int32)
    padded = (counts + MOE_ROWS - 1) // MOE_ROWS * MOE_ROWS
    pend = jnp.cumsum(padded).astype(jnp.int32)
    pstart = pend - padded
    experts = jnp.arange(N_EXPERTS, dtype=jnp.int32)
    dest = rank + jnp.sum(jnp.where(idx[..., None] == experts, pstart, 0), axis=-1)
    block_start = jnp.arange(n_blocks, dtype=jnp.int32) * MOE_ROWS
    block_e = jnp.minimum(jnp.sum((pend[None, :] <= block_start[:, None]).astype(jnp.int32), axis=1), N_EXPERTS - 1)
    n_used = (pend[-1:] // MOE_ROWS).astype(jnp.int32)
    return dest.astype(jnp.int32), pend, block_e.astype(jnp.int32), n_used


def _pos_tiles(pos, tm):
    t = pos.shape[0]
    return jnp.transpose(pos.reshape(t // tm, tm, TOP_K), (0, 2, 1)).reshape(t // tm, TOP_K * tm)


def _pick_tile(n, pref):
    t = min(n, pref)
    assert n % t == 0 and t % 8 == 0
    return t


def kernel(x_prompt, x_sample, cache_ckv, cache_krope, c_prompt, c_sample, ada_w, ada_b, norm_mix_g, w_in, q_norm_g, w_uq, kv_norm_g, w_uk, w_uv, w_pa, sg_norm_g, sg_norm_b, w_spatial, b_spatial, w_pb, w_o, norm_ffn_g, router_w, router_b, w_gu, b_gu, w_dn, b_dn, final_g):
    bp, lp, d = x_prompt.shape
    bs, ls, _ = x_sample.shape
    depth = w_in.shape[0]
    past = cache_ckv.shape[2]
    tp, ts = bp * lp, bs * ls
    assert lp % SG_CHUNK == 0 and ls <= SG_CHUNK

    tm_p = _pick_tile(lp, ROW_TILE)
    tm_s = _pick_tile(ts, SAMPLE_TILE)
    t_attn = _pick_tile(lp, ATTN_TILE)
    tpb = lp // tm_p

    cc_p, ss_p = _rope_tables(jnp.arange(lp, dtype=jnp.int32))
    cc_s, ss_s = _rope_tables(past + jnp.arange(ls, dtype=jnp.int32))
    cc_s, ss_s = jnp.tile(cc_s, (bs, 1)), jnp.tile(ss_s, (bs, 1))

    b_all = bp + bs
    b_pad = -(-b_all // 8) * 8
    c_all = jnp.concatenate([c_prompt, c_sample, jnp.zeros((b_pad - b_all, d), F32)], axis=0)

    xp = x_prompt.reshape(tp, d)
    xs = x_sample.reshape(ts, d)
    outs = dict(ckv_p=[], kr_p=[], ckv_s=[], kr_s=[], v_s=[])
    final_g2 = final_g.reshape(1, d)
    for l in range(depth):
        wts = _layer_weights(l, w_in, norm_mix_g, q_norm_g, w_uq, kv_norm_g, w_uk, w_uv, w_pa, sg_norm_g,
                             sg_norm_b, w_pb, w_o, norm_ffn_g, router_w, router_b)
        mod = _adaln(c_all, ada_w[l], ada_b[l])
        mod_p = [mod[:bp, j * d:(j + 1) * d].reshape(bp, 1, d) for j in range(6)]
        mod_s = [jnp.repeat(mod[bp:b_all, j * d:(j + 1) * d], ls, axis=0) for j in range(6)]

        q, k, v, ckv, kr, u, vn, ga, gb = _inproj(xp, mod_p[0], mod_p[1], cc_p, ss_p, wts, tm=tm_p, per_row=False,
                                                  tiles_per_batch=tpb, vn_dtype=BF16)
        o = _attn_prompt(q.reshape(bp, lp, QK_WIDTH), k.reshape(bp, lp, QK_WIDTH), v.reshape(bp, lp, QK_WIDTH),
                         tq=t_attn, tk=t_attn).reshape(tp, V_WIDTH)
        mixw, bias = _spatial_mix_weights(w_spatial[l], b_spatial[l], lp, bp)
        x1p = _merge(o, u, vn, ga, gb, xp, mod_p[2], mixw, bias, wts, tm=tm_p, chunk=SG_CHUNK, per_row=False,
                     tiles_per_batch=tpb)
        outs["ckv_p"].append(ckv.reshape(bp, lp, KV_RANK))
        outs["kr_p"].append(kr.reshape(bp, lp, QK_ROPE))

        q, k, v, ckv, kr, u, vn, ga, gb = _inproj(xs, mod_s[0], mod_s[1], cc_s, ss_s, wts, tm=tm_s, per_row=True,
                                                  tiles_per_batch=1, vn_dtype=F32)
        ckv3, kr3 = ckv.reshape(bs, ls, KV_RANK), kr.reshape(bs, ls, QK_ROPE)
        o = _attn_sample(q.reshape(bs, ls, QK_WIDTH), cache_ckv[l], cache_krope[l], ckv3, kr3,
                         wts["mabs"], wts["wv"]).reshape(ts, V_WIDTH)
        mixw, bias = _spatial_mix_weights(w_spatial[l], b_spatial[l], ls, tm_s // ls)
        x1s = _merge(o, u, vn, ga, gb, xs, mod_s[2], mixw, bias, wts, tm=tm_s, chunk=tm_s, per_row=True,
                     tiles_per_batch=1)
        outs["ckv_s"].append(ckv3)
        outs["kr_s"].append(kr3)
        outs["v_s"].append(vn.reshape(bs, ls, SG_WIDTH))

        t_all = tp + ts
        tm_r = _pick_tile(math.gcd(tp, ts), ROW_TILE)
        h2_all, idx, gate, rank, counts = _router(x1p, mod_p[3], mod_p[4], x1s, mod_s[3], mod_s[4], wts, tm=tm_r,
                                                  tiles_per_batch=lp // tm_r)
        gate_p, gate_s = gate[:tp], gate[tp:]
        n_blocks = -(-(t_all * TOP_K) // MOE_ROWS) + N_EXPERTS
        pos, pend, block_e, n_used = _routing_tables(idx, rank, counts, n_blocks)
        x_sorted = _dispatch(pend, n_used, _pos_tiles(pos, tm_r), h2_all, tm=tm_r, n_blocks=n_blocks)
        ys = _moe_experts(block_e, n_used, x_sorted, w_gu[l], b_gu[l], w_dn[l], b_dn[l])
        last = l == depth - 1
        xp = _combine(_pos_tiles(pos[:tp], tm_r), ys, x1p, gate_p, mod_p[5], final_g2, tm=tm_r, per_row=False,
                      tiles_per_batch=lp // tm_r, final_norm=last)
        xs = _combine(_pos_tiles(pos[tp:], tm_r), ys, x1s, gate_s, mod_s[5], final_g2, tm=tm_r, per_row=True,
                      tiles_per_batch=1, final_norm=last)
    return (xp.reshape(bp, lp, d), xs.reshape(bs, ls, d),
            jnp.stack(outs["ckv_p"]), jnp.stack(outs["kr_p"]),
            jnp.stack(outs["ckv_s"]), jnp.stack(outs["kr_s"]), jnp.stack(outs["v_s"]))
```

```python
import functools
import math

import jax
import jax.numpy as jnp
from jax import lax
from jax.experimental import pallas as pl
from jax.experimental.pallas import tpu as pltpu

F32 = jnp.float32
BF16 = jnp.bfloat16

LANES = 128
VMEM_LIMIT_BYTES = 56 * 1024 * 1024

CHUNK = 64
CHUNK_SHIFT = 6
MLA_HEADS = 8
QK_NOPE = 64
QK_ROPE = 32
V_HEAD = 64
V_HEAD_SHIFT = 6
Q_RANK = 384
KV_RANK = 256
ROPE_THETA = 10000.0
ATTN_SCALE = 1.0 / math.sqrt(QK_NOPE + QK_ROPE)
LOG2_E = math.log2(math.e)
SG_CHUNK = 128
SG_GROUPS = 4
SG_WIDTH = 512
N_EXPERTS = 32
TOP_K = 4
SWIGLU_LIMIT = 7.0
SWIGLU_ALPHA = 1.702
EPS = 1e-6

HEAD_SLOT = LANES
QK_WIDTH = MLA_HEADS * HEAD_SLOT
V_WIDTH = MLA_HEADS * V_HEAD
MOE_ROWS = 512
ROW_TILE = 256
SAMPLE_TILE = 512
ATTN_TILE = 1024
ATTN_SUB_KEYS = 256
ROUTER_PAD = LANES
NEG_BIG = -1e30

_C_CQ = 0
_C_CKV = _C_CQ + Q_RANK
_C_U = _C_CKV + KV_RANK
_C_V = _C_U + SG_WIDTH
_C_GA = _C_V + SG_WIDTH


def _params(sem):
    return pltpu.CompilerParams(dimension_semantics=sem, vmem_limit_bytes=VMEM_LIMIT_BYTES)


def _dot(a, b):
    return jnp.dot(a, b, preferred_element_type=F32)


def _dot_nt(a, b):
    return lax.dot_general(a, b, (((1,), (1,)), ((), ())), preferred_element_type=F32)


def _rms(x, g):
    return x * lax.rsqrt(jnp.mean(x * x, axis=-1, keepdims=True) + EPS) * g


def _adaln_kernel(c_ref, w_ref, b_ref, o_ref):
    c = c_ref[...]
    s = (c * jax.nn.sigmoid(c)).astype(BF16)
    o_ref[...] = _dot(s, w_ref[...].astype(BF16)) + b_ref[...]


def _adaln(c_all, ada_w, ada_b):
    bp, d = c_all.shape
    n = ada_w.shape[1]
    return pl.pallas_call(
        _adaln_kernel,
        grid=(n // d,),
        in_specs=[pl.BlockSpec((bp, d), lambda j: (0, 0)),
                  pl.BlockSpec((d, d), lambda j: (0, j)),
                  pl.BlockSpec((1, d), lambda j: (0, j))],
        out_specs=pl.BlockSpec((bp, d), lambda j: (0, j)),
        out_shape=jax.ShapeDtypeStruct((bp, n), F32),
        compiler_params=_params(("arbitrary",)),
        name="adaln",
    )(c_all, ada_w, ada_b.reshape(1, n))


def _inproj_kernel(x_ref, sh_ref, sc_ref, gmix_ref, cc_ref, ss_ref, win_ref, gq_ref, gkv_ref,
                   wq_ref, wqs_ref, wk_ref, wv_ref, vone_ref, sgg_ref, sgb_ref,
                   q_ref, k_ref, v_ref, ckv_ref, kr_ref, u_ref, vn_ref, ga_ref, gb_ref, *, d_model):
    x = x_ref[...]
    h = (_rms(x, gmix_ref[...]) * (1.0 + sc_ref[...]) + sh_ref[...]).astype(BF16)

    def proj(lo, width):
        return _dot(h, win_ref[:, lo:lo + width])

    cc = cc_ref[...]
    ss = ss_ref[...]
    c_gb = _C_GA + d_model
    c_kra = c_gb + d_model
    c_krb = c_kra + LANES

    cqn = _rms(proj(_C_CQ, Q_RANK), gq_ref[...]).astype(BF16)
    qa = _dot(cqn, wq_ref[...])
    qb = _dot(cqn, wqs_ref[...])
    for hd in range(MLA_HEADS):
        sl = slice(hd * HEAD_SLOT, (hd + 1) * HEAD_SLOT)
        q_ref[:, sl] = (qa[:, sl] * cc + qb[:, sl] * ss).astype(BF16)

    ckvn = _rms(proj(_C_CKV, KV_RANK), gkv_ref[...])
    ckv_ref[...] = ckvn
    ckvb = ckvn.astype(BF16)
    krs = proj(c_kra, LANES) * cc + proj(c_krb, LANES) * ss
    kr_ref[...] = krs[:, :QK_ROPE]
    kn = _dot(ckvb, wk_ref[...])
    for hd in range(MLA_HEADS):
        sl = slice(hd * HEAD_SLOT, (hd + 1) * HEAD_SLOT)
        k_ref[:, sl] = (kn[:, sl] + krs).astype(BF16)
    v_ref[...] = (_dot(ckvb, wv_ref[...]) + vone_ref[...]).astype(BF16)

    u_ref[...] = proj(_C_U, SG_WIDTH).astype(u_ref.dtype)
    vv = proj(_C_V, SG_WIDTH)
    mu = jnp.mean(vv, axis=-1, keepdims=True)
    vc = vv - mu
    var = jnp.mean(vc * vc, axis=-1, keepdims=True)
    vn_ref[...] = (vc * lax.rsqrt(var + EPS) * sgg_ref[...] + sgb_ref[...]).astype(vn_ref.dtype)
    ga_ref[...] = proj(_C_GA, d_model).astype(BF16)
    gb_ref[...] = proj(c_gb, d_model).astype(BF16)


def _mod_spec(per_row, tm, d, tiles_per_batch):
    if per_row:
        return pl.BlockSpec((tm, d), lambda i: (i, 0))
    return pl.BlockSpec((None, 1, d), lambda i: (i // tiles_per_batch, 0, 0))


def _const_spec(shape):
    nd = len(shape)
    return pl.BlockSpec(shape, lambda i: (0,) * nd)


def _inproj(x2d, shift, scale, cc, ss, wts, *, tm, per_row, tiles_per_batch, vn_dtype):
    t, d = x2d.shape
    n_tab = cc.shape[0] // tm
    row = lambda w: pl.BlockSpec((tm, w), lambda i: (i, 0))
    tab = pl.BlockSpec((tm, LANES), lambda i: (i % n_tab, 0))
    mod = _mod_spec(per_row, tm, d, tiles_per_batch)
    consts = [wts["w_in_r"], wts["gq"], wts["gkv"], wts["wq"], wts["wqs"], wts["wk"], wts["wv_slot"],
              wts["vone"], wts["sgg"], wts["sgb"]]
    out_shapes = [jax.ShapeDtypeStruct((t, QK_WIDTH), BF16), jax.ShapeDtypeStruct((t, QK_WIDTH), BF16),
                  jax.ShapeDtypeStruct((t, QK_WIDTH), BF16), jax.ShapeDtypeStruct((t, KV_RANK), F32),
                  jax.ShapeDtypeStruct((t, QK_ROPE), F32), jax.ShapeDtypeStruct((t, SG_WIDTH), BF16),
                  jax.ShapeDtypeStruct((t, SG_WIDTH), vn_dtype), jax.ShapeDtypeStruct((t, d), BF16),
                  jax.ShapeDtypeStruct((t, d), BF16)]
    return pl.pallas_call(
        functools.partial(_inproj_kernel, d_model=d),
        grid=(t // tm,),
        in_specs=[row(d), mod, mod, _const_spec((1, d)), tab, tab] + [_const_spec(c.shape) for c in consts],
        out_specs=[row(s.shape[1]) for s in out_shapes],
        out_shape=out_shapes,
        compiler_params=_params(("arbitrary",)),
        name="inproj",
    )(x2d, shift, scale, wts["gmix"], cc, ss, *consts)


def _attn_kernel(qi_ref, kj_ref, flag_ref, q_ref, k_ref, v_ref, o_ref, m_sc, acc_sc, *, tq, tk, sub):
    s_id = pl.program_id(1)
    qi = qi_ref[s_id]
    kj = kj_ref[s_id]
    flags = flag_ref[s_id]

    @pl.when(kj == 0)
    def _():
        m_sc[...] = jnp.full(m_sc.shape, -jnp.inf, F32)
        acc_sc[...] = jnp.zeros(acc_sc.shape, F32)

    def sweep(bias):
        for kb in range(tk // sub):
            keys = slice(kb * sub, (kb + 1) * sub)
            rows = slice(kb * sub if bias is not None else 0, tq)
            for hd in range(MLA_HEADS):
                sl = slice(hd * HEAD_SLOT, (hd + 1) * HEAD_SLOT)
                s = _dot_nt(q_ref[rows, sl], k_ref[keys, sl])
                if bias is not None:
                    s = s + bias[rows, keys]
                tiles = [s[:, c * LANES:(c + 1) * LANES] for c in range(sub // LANES)]
                m_tile = tiles[0]
                for t in tiles[1:]:
                    m_tile = jnp.maximum(m_tile, t)
                m_old = m_sc[hd, rows]
                m_new = jnp.maximum(m_old, jnp.max(m_tile, axis=-1, keepdims=True))
                alpha = jnp.exp2(m_old - m_new)
                p = jnp.concatenate([jnp.exp2(t - m_new).astype(BF16) for t in tiles], axis=1)
                acc_sc[hd, rows] = alpha * acc_sc[hd, rows] + _dot(p, v_ref[keys, sl])
                m_sc[hd, rows] = m_new

    @pl.when((flags & 2) == 0)
    def _():
        sweep(None)

    @pl.when((flags & 2) != 0)
    def _():
        row = lax.broadcasted_iota(jnp.int32, (tq, tk), 0) + qi * tq
        col = lax.broadcasted_iota(jnp.int32, (tq, tk), 1) + kj * tk
        sweep(jnp.where((col >> CHUNK_SHIFT) <= (row >> CHUNK_SHIFT), 0.0, -jnp.inf))

    @pl.when((flags & 1) != 0)
    def _():
        lane = lax.broadcasted_iota(jnp.int32, (tq, LANES), 1)
        for pr in range(MLA_HEADS // 2):
            outs = []
            for hd in (2 * pr, 2 * pr + 1):
                acc = acc_sc[hd]
                outs.append(acc / pltpu.roll(acc, V_HEAD, axis=1))
            pair = jnp.where(lane < V_HEAD, outs[0], pltpu.roll(outs[1], V_HEAD, axis=1))
            o_ref[:, pr * LANES:(pr + 1) * LANES] = pair.astype(BF16)


def _attn_prompt(q, k, v, *, tq, tk):
    b, l, _ = q.shape
    assert tq == tk
    nq = l // tq
    qi_l, kj_l, flag_l = [], [], []
    for i in range(nq):
        n_kv = ((i + 1) * tq - 1) // tk + 1
        for j in range(n_kv):
            qi_l.append(i)
            kj_l.append(j)
            masked = ((j + 1) * tk - 1) // CHUNK > (i * tq) // CHUNK
            flag_l.append((1 if j == n_kv - 1 else 0) | (2 if masked else 0))
    steps = len(qi_l)
    grid_spec = pltpu.PrefetchScalarGridSpec(
        num_scalar_prefetch=3,
        grid=(b, steps),
        in_specs=[pl.BlockSpec((None, tq, QK_WIDTH), lambda bi, s, qi, kj, fl: (bi, qi[s], 0)),
                  pl.BlockSpec((None, tk, QK_WIDTH), lambda bi, s, qi, kj, fl: (bi, kj[s], 0)),
                  pl.BlockSpec((None, tk, QK_WIDTH), lambda bi, s, qi, kj, fl: (bi, kj[s], 0))],
        out_specs=pl.BlockSpec((None, tq, V_WIDTH), lambda bi, s, qi, kj, fl: (bi, qi[s], 0)),
        scratch_shapes=[pltpu.VMEM((MLA_HEADS, tq, LANES), F32), pltpu.VMEM((MLA_HEADS, tq, LANES), F32)],
    )
    return pl.pallas_call(
        functools.partial(_attn_kernel, tq=tq, tk=tk, sub=min(tk, ATTN_SUB_KEYS)),
        grid_spec=grid_spec,
        out_shape=jax.ShapeDtypeStruct((b, l, V_WIDTH), BF16),
        compiler_params=_params(("arbitrary", "arbitrary")),
        name="attn_prompt",
    )(jnp.asarray(qi_l, jnp.int32), jnp.asarray(kj_l, jnp.int32), jnp.asarray(flag_l, jnp.int32), q, k, v)


def _attn_sample_kernel(q_ref, pckv_ref, pkr_ref, nckv_ref, nkr_ref, mabs_ref, wv_ref, o_ref, *, ls, past):
    hl = MLA_HEADS * ls
    qcat = jnp.concatenate(
        [_dot(q_ref[:, hd * HEAD_SLOT:(hd + 1) * HEAD_SLOT], mabs_ref[hd]) for hd in range(MLA_HEADS)],
        axis=0).astype(BF16)
    q_abs = qcat[:, :KV_RANK]
    q_rope = qcat[:, KV_RANK:]
    pckv = pckv_ref[...].astype(BF16)
    nckv = nckv_ref[...].astype(BF16)

    def pad_lanes(kr):
        return jnp.concatenate([kr, jnp.zeros((kr.shape[0], LANES - QK_ROPE), kr.dtype)], axis=1).astype(BF16)

    s_past = _dot_nt(q_abs, pckv) + _dot_nt(q_rope, pad_lanes(pkr_ref[...]))
    s_new = _dot_nt(q_abs, nckv) + _dot_nt(q_rope, pad_lanes(nkr_ref[...]))

    qpos_1 = lax.broadcasted_iota(jnp.int32, (ls, 1), 0) + past
    qchunk = jnp.concatenate([qpos_1] * MLA_HEADS, axis=0) >> CHUNK_SHIFT
    kchunk_past = lax.broadcasted_iota(jnp.int32, (hl, past), 1) >> CHUNK_SHIFT
    kchunk_new = (lax.broadcasted_iota(jnp.int32, (hl, ls), 1) + past) >> CHUNK_SHIFT
    s_past = jnp.where(kchunk_past <= qchunk, s_past, -jnp.inf)
    s_new = jnp.where(kchunk_new <= qchunk, s_new, -jnp.inf)

    m = jnp.maximum(jnp.max(s_past, axis=-1, keepdims=True), jnp.max(s_new, axis=-1, keepdims=True))
    p_past = jnp.exp2(s_past - m)
    p_new = jnp.exp2(s_new - m)
    denom = jnp.sum(p_past, axis=-1, keepdims=True) + jnp.sum(p_new, axis=-1, keepdims=True)
    olat = (_dot(p_past.astype(BF16), pckv) + _dot(p_new.astype(BF16), nckv)) / denom
    ofull = _dot(olat.astype(BF16), wv_ref[...])
    col_head = lax.broadcasted_iota(jnp.int32, (ls, V_WIDTH), 1) >> V_HEAD_SHIFT
    out = jnp.zeros((ls, V_WIDTH), F32)
    for hd in range(MLA_HEADS):
        out = out + jnp.where(col_head == hd, ofull[hd * ls:(hd + 1) * ls], 0.0)
    o_ref[...] = out.astype(BF16)


def _attn_sample(q, past_ckv, past_kr, new_ckv, new_kr, mabs, wv):
    b, ls, _ = q.shape
    past = past_ckv.shape[1]
    blk = lambda n, w: pl.BlockSpec((None, n, w), lambda i: (i, 0, 0))
    return pl.pallas_call(
        functools.partial(_attn_sample_kernel, ls=ls, past=past),
        grid=(b,),
        in_specs=[blk(ls, QK_WIDTH), blk(past, KV_RANK), blk(past, QK_ROPE), blk(ls, KV_RANK), blk(ls, QK_ROPE),
                  _const_spec(mabs.shape), _const_spec(wv.shape)],
        out_specs=blk(ls, V_WIDTH),
        out_shape=jax.ShapeDtypeStruct((b, ls, V_WIDTH), BF16),
        compiler_params=_params(("arbitrary",)),
        name="attn_sample",
    )(q, past_ckv, past_kr, new_ckv, new_kr, mabs, wv)


def _merge_kernel(o_ref, u_ref, vn_ref, ga_ref, gb_ref, x_ref, gate_ref, mix_ref, bias_ref,
                  wpa_ref, wpb_ref, wo_ref, x1_ref, sg_sc, *, chunk):
    tm = x_ref.shape[0]
    gw = SG_WIDTH // SG_GROUPS
    for c in range(tm // chunk):
        rows = slice(c * chunk, (c + 1) * chunk)
        for g in range(SG_GROUPS):
            cols = slice(g * gw, (g + 1) * gw)
            mixed = _dot(mix_ref[g], vn_ref[rows, cols].astype(BF16)) + bias_ref[:, cols]
            sg_sc[rows, cols] = (u_ref[rows, cols].astype(F32) * mixed).astype(BF16)
    ya = _dot(o_ref[...], wpa_ref[...])
    yb = _dot(sg_sc[...], wpb_ref[...])
    m = jax.nn.sigmoid(ga_ref[...].astype(F32)) * ya + jax.nn.sigmoid(gb_ref[...].astype(F32)) * yb
    x1_ref[...] = x_ref[...] + gate_ref[...] * _dot(m.astype(BF16), wo_ref[...])


def _merge(o, u, vn, ga, gb, x2d, gate, mixw, bias, wts, *, tm, chunk, per_row, tiles_per_batch):
    t, d = x2d.shape
    row = lambda w: pl.BlockSpec((tm, w), lambda i: (i, 0))
    consts = [mixw, bias, wts["w_pa"], wts["w_pb"], wts["w_o"]]
    return pl.pallas_call(
        functools.partial(_merge_kernel, chunk=chunk),
        grid=(t // tm,),
        in_specs=[row(V_WIDTH), row(SG_WIDTH), row(SG_WIDTH), row(d), row(d), row(d),
                  _mod_spec(per_row, tm, d, tiles_per_batch)] + [_const_spec(c.shape) for c in consts],
        out_specs=row(d),
        out_shape=jax.ShapeDtypeStruct((t, d), F32),
        scratch_shapes=[pltpu.VMEM((tm, SG_WIDTH), BF16)],
        compiler_params=_params(("arbitrary",)),
        name="merge",
    )(o, u, vn, ga, gb, x2d, gate, *consts)


def _router_kernel(xp_ref, shp_ref, scp_ref, xs_ref, shs_ref, scs_ref, g_ref, whi_ref, wlo_ref, rb_ref,
                   h2_ref, idx_ref, gate_ref, rank_ref, cnt_ref, carry_sc, *, n_prompt_tiles):
    i = pl.program_id(0)
    out_refs = (g_ref, whi_ref, wlo_ref, rb_ref, h2_ref, idx_ref, gate_ref, rank_ref, cnt_ref, carry_sc)

    @pl.when(i == 0)
    def _():
        carry_sc[...] = jnp.zeros(carry_sc.shape, F32)

    @pl.when(i < n_prompt_tiles)
    def _():
        _route_rows(xp_ref, shp_ref, scp_ref, *out_refs)

    @pl.when(i >= n_prompt_tiles)
    def _():
        _route_rows(xs_ref, shs_ref, scs_ref, *out_refs)


def _route_rows(x1_ref, sh_ref, sc_ref, g_ref, whi_ref, wlo_ref, rb_ref, h2_ref, idx_ref, gate_ref, rank_ref,
                cnt_ref, carry_sc):
    h2 = _rms(x1_ref[...], g_ref[...]) * (1.0 + sc_ref[...]) + sh_ref[...]
    h2_ref[...] = h2
    hi = h2.astype(BF16)
    lo = (h2 - hi.astype(F32)).astype(BF16)
    logits = _dot(hi, whi_ref[...]) + _dot(lo, whi_ref[...]) + _dot(hi, wlo_ref[...]) + rb_ref[...]
    tm = logits.shape[0]
    work = jnp.transpose(logits)[:N_EXPERTS]
    expert = lax.broadcasted_iota(jnp.int32, work.shape, 0)
    vals, idxs = [], []
    for _ in range(TOP_K):
        mx = jnp.max(work, axis=0, keepdims=True)
        ix = jnp.min(jnp.where(work == mx, expert, N_EXPERTS), axis=0, keepdims=True)
        vals.append(mx)
        idxs.append(ix)
        work = jnp.where(expert == ix, -jnp.inf, work)
    es = [jnp.exp(v - vals[0]) for v in vals]
    tot = es[0]
    for e in es[1:]:
        tot = tot + e

    onehot = jnp.zeros(work.shape, F32)
    for j in range(TOP_K):
        onehot = jnp.where(expert == idxs[j], 1.0, onehot)
    earlier = (lax.broadcasted_iota(jnp.int32, (tm, tm), 0) < lax.broadcasted_iota(jnp.int32, (tm, tm), 1))
    within = _dot(onehot.astype(BF16), jnp.where(earlier, 1.0, 0.0).astype(BF16))
    carry = carry_sc[...]
    rank_full = within + (jnp.tile(carry, (1, tm // LANES)) if tm >= LANES else carry[:, :tm])
    ranks = [jnp.sum(jnp.where(expert == idxs[j], rank_full, 0.0), axis=0, keepdims=True) for j in range(TOP_K)]
    carry_sc[...] = carry_sc[...] + jnp.sum(onehot, axis=1, keepdims=True)
    cnt_ref[...] = carry_sc[...]

    row = lax.broadcasted_iota(jnp.int32, (2 * ROW_GROUP, tm), 0)
    packed = jnp.zeros((2 * ROW_GROUP, tm), F32)
    for j in range(TOP_K):
        packed = jnp.where(row == j, idxs[j].astype(F32), packed)
        packed = jnp.where(row == TOP_K + j, es[j] / tot, packed)
        packed = jnp.where(row == 2 * TOP_K + j, ranks[j], packed)
    packed = jnp.concatenate([packed, jnp.zeros((LANES - 2 * ROW_GROUP, tm), F32)], axis=0)
    by_token = jnp.transpose(packed)
    idx_ref[...] = by_token[:, :TOP_K].astype(jnp.int32)
    gate_ref[...] = by_token[:, TOP_K:2 * TOP_K]
    rank_ref[...] = by_token[:, 2 * TOP_K:3 * TOP_K].astype(jnp.int32)


def _router(x1p, shift_p, scale_p, x1s, shift_s, scale_s, wts, *, tm, tiles_per_batch):
    tp, d = x1p.shape
    ts = x1s.shape[0]
    n_p, n_s = tp // tm, ts // tm
    t_all = tp + ts
    row = lambda w: pl.BlockSpec((tm, w), lambda i: (i, 0))
    p_row = pl.BlockSpec((tm, d), lambda i: (jnp.minimum(i, n_p - 1), 0))
    p_mod = pl.BlockSpec((None, 1, d), lambda i: (jnp.minimum(i, n_p - 1) // tiles_per_batch, 0, 0))
    s_row = pl.BlockSpec((tm, d), lambda i: (jnp.maximum(i - n_p, 0), 0))
    consts = [wts["gffn"], wts["rw_hi"], wts["rw_lo"], wts["rb"]]
    return pl.pallas_call(
        functools.partial(_router_kernel, n_prompt_tiles=n_p),
        grid=(n_p + n_s,),
        in_specs=[p_row, p_mod, p_mod, s_row, s_row, s_row] + [_const_spec(c.shape) for c in consts],
        out_specs=[row(d), row(TOP_K), row(TOP_K), row(TOP_K), _const_spec((N_EXPERTS, LANES))],
        out_shape=[jax.ShapeDtypeStruct((t_all, d), F32), jax.ShapeDtypeStruct((t_all, TOP_K), jnp.int32),
                   jax.ShapeDtypeStruct((t_all, TOP_K), F32), jax.ShapeDtypeStruct((t_all, TOP_K), jnp.int32),
                   jax.ShapeDtypeStruct((N_EXPERTS, LANES), F32)],
        scratch_shapes=[pltpu.VMEM((N_EXPERTS, LANES), F32)],
        compiler_params=_params(("arbitrary",)),
        name="router",
    )(x1p, shift_p, scale_p, x1s, shift_s, scale_s, *consts)


ROW_GROUP = 8


def _slot_offset(slot, n_rows):
    return slot * n_rows if isinstance(slot, int) else pl.multiple_of(slot * n_rows, n_rows)


def _row_gather_start(idx_smem, slot, src_hbm, dst_vmem, sem, n_rows):
    base = _slot_offset(slot, n_rows)

    def group(g, carry):
        r0 = g * ROW_GROUP
        for j in range(ROW_GROUP):
            pltpu.make_async_copy(src_hbm.at[pl.ds(idx_smem[base + r0 + j], 1)], dst_vmem.at[g, pl.ds(j, 1)],
                                  sem).start()
        return carry
    lax.fori_loop(0, n_rows // ROW_GROUP, group, 0)


def _row_gather_wait(dst_vmem, sem):
    pltpu.make_async_copy(dst_vmem, dst_vmem, sem).wait()


def _gather_pipeline(i, n_steps, idx_hbm, idx_smem, isem, src_hbm, buf, gsem, n_rows, first_tile):
    def idx_copy(blk, slot):
        return pltpu.make_async_copy(idx_hbm.at[first_tile + blk],
                                     idx_smem.at[pl.ds(_slot_offset(slot, n_rows), n_rows)], isem.at[slot])

    @pl.when(i == 0)
    def _():
        idx_copy(0, 0).start()
        idx_copy(0, 0).wait()
        _row_gather_start(idx_smem, 0, src_hbm, buf.at[0], gsem.at[0], n_rows)

        @pl.when(n_steps > 1)
        def _():
            idx_copy(1, 1).start()

    nxt = (i + 1) % 2

    @pl.when(i + 1 < n_steps)
    def _():
        idx_copy(i + 1, nxt).wait()
        _row_gather_start(idx_smem, nxt, src_hbm, buf.at[nxt], gsem.at[nxt], n_rows)

    @pl.when(i + 2 < n_steps)
    def _():
        idx_copy(i + 2, i % 2).start()

    _row_gather_wait(buf.at[i % 2], gsem.at[i % 2])


def _dispatch_kernel(pend_ref, nused_ref, dest_ref, h2_ref, xs_ref, zbuf, idx_smem, isem, csem, zsem, *,
                     tm, n_blocks):
    i = pl.program_id(0)
    n = pl.num_programs(0)
    rows = TOP_K * tm

    def idx_copy(blk, slot):
        return pltpu.make_async_copy(dest_ref.at[blk], idx_smem.at[pl.ds(_slot_offset(slot, rows), rows)],
                                     isem.at[slot])

    def zero_copy(block_start):
        start = pl.multiple_of(block_start, MOE_ROWS)
        return pltpu.make_async_copy(zbuf, xs_ref.at[pl.ds(start, MOE_ROWS)], zsem)

    def rows_done():
        return pltpu.make_async_copy(xs_ref.at[pl.ds(0, rows)], xs_ref.at[pl.ds(0, rows)], csem)

    @pl.when(i == 0)
    def _():
        idx_copy(0, 0).start()
        zbuf[...] = jnp.zeros(zbuf.shape, F32)
        n_used = nused_ref[0]

        def last_block(e, carry):
            zero_copy(jnp.maximum(pend_ref[e] - MOE_ROWS, 0)).start()
            return carry
        lax.fori_loop(0, N_EXPERTS, last_block, 0)

        def tail_block(b, carry):
            zero_copy(b * MOE_ROWS).start()
            return carry
        lax.fori_loop(n_used, n_blocks, tail_block, 0)

        def drain(b, carry):
            zero_copy(0).wait()
            return carry
        lax.fori_loop(0, N_EXPERTS + n_blocks - n_used, drain, 0)

    slot = i % 2
    idx_copy(i, slot).wait()

    @pl.when(i + 1 < n)
    def _():
        idx_copy(i + 1, 1 - slot).start()

    base = _slot_offset(slot, rows)

    def group(g, carry):
        r0 = g * ROW_GROUP
        for j in range(ROW_GROUP):
            src = h2_ref.at[g, pl.ds(j, 1)]
            for kk in range(TOP_K):
                dst = xs_ref.at[pl.ds(idx_smem[base + kk * tm + r0 + j], 1)]
                pltpu.make_async_copy(src, dst, csem).start()
        return carry
    lax.fori_loop(0, tm // ROW_GROUP, group, 0)
    rows_done().wait()


def _dispatch(pend, n_used, dest_tiles, h2_all, *, tm, n_blocks):
    t_all, d = h2_all.shape
    grid_spec = pltpu.PrefetchScalarGridSpec(
        num_scalar_prefetch=2,
        grid=(t_all // tm,),
        in_specs=[pl.BlockSpec(memory_space=pl.ANY),
                  pl.BlockSpec((tm // ROW_GROUP, ROW_GROUP, d), lambda i, pe, nu: (i, 0, 0))],
        out_specs=pl.BlockSpec(memory_space=pl.ANY),
        scratch_shapes=[pltpu.VMEM((MOE_ROWS, d), F32),
                        pltpu.SMEM((2 * TOP_K * tm,), jnp.int32),
                        pltpu.SemaphoreType.DMA((2,)),
                        pltpu.SemaphoreType.DMA(()),
                        pltpu.SemaphoreType.DMA(())],
    )
    return pl.pallas_call(
        functools.partial(_dispatch_kernel, tm=tm, n_blocks=n_blocks),
        grid_spec=grid_spec,
        out_shape=jax.ShapeDtypeStruct((n_blocks * MOE_ROWS, d), F32),
        compiler_params=_params(("arbitrary",)),
        name="dispatch",
    )(pend, n_used, dest_tiles, h2_all.reshape(t_all // ROW_GROUP, ROW_GROUP, d))


def _moe_kernel(be_ref, nused_ref, xs_ref, wgu_ref, bgu_ref, wdn_ref, bdn_ref, y_ref, wgu_bf, wdn_bf, *, d_model):
    i = pl.program_id(0)
    n_used = nused_ref[0]

    @pl.when(i < n_used)
    def _():
        prev = be_ref[jnp.maximum(i - 1, 0)]

        @pl.when((i == 0) | (be_ref[i] != prev))
        def _():
            wgu_bf[...] = wgu_ref[...].astype(BF16)
            wdn_bf[...] = wdn_ref[...].astype(BF16)

        xb = xs_ref[...].astype(BF16)
        gu = _dot(xb, wgu_bf[...]) + bgu_ref[...]
        g = jnp.minimum(gu[:, :d_model], SWIGLU_LIMIT)
        lin = jnp.clip(gu[:, d_model:], -SWIGLU_LIMIT, SWIGLU_LIMIT)
        act = g * jax.nn.sigmoid(SWIGLU_ALPHA * g) * (lin + 1.0)
        y_ref[...] = _dot(act.astype(BF16), wdn_bf[...]) + bdn_ref[...]

    @pl.when(i >= n_used)
    def _():
        y_ref[...] = jnp.zeros(y_ref.shape, F32)


def _moe_experts(block_e, n_used, xs, w_gu, b_gu, w_dn, b_dn):
    n_blocks = xs.shape[0] // MOE_ROWS
    e, d, d2 = w_gu.shape
    grid_spec = pltpu.PrefetchScalarGridSpec(
        num_scalar_prefetch=2,
        grid=(n_blocks,),
        in_specs=[pl.BlockSpec((MOE_ROWS, d), lambda i, be, nu: (jnp.minimum(i, nu[0] - 1), 0)),
                  pl.BlockSpec((None, d, d2), lambda i, be, nu: (be[i], 0, 0)),
                  pl.BlockSpec((None, 1, d2), lambda i, be, nu: (be[i], 0, 0)),
                  pl.BlockSpec((None, d, d), lambda i, be, nu: (be[i], 0, 0)),
                  pl.BlockSpec((None, 1, d), lambda i, be, nu: (be[i], 0, 0))],
        out_specs=pl.BlockSpec((MOE_ROWS, d), lambda i, be, nu: (i, 0)),
        scratch_shapes=[pltpu.VMEM((d, d2), BF16),
                        pltpu.VMEM((d, d), BF16)],
    )
    return pl.pallas_call(
        functools.partial(_moe_kernel, d_model=d),
        grid_spec=grid_spec,
        out_shape=jax.ShapeDtypeStruct((n_blocks * MOE_ROWS, d), F32),
        compiler_params=_params(("arbitrary",)),
        name="moe_experts",
    )(block_e, n_used, xs, w_gu, b_gu.reshape(e, 1, d2), w_dn, b_dn.reshape(e, 1, d))


def _combine_kernel(pos_ref, ys_ref, x1_ref, gate_ref, gm_ref, gfin_ref, y_ref, ybuf, idx_smem, isem, gsem, *,
                    tm, first_tile, final_norm):
    i = pl.program_id(0)
    _gather_pipeline(i, pl.num_programs(0), pos_ref, idx_smem, isem, ys_ref, ybuf, gsem, TOP_K * tm, first_tile)
    gate = gate_ref[...]
    f = jnp.zeros(x1_ref.shape, F32)
    groups = tm // ROW_GROUP
    for kk in range(TOP_K):
        rows = ybuf[i % 2, kk * groups:(kk + 1) * groups].reshape(x1_ref.shape)
        f = f + gate[:, kk:kk + 1] * rows
    x2 = x1_ref[...] + gm_ref[...] * f
    y_ref[...] = _rms(x2, gfin_ref[...]) if final_norm else x2


def _combine(pos_tiles, ys, x1, gate, g_m, final_g, *, tm, first_tile, per_row, tiles_per_batch, final_norm):
    t, d = x1.shape
    row = lambda w: pl.BlockSpec((tm, w), lambda i: (i, 0))
    return pl.pallas_call(
        functools.partial(_combine_kernel, tm=tm, first_tile=first_tile, final_norm=final_norm),
        grid=(t // tm,),
        in_specs=[pl.BlockSpec(memory_space=pl.ANY), pl.BlockSpec(memory_space=pl.ANY), row(d),
                  pl.BlockSpec((tm, TOP_K), lambda i: (first_tile + i, 0)),
                  _mod_spec(per_row, tm, d, tiles_per_batch), _const_spec((1, d))],
        out_specs=row(d),
        out_shape=jax.ShapeDtypeStruct((t, d), F32),
        scratch_shapes=[pltpu.VMEM((2, TOP_K * tm // ROW_GROUP, ROW_GROUP, d), F32),
                        pltpu.SMEM((2 * TOP_K * tm,), jnp.int32),
                        pltpu.SemaphoreType.DMA((2,)),
                        pltpu.SemaphoreType.DMA((2,))],
        compiler_params=_params(("arbitrary",)),
        name="combine",
    )(pos_tiles, ys, x1, gate, g_m, final_g)


def _rope_tables(pos):
    inv = ROPE_THETA ** (-jnp.arange(0, QK_ROPE, 2, dtype=F32) / QK_ROPE)
    ang = pos.astype(F32)[:, None] * inv[None, :]
    cos, sin = jnp.cos(ang), jnp.sin(ang)
    n = pos.shape[0]
    cc = jnp.concatenate([cos, cos, jnp.ones((n, LANES - QK_ROPE), F32)], axis=1)
    ss = jnp.concatenate([sin, sin, jnp.zeros((n, LANES - QK_ROPE), F32)], axis=1)
    return cc, ss


def _swap_halves(w):
    half = QK_ROPE // 2
    return jnp.concatenate([-w[..., half:], w[..., :half]], axis=-1)


def _layer_weights(l, w_in, norm_mix_g, q_norm_g, w_uq, kv_norm_g, w_uk, w_uv, w_pa, sg_norm_g, sg_norm_b,
                   w_pb, w_o, norm_ffn_g, router_w, router_b):
    d = w_in.shape[1]
    wi = w_in[l]
    o_kr = Q_RANK + KV_RANK
    o_u = o_kr + QK_ROPE
    o_v = o_u + SG_WIDTH
    o_ga = o_v + SG_WIDTH
    kr = wi[:, o_kr:o_u]
    zpad = jnp.zeros((d, LANES - QK_ROPE), F32)
    w_in_r = jnp.concatenate([wi[:, :o_kr], wi[:, o_u:o_ga], wi[:, o_ga:],
                              kr, zpad, _swap_halves(kr), zpad], axis=1).astype(BF16)
    uq = w_uq[l]
    nope, rope = uq[..., :QK_NOPE], uq[..., QK_NOPE:]
    z32 = jnp.zeros(rope.shape[:2] + (HEAD_SLOT - QK_NOPE - QK_ROPE,), F32)
    wq = jnp.concatenate([rope, nope, z32], axis=-1).reshape(Q_RANK, QK_WIDTH).astype(BF16)
    wqs = jnp.concatenate([_swap_halves(rope), jnp.zeros_like(nope), z32], axis=-1)
    wqs = wqs.reshape(Q_RANK, QK_WIDTH).astype(BF16)
    uk = w_uk[l]
    zk_lo = jnp.zeros(uk.shape[:2] + (QK_ROPE,), F32)
    zk_hi = jnp.zeros(uk.shape[:2] + (HEAD_SLOT - QK_NOPE - QK_ROPE,), F32)
    wk = jnp.concatenate([zk_lo, uk, zk_hi], axis=-1).reshape(KV_RANK, QK_WIDTH).astype(BF16)
    wv = w_uv[l].reshape(KV_RANK, V_WIDTH).astype(BF16)
    wv_slot = jnp.concatenate([w_uv[l], jnp.zeros_like(w_uv[l])], axis=-1).reshape(KV_RANK, QK_WIDTH).astype(BF16)
    vone = jnp.tile(jnp.concatenate([jnp.zeros((V_HEAD,), F32), jnp.ones((HEAD_SLOT - V_HEAD,), F32)]),
                    MLA_HEADS).reshape(1, QK_WIDTH)
    ukt = jnp.transpose(uk, (1, 2, 0))
    eye = jnp.broadcast_to(jnp.eye(QK_ROPE, LANES, dtype=F32), (MLA_HEADS, QK_ROPE, LANES))
    top = jnp.concatenate([jnp.zeros((MLA_HEADS, QK_ROPE, KV_RANK), F32), eye], axis=-1)
    mid = jnp.concatenate([ukt, jnp.zeros((MLA_HEADS, QK_NOPE, LANES), F32)], axis=-1)
    bot = jnp.zeros((MLA_HEADS, HEAD_SLOT - QK_NOPE - QK_ROPE, KV_RANK + LANES), F32)
    mabs = jnp.concatenate([top, mid, bot], axis=1).astype(BF16)
    rw = jnp.pad(router_w[l], ((0, 0), (0, ROUTER_PAD - N_EXPERTS)))
    rw_hi = rw.astype(BF16)
    rw_lo = (rw - rw_hi.astype(F32)).astype(BF16)
    rb = jnp.concatenate([router_b[l], jnp.full((ROUTER_PAD - N_EXPERTS,), NEG_BIG, F32)]).reshape(1, ROUTER_PAD)
    return dict(
        w_in_r=w_in_r, gmix=norm_mix_g[l].reshape(1, d),
        gq=(q_norm_g[l] * (ATTN_SCALE * LOG2_E)).reshape(1, Q_RANK),
        gkv=kv_norm_g[l].reshape(1, KV_RANK), wq=wq, wqs=wqs, wk=wk, wv=wv, wv_slot=wv_slot, vone=vone, mabs=mabs,
        sgg=sg_norm_g[l].reshape(1, SG_WIDTH), sgb=sg_norm_b[l].reshape(1, SG_WIDTH),
        w_pa=w_pa[l].astype(BF16), w_pb=w_pb[l].astype(BF16), w_o=w_o[l].astype(BF16),
        gffn=norm_ffn_g[l].reshape(1, d), rw_hi=rw_hi, rw_lo=rw_lo, rb=rb)


def _spatial_mix_weights(w_s, b_s, seq, n_batch):
    gw = SG_WIDTH // SG_GROUPS
    tril = jnp.tril(jnp.ones((SG_CHUNK, SG_CHUNK), dtype=bool))
    w = jnp.where(tril[None], w_s, 0.0)
    if seq % SG_CHUNK == 0:
        mixw = w
        bias_t = b_s
    else:
        assert seq < SG_CHUNK
        blk = w[:, :seq, :seq]
        eye = jnp.eye(n_batch, dtype=F32)
        mixw = jnp.einsum("ab,gts->gatbs", eye, blk).reshape(SG_GROUPS, n_batch * seq, n_batch * seq)
        bias_t = jnp.tile(b_s[:, :seq], (1, n_batch))
    bias = jnp.repeat(jnp.transpose(bias_t), gw, axis=1)
    return mixw.astype(BF16), bias


def _routing_tables(idx, rank, counts_f, n_blocks):
    counts = counts_f[:, 0].astype(jnp.int32)
    padded = (counts + MOE_ROWS - 1) // MOE_ROWS * MOE_ROWS
    pend = jnp.cumsum(padded).astype(jnp.int32)
    pstart = pend - padded
    experts = jnp.arange(N_EXPERTS, dtype=jnp.int32)
    dest = rank + jnp.sum(jnp.where(idx[..., None] == experts, pstart, 0), axis=-1)
    block_start = jnp.arange(n_blocks, dtype=jnp.int32) * MOE_ROWS
    block_e = jnp.minimum(jnp.sum((pend[None, :] <= block_start[:, None]).astype(jnp.int32), axis=1), N_EXPERTS - 1)
    n_used = (pend[-1:] // MOE_ROWS).astype(jnp.int32)
    return dest.astype(jnp.int32), pend, block_e.astype(jnp.int32), n_used


def _pos_tiles(pos, tm):
    t = pos.shape[0]
    return jnp.transpose(pos.reshape(t // tm, tm, TOP_K), (0, 2, 1)).reshape(t // tm, TOP_K * tm)


def _pick_tile(n, pref):
    t = min(n, pref)
    assert n % t == 0 and t % 8 == 0
    return t


def kernel(x_prompt, x_sample, cache_ckv, cache_krope, c_prompt, c_sample, ada_w, ada_b, norm_mix_g, w_in, q_norm_g, w_uq, kv_norm_g, w_uk, w_uv, w_pa, sg_norm_g, sg_norm_b, w_spatial, b_spatial, w_pb, w_o, norm_ffn_g, router_w, router_b, w_gu, b_gu, w_dn, b_dn, final_g):
    bp, lp, d = x_prompt.shape
    bs, ls, _ = x_sample.shape
    depth = w_in.shape[0]
    past = cache_ckv.shape[2]
    tp, ts = bp * lp, bs * ls
    assert lp % SG_CHUNK == 0 and ls <= SG_CHUNK

    tm_p = _pick_tile(lp, ROW_TILE)
    tm_s = _pick_tile(ts, SAMPLE_TILE)
    t_attn = _pick_tile(lp, ATTN_TILE)
    tpb = lp // tm_p

    cc_p, ss_p = _rope_tables(jnp.arange(lp, dtype=jnp.int32))
    cc_s, ss_s = _rope_tables(past + jnp.arange(ls, dtype=jnp.int32))
    cc_s, ss_s = jnp.tile(cc_s, (bs, 1)), jnp.tile(ss_s, (bs, 1))

    b_all = bp + bs
    b_pad = -(-b_all // 8) * 8
    c_all = jnp.concatenate([c_prompt, c_sample, jnp.zeros((b_pad - b_all, d), F32)], axis=0)

    xp = x_prompt.reshape(tp, d)
    xs = x_sample.reshape(ts, d)
    outs = dict(ckv_p=[], kr_p=[], ckv_s=[], kr_s=[], v_s=[])
    final_g2 = final_g.reshape(1, d)
    for l in range(depth):
        wts = _layer_weights(l, w_in, norm_mix_g, q_norm_g, w_uq, kv_norm_g, w_uk, w_uv, w_pa, sg_norm_g,
                             sg_norm_b, w_pb, w_o, norm_ffn_g, router_w, router_b)
        mod = _adaln(c_all, ada_w[l], ada_b[l])
        mod_p = [mod[:bp, j * d:(j + 1) * d].reshape(bp, 1, d) for j in range(6)]
        mod_s = [jnp.repeat(mod[bp:b_all, j * d:(j + 1) * d], ls, axis=0) for j in range(6)]

        q, k, v, ckv, kr, u, vn, ga, gb = _inproj(xp, mod_p[0], mod_p[1], cc_p, ss_p, wts, tm=tm_p, per_row=False,
                                                  tiles_per_batch=tpb, vn_dtype=BF16)
        o = _attn_prompt(q.reshape(bp, lp, QK_WIDTH), k.reshape(bp, lp, QK_WIDTH), v.reshape(bp, lp, QK_WIDTH),
                         tq=t_attn, tk=t_attn).reshape(tp, V_WIDTH)
        mixw, bias = _spatial_mix_weights(w_spatial[l], b_spatial[l], lp, bp)
        x1p = _merge(o, u, vn, ga, gb, xp, mod_p[2], mixw, bias, wts, tm=tm_p, chunk=SG_CHUNK, per_row=False,
                     tiles_per_batch=tpb)
        outs["ckv_p"].append(ckv.reshape(bp, lp, KV_RANK))
        outs["kr_p"].append(kr.reshape(bp, lp, QK_ROPE))

        q, k, v, ckv, kr, u, vn, ga, gb = _inproj(xs, mod_s[0], mod_s[1], cc_s, ss_s, wts, tm=tm_s, per_row=True,
                                                  tiles_per_batch=1, vn_dtype=F32)
        ckv3, kr3 = ckv.reshape(bs, ls, KV_RANK), kr.reshape(bs, ls, QK_ROPE)
        o = _attn_sample(q.reshape(bs, ls, QK_WIDTH), cache_ckv[l], cache_krope[l], ckv3, kr3,
                         wts["mabs"], wts["wv"]).reshape(ts, V_WIDTH)
        mixw, bias = _spatial_mix_weights(w_spatial[l], b_spatial[l], ls, tm_s // ls)
        x1s = _merge(o, u, vn, ga, gb, xs, mod_s[2], mixw, bias, wts, tm=tm_s, chunk=tm_s, per_row=True,
                     tiles_per_batch=1)
        outs["ckv_s"].append(ckv3)
        outs["kr_s"].append(kr3)
        outs["v_s"].append(vn.reshape(bs, ls, SG_WIDTH))

        t_all = tp + ts
        tm_r = _pick_tile(math.gcd(tp, ts), ROW_TILE)
        h2_all, idx, gate, rank, counts = _router(x1p, mod_p[3], mod_p[4], x1s, mod_s[3], mod_s[4], wts, tm=tm_r,
                                                  tiles_per_batch=lp // tm_r)
        n_blocks = -(-(t_all * TOP_K) // MOE_ROWS) + N_EXPERTS
        pos, pend, block_e, n_used = _routing_tables(idx, rank, counts, n_blocks)
        pos_tiles = _pos_tiles(pos, tm_r)
        x_sorted = _dispatch(pend, n_used, pos_tiles, h2_all, tm=tm_r, n_blocks=n_blocks)
        ys = _moe_experts(block_e, n_used, x_sorted, w_gu[l], b_gu[l], w_dn[l], b_dn[l])
        last = l == depth - 1
        xp = _combine(pos_tiles, ys, x1p, gate, mod_p[5], final_g2, tm=tm_r, first_tile=0, per_row=False,
                      tiles_per_batch=lp // tm_r, final_norm=last)
        xs = _combine(pos_tiles, ys, x1s, gate, mod_s[5], final_g2, tm=tm_r, first_tile=tp // tm_r, per_row=True,
                      tiles_per_batch=1, final_norm=last)
    return (xp.reshape(bp, lp, d), xs.reshape(bs, ls, d),
            jnp.stack(outs["ckv_p"]), jnp.stack(outs["kr_p"]),
            jnp.stack(outs["ckv_s"]), jnp.stack(outs["kr_s"]), jnp.stack(outs["v_s"]))
```

```python
import functools
import math

import jax
import jax.numpy as jnp
from jax import lax
from jax.experimental import pallas as pl
from jax.experimental.pallas import tpu as pltpu

F32 = jnp.float32
BF16 = jnp.bfloat16

LANES = 128
VMEM_LIMIT_BYTES = 56 * 1024 * 1024

CHUNK = 64
CHUNK_SHIFT = 6
MLA_HEADS = 8
QK_NOPE = 64
QK_ROPE = 32
V_HEAD = 64
V_HEAD_SHIFT = 6
Q_RANK = 384
KV_RANK = 256
ROPE_THETA = 10000.0
ATTN_SCALE = 1.0 / math.sqrt(QK_NOPE + QK_ROPE)
LOG2_E = math.log2(math.e)
SG_CHUNK = 128
SG_GROUPS = 4
SG_WIDTH = 512
N_EXPERTS = 32
TOP_K = 4
SWIGLU_LIMIT = 7.0
SWIGLU_ALPHA = 1.702
EPS = 1e-6

HEAD_SLOT = LANES
QK_WIDTH = MLA_HEADS * HEAD_SLOT
V_WIDTH = MLA_HEADS * V_HEAD
MOE_ROWS = 512
ROW_TILE = 256
INPROJ_TILE = 512
MERGE_TILE = 512
SAMPLE_TILE = 512
ATTN_TILE = 1024
ATTN_SUB_KEYS = 256
ROUTER_PAD = LANES
NEG_BIG = -1e30

_C_CQ = 0
_C_CKV = _C_CQ + Q_RANK
_C_U = _C_CKV + KV_RANK
_C_V = _C_U + SG_WIDTH
_C_GA = _C_V + SG_WIDTH


def _params(sem):
    return pltpu.CompilerParams(dimension_semantics=sem, vmem_limit_bytes=VMEM_LIMIT_BYTES)


def _dot(a, b):
    return jnp.dot(a, b, preferred_element_type=F32)


def _dot_nt(a, b):
    return lax.dot_general(a, b, (((1,), (1,)), ((), ())), preferred_element_type=F32)


def _rms(x, g):
    return x * lax.rsqrt(jnp.mean(x * x, axis=-1, keepdims=True) + EPS) * g


def _adaln_kernel(c_ref, w_ref, b_ref, o_ref):
    c = c_ref[...]
    s = (c * jax.nn.sigmoid(c)).astype(BF16)
    o_ref[...] = _dot(s, w_ref[...].astype(BF16)) + b_ref[...]


def _adaln(c_all, ada_w, ada_b):
    bp, d = c_all.shape
    n = ada_w.shape[1]
    return pl.pallas_call(
        _adaln_kernel,
        grid=(n // d,),
        in_specs=[pl.BlockSpec((bp, d), lambda j: (0, 0)),
                  pl.BlockSpec((d, d), lambda j: (0, j)),
                  pl.BlockSpec((1, d), lambda j: (0, j))],
        out_specs=pl.BlockSpec((bp, d), lambda j: (0, j)),
        out_shape=jax.ShapeDtypeStruct((bp, n), F32),
        compiler_params=_params(("arbitrary",)),
        name="adaln",
    )(c_all, ada_w, ada_b.reshape(1, n))


def _inproj_kernel(x_ref, sh_ref, sc_ref, gmix_ref, cc_ref, ss_ref, win_ref, gq_ref, gkv_ref,
                   wq_ref, wqs_ref, wk_ref, wv_ref, vone_ref, sgg_ref, sgb_ref,
                   q_ref, k_ref, v_ref, ckv_ref, kr_ref, u_ref, vn_ref, ga_ref, gb_ref, *, d_model):
    x = x_ref[...]
    h = (_rms(x, gmix_ref[...]) * (1.0 + sc_ref[...]) + sh_ref[...]).astype(BF16)

    def proj(lo, width):
        return _dot(h, win_ref[:, lo:lo + width])

    cc = cc_ref[...]
    ss = ss_ref[...]
    c_gb = _C_GA + d_model
    c_kra = c_gb + d_model
    c_krb = c_kra + LANES

    cqn = _rms(proj(_C_CQ, Q_RANK), gq_ref[...]).astype(BF16)
    qa = _dot(cqn, wq_ref[...])
    qb = _dot(cqn, wqs_ref[...])
    for hd in range(MLA_HEADS):
        sl = slice(hd * HEAD_SLOT, (hd + 1) * HEAD_SLOT)
        q_ref[:, sl] = (qa[:, sl] * cc + qb[:, sl] * ss).astype(BF16)

    ckvn = _rms(proj(_C_CKV, KV_RANK), gkv_ref[...])
    ckv_ref[...] = ckvn
    ckvb = ckvn.astype(BF16)
    krs = proj(c_kra, LANES) * cc + proj(c_krb, LANES) * ss
    kr_ref[...] = krs[:, :QK_ROPE]
    kn = _dot(ckvb, wk_ref[...])
    for hd in range(MLA_HEADS):
        sl = slice(hd * HEAD_SLOT, (hd + 1) * HEAD_SLOT)
        k_ref[:, sl] = (kn[:, sl] + krs).astype(BF16)
    v_ref[...] = (_dot(ckvb, wv_ref[...]) + vone_ref[...]).astype(BF16)

    u_ref[...] = proj(_C_U, SG_WIDTH).astype(u_ref.dtype)
    vv = proj(_C_V, SG_WIDTH)
    mu = jnp.mean(vv, axis=-1, keepdims=True)
    vc = vv - mu
    var = jnp.mean(vc * vc, axis=-1, keepdims=True)
    vn_ref[...] = (vc * lax.rsqrt(var + EPS) * sgg_ref[...] + sgb_ref[...]).astype(vn_ref.dtype)
    ga_ref[...] = proj(_C_GA, d_model).astype(BF16)
    gb_ref[...] = proj(c_gb, d_model).astype(BF16)


def _mod_spec(per_row, tm, d, tiles_per_batch):
    if per_row:
        return pl.BlockSpec((tm, d), lambda i: (i, 0))
    return pl.BlockSpec((None, 1, d), lambda i: (i // tiles_per_batch, 0, 0))


def _const_spec(shape):
    nd = len(shape)
    return pl.BlockSpec(shape, lambda i: (0,) * nd)


def _inproj(x2d, shift, scale, cc, ss, wts, *, tm, per_row, tiles_per_batch, vn_dtype):
    t, d = x2d.shape
    n_tab = cc.shape[0] // tm
    row = lambda w: pl.BlockSpec((tm, w), lambda i: (i, 0))
    tab = pl.BlockSpec((tm, LANES), lambda i: (i % n_tab, 0))
    mod = _mod_spec(per_row, tm, d, tiles_per_batch)
    consts = [wts["w_in_r"], wts["gq"], wts["gkv"], wts["wq"], wts["wqs"], wts["wk"], wts["wv_slot"],
              wts["vone"], wts["sgg"], wts["sgb"]]
    out_shapes = [jax.ShapeDtypeStruct((t, QK_WIDTH), BF16), jax.ShapeDtypeStruct((t, QK_WIDTH), BF16),
                  jax.ShapeDtypeStruct((t, QK_WIDTH), BF16), jax.ShapeDtypeStruct((t, KV_RANK), F32),
                  jax.ShapeDtypeStruct((t, QK_ROPE), F32), jax.ShapeDtypeStruct((t, SG_WIDTH), BF16),
                  jax.ShapeDtypeStruct((t, SG_WIDTH), vn_dtype), jax.ShapeDtypeStruct((t, d), BF16),
                  jax.ShapeDtypeStruct((t, d), BF16)]
    return pl.pallas_call(
        functools.partial(_inproj_kernel, d_model=d),
        grid=(t // tm,),
        in_specs=[row(d), mod, mod, _const_spec((1, d)), tab, tab] + [_const_spec(c.shape) for c in consts],
        out_specs=[row(s.shape[1]) for s in out_shapes],
        out_shape=out_shapes,
        compiler_params=_params(("arbitrary",)),
        name="inproj",
    )(x2d, shift, scale, wts["gmix"], cc, ss, *consts)


def _attn_kernel(qi_ref, kj_ref, flag_ref, q_ref, k_ref, v_ref, o_ref, m_sc, acc_sc, *, tq, tk, sub):
    s_id = pl.program_id(1)
    qi = qi_ref[s_id]
    kj = kj_ref[s_id]
    flags = flag_ref[s_id]

    @pl.when(kj == 0)
    def _():
        m_sc[...] = jnp.full(m_sc.shape, -jnp.inf, F32)
        acc_sc[...] = jnp.zeros(acc_sc.shape, F32)

    def sweep(bias):
        for kb in range(tk // sub):
            keys = slice(kb * sub, (kb + 1) * sub)
            rows = slice(kb * sub if bias is not None else 0, tq)
            for hd in range(MLA_HEADS):
                sl = slice(hd * HEAD_SLOT, (hd + 1) * HEAD_SLOT)
                s = _dot_nt(q_ref[rows, sl], k_ref[keys, sl])
                if bias is not None:
                    s = s + bias[rows, keys]
                tiles = [s[:, c * LANES:(c + 1) * LANES] for c in range(sub // LANES)]
                m_tile = tiles[0]
                for t in tiles[1:]:
                    m_tile = jnp.maximum(m_tile, t)
                m_old = m_sc[hd, rows]
                m_new = jnp.maximum(m_old, jnp.max(m_tile, axis=-1, keepdims=True))
                alpha = jnp.exp2(m_old - m_new)
                p = jnp.concatenate([jnp.exp2((t - m_new).astype(BF16)) for t in tiles], axis=1)
                acc_sc[hd, rows] = alpha * acc_sc[hd, rows] + _dot(p, v_ref[keys, sl])
                m_sc[hd, rows] = m_new

    @pl.when((flags & 2) == 0)
    def _():
        sweep(None)

    @pl.when((flags & 2) != 0)
    def _():
        row = lax.broadcasted_iota(jnp.int32, (tq, tk), 0) + qi * tq
        col = lax.broadcasted_iota(jnp.int32, (tq, tk), 1) + kj * tk
        sweep(jnp.where((col >> CHUNK_SHIFT) <= (row >> CHUNK_SHIFT), 0.0, -jnp.inf))

    @pl.when((flags & 1) != 0)
    def _():
        lane = lax.broadcasted_iota(jnp.int32, (tq, LANES), 1)
        for pr in range(MLA_HEADS // 2):
            outs = []
            for hd in (2 * pr, 2 * pr + 1):
                acc = acc_sc[hd]
                outs.append(acc / pltpu.roll(acc, V_HEAD, axis=1))
            pair = jnp.where(lane < V_HEAD, outs[0], pltpu.roll(outs[1], V_HEAD, axis=1))
            o_ref[:, pr * LANES:(pr + 1) * LANES] = pair.astype(BF16)


def _attn_prompt(q, k, v, *, tq, tk):
    b, l, _ = q.shape
    assert tq == tk
    nq = l // tq
    qi_l, kj_l, flag_l = [], [], []
    for i in range(nq):
        n_kv = ((i + 1) * tq - 1) // tk + 1
        for j in range(n_kv):
            qi_l.append(i)
            kj_l.append(j)
            masked = ((j + 1) * tk - 1) // CHUNK > (i * tq) // CHUNK
            flag_l.append((1 if j == n_kv - 1 else 0) | (2 if masked else 0))
    steps = len(qi_l)
    grid_spec = pltpu.PrefetchScalarGridSpec(
        num_scalar_prefetch=3,
        grid=(b, steps),
        in_specs=[pl.BlockSpec((None, tq, QK_WIDTH), lambda bi, s, qi, kj, fl: (bi, qi[s], 0)),
                  pl.BlockSpec((None, tk, QK_WIDTH), lambda bi, s, qi, kj, fl: (bi, kj[s], 0)),
                  pl.BlockSpec((None, tk, QK_WIDTH), lambda bi, s, qi, kj, fl: (bi, kj[s], 0))],
        out_specs=pl.BlockSpec((None, tq, V_WIDTH), lambda bi, s, qi, kj, fl: (bi, qi[s], 0)),
        scratch_shapes=[pltpu.VMEM((MLA_HEADS, tq, LANES), F32), pltpu.VMEM((MLA_HEADS, tq, LANES), F32)],
    )
    return pl.pallas_call(
        functools.partial(_attn_kernel, tq=tq, tk=tk, sub=min(tk, ATTN_SUB_KEYS)),
        grid_spec=grid_spec,
        out_shape=jax.ShapeDtypeStruct((b, l, V_WIDTH), BF16),
        compiler_params=_params(("arbitrary", "arbitrary")),
        name="attn_prompt",
    )(jnp.asarray(qi_l, jnp.int32), jnp.asarray(kj_l, jnp.int32), jnp.asarray(flag_l, jnp.int32), q, k, v)


def _attn_sample_kernel(q_ref, pckv_ref, pkr_ref, nckv_ref, nkr_ref, mabs_ref, wv_ref, o_ref, *, ls, past):
    hl = MLA_HEADS * ls
    qcat = jnp.concatenate(
        [_dot(q_ref[:, hd * HEAD_SLOT:(hd + 1) * HEAD_SLOT], mabs_ref[hd]) for hd in range(MLA_HEADS)],
        axis=0).astype(BF16)
    q_abs = qcat[:, :KV_RANK]
    q_rope = qcat[:, KV_RANK:]
    pckv = pckv_ref[...].astype(BF16)
    nckv = nckv_ref[...].astype(BF16)

    def pad_lanes(kr):
        return jnp.concatenate([kr, jnp.zeros((kr.shape[0], LANES - QK_ROPE), kr.dtype)], axis=1).astype(BF16)

    s_past = _dot_nt(q_abs, pckv) + _dot_nt(q_rope, pad_lanes(pkr_ref[...]))
    s_new = _dot_nt(q_abs, nckv) + _dot_nt(q_rope, pad_lanes(nkr_ref[...]))

    qpos_1 = lax.broadcasted_iota(jnp.int32, (ls, 1), 0) + past
    qchunk = jnp.concatenate([qpos_1] * MLA_HEADS, axis=0) >> CHUNK_SHIFT
    kchunk_past = lax.broadcasted_iota(jnp.int32, (hl, past), 1) >> CHUNK_SHIFT
    kchunk_new = (lax.broadcasted_iota(jnp.int32, (hl, ls), 1) + past) >> CHUNK_SHIFT
    s_past = jnp.where(kchunk_past <= qchunk, s_past, -jnp.inf)
    s_new = jnp.where(kchunk_new <= qchunk, s_new, -jnp.inf)

    m = jnp.maximum(jnp.max(s_past, axis=-1, keepdims=True), jnp.max(s_new, axis=-1, keepdims=True))
    p_past = jnp.exp2(s_past - m)
    p_new = jnp.exp2(s_new - m)
    denom = jnp.sum(p_past, axis=-1, keepdims=True) + jnp.sum(p_new, axis=-1, keepdims=True)
    olat = (_dot(p_past.astype(BF16), pckv) + _dot(p_new.astype(BF16), nckv)) / denom
    ofull = _dot(olat.astype(BF16), wv_ref[...])
    col_head = lax.broadcasted_iota(jnp.int32, (ls, V_WIDTH), 1) >> V_HEAD_SHIFT
    out = jnp.zeros((ls, V_WIDTH), F32)
    for hd in range(MLA_HEADS):
        out = out + jnp.where(col_head == hd, ofull[hd * ls:(hd + 1) * ls], 0.0)
    o_ref[...] = out.astype(BF16)


def _attn_sample(q, past_ckv, past_kr, new_ckv, new_kr, mabs, wv):
    b, ls, _ = q.shape
    past = past_ckv.shape[1]
    blk = lambda n, w: pl.BlockSpec((None, n, w), lambda i: (i, 0, 0))
    return pl.pallas_call(
        functools.partial(_attn_sample_kernel, ls=ls, past=past),
        grid=(b,),
        in_specs=[blk(ls, QK_WIDTH), blk(past, KV_RANK), blk(past, QK_ROPE), blk(ls, KV_RANK), blk(ls, QK_ROPE),
                  _const_spec(mabs.shape), _const_spec(wv.shape)],
        out_specs=blk(ls, V_WIDTH),
        out_shape=jax.ShapeDtypeStruct((b, ls, V_WIDTH), BF16),
        compiler_params=_params(("arbitrary",)),
        name="attn_sample",
    )(q, past_ckv, past_kr, new_ckv, new_kr, mabs, wv)


def _merge_kernel(o_ref, u_ref, vn_ref, ga_ref, gb_ref, x_ref, gate_ref, mix_ref, bias_ref,
                  wpa_ref, wpb_ref, wo_ref, x1_ref, sg_sc, *, chunk):
    tm = x_ref.shape[0]
    gw = SG_WIDTH // SG_GROUPS
    for c in range(tm // chunk):
        rows = slice(c * chunk, (c + 1) * chunk)
        for g in range(SG_GROUPS):
            cols = slice(g * gw, (g + 1) * gw)
            mixed = _dot(mix_ref[g], vn_ref[rows, cols].astype(BF16)) + bias_ref[:, cols]
            sg_sc[rows, cols] = (u_ref[rows, cols].astype(F32) * mixed).astype(BF16)
    ya = _dot(o_ref[...], wpa_ref[...])
    yb = _dot(sg_sc[...], wpb_ref[...])
    m = jax.nn.sigmoid(ga_ref[...].astype(F32)) * ya + jax.nn.sigmoid(gb_ref[...].astype(F32)) * yb
    x1_ref[...] = x_ref[...] + gate_ref[...] * _dot(m.astype(BF16), wo_ref[...])


def _merge(o, u, vn, ga, gb, x2d, gate, mixw, bias, wts, *, tm, chunk, per_row, tiles_per_batch):
    t, d = x2d.shape
    row = lambda w: pl.BlockSpec((tm, w), lambda i: (i, 0))
    consts = [mixw, bias, wts["w_pa"], wts["w_pb"], wts["w_o"]]
    return pl.pallas_call(
        functools.partial(_merge_kernel, chunk=chunk),
        grid=(t // tm,),
        in_specs=[row(V_WIDTH), row(SG_WIDTH), row(SG_WIDTH), row(d), row(d), row(d),
                  _mod_spec(per_row, tm, d, tiles_per_batch)] + [_const_spec(c.shape) for c in consts],
        out_specs=row(d),
        out_shape=jax.ShapeDtypeStruct((t, d), F32),
        scratch_shapes=[pltpu.VMEM((tm, SG_WIDTH), BF16)],
        compiler_params=_params(("arbitrary",)),
        name="merge",
    )(o, u, vn, ga, gb, x2d, gate, *consts)


def _router_kernel(xp_ref, shp_ref, scp_ref, xs_ref, shs_ref, scs_ref, g_ref, whi_ref, wlo_ref, rb_ref,
                   h2_ref, idx_ref, gate_ref, rank_ref, cnt_ref, carry_sc, *, n_prompt_tiles):
    i = pl.program_id(0)
    out_refs = (g_ref, whi_ref, wlo_ref, rb_ref, h2_ref, idx_ref, gate_ref, rank_ref, cnt_ref, carry_sc)

    @pl.when(i == 0)
    def _():
        carry_sc[...] = jnp.zeros(carry_sc.shape, F32)

    @pl.when(i < n_prompt_tiles)
    def _():
        _route_rows(xp_ref, shp_ref, scp_ref, *out_refs)

    @pl.when(i >= n_prompt_tiles)
    def _():
        _route_rows(xs_ref, shs_ref, scs_ref, *out_refs)


def _route_rows(x1_ref, sh_ref, sc_ref, g_ref, whi_ref, wlo_ref, rb_ref, h2_ref, idx_ref, gate_ref, rank_ref,
                cnt_ref, carry_sc):
    h2 = _rms(x1_ref[...], g_ref[...]) * (1.0 + sc_ref[...]) + sh_ref[...]
    h2_ref[...] = h2
    hi = h2.astype(BF16)
    lo = (h2 - hi.astype(F32)).astype(BF16)
    logits = _dot(hi, whi_ref[...]) + _dot(lo, whi_ref[...]) + _dot(hi, wlo_ref[...]) + rb_ref[...]
    tm = logits.shape[0]
    work = jnp.transpose(logits)[:N_EXPERTS]
    expert = lax.broadcasted_iota(jnp.int32, work.shape, 0)
    vals, idxs = [], []
    for _ in range(TOP_K):
        mx = jnp.max(work, axis=0, keepdims=True)
        ix = jnp.min(jnp.where(work == mx, expert, N_EXPERTS), axis=0, keepdims=True)
        vals.append(mx)
        idxs.append(ix)
        work = jnp.where(expert == ix, -jnp.inf, work)
    es = [jnp.exp(v - vals[0]) for v in vals]
    tot = es[0]
    for e in es[1:]:
        tot = tot + e

    onehot = jnp.zeros(work.shape, F32)
    for j in range(TOP_K):
        onehot = jnp.where(expert == idxs[j], 1.0, onehot)
    earlier = (lax.broadcasted_iota(jnp.int32, (tm, tm), 0) < lax.broadcasted_iota(jnp.int32, (tm, tm), 1))
    within = _dot(onehot.astype(BF16), jnp.where(earlier, 1.0, 0.0).astype(BF16))
    carry = carry_sc[...]
    rank_full = within + (jnp.tile(carry, (1, tm // LANES)) if tm >= LANES else carry[:, :tm])
    ranks = [jnp.sum(jnp.where(expert == idxs[j], rank_full, 0.0), axis=0, keepdims=True) for j in range(TOP_K)]
    carry_sc[...] = carry_sc[...] + jnp.sum(onehot, axis=1, keepdims=True)
    cnt_ref[...] = carry_sc[...]

    row = lax.broadcasted_iota(jnp.int32, (2 * ROW_GROUP, tm), 0)
    packed = jnp.zeros((2 * ROW_GROUP, tm), F32)
    for j in range(TOP_K):
        packed = jnp.where(row == j, idxs[j].astype(F32), packed)
        packed = jnp.where(row == TOP_K + j, es[j] / tot, packed)
        packed = jnp.where(row == 2 * TOP_K + j, ranks[j], packed)
    packed = jnp.concatenate([packed, jnp.zeros((LANES - 2 * ROW_GROUP, tm), F32)], axis=0)
    by_token = jnp.transpose(packed)
    idx_ref[...] = by_token[:, :TOP_K].astype(jnp.int32)
    gate_ref[...] = by_token[:, TOP_K:2 * TOP_K]
    rank_ref[...] = by_token[:, 2 * TOP_K:3 * TOP_K].astype(jnp.int32)


def _router(x1p, shift_p, scale_p, x1s, shift_s, scale_s, wts, *, tm, tiles_per_batch):
    tp, d = x1p.shape
    ts = x1s.shape[0]
    n_p, n_s = tp // tm, ts // tm
    t_all = tp + ts
    row = lambda w: pl.BlockSpec((tm, w), lambda i: (i, 0))
    p_row = pl.BlockSpec((tm, d), lambda i: (jnp.minimum(i, n_p - 1), 0))
    p_mod = pl.BlockSpec((None, 1, d), lambda i: (jnp.minimum(i, n_p - 1) // tiles_per_batch, 0, 0))
    s_row = pl.BlockSpec((tm, d), lambda i: (jnp.maximum(i - n_p, 0), 0))
    consts = [wts["gffn"], wts["rw_hi"], wts["rw_lo"], wts["rb"]]
    return pl.pallas_call(
        functools.partial(_router_kernel, n_prompt_tiles=n_p),
        grid=(n_p + n_s,),
        in_specs=[p_row, p_mod, p_mod, s_row, s_row, s_row] + [_const_spec(c.shape) for c in consts],
        out_specs=[row(d), row(TOP_K), row(TOP_K), row(TOP_K), _const_spec((N_EXPERTS, LANES))],
        out_shape=[jax.ShapeDtypeStruct((t_all, d), F32), jax.ShapeDtypeStruct((t_all, TOP_K), jnp.int32),
                   jax.ShapeDtypeStruct((t_all, TOP_K), F32), jax.ShapeDtypeStruct((t_all, TOP_K), jnp.int32),
                   jax.ShapeDtypeStruct((N_EXPERTS, LANES), F32)],
        scratch_shapes=[pltpu.VMEM((N_EXPERTS, LANES), F32)],
        compiler_params=_params(("arbitrary",)),
        name="router",
    )(x1p, shift_p, scale_p, x1s, shift_s, scale_s, *consts)


ROW_GROUP = 8


def _slot_offset(slot, n_rows):
    return slot * n_rows if isinstance(slot, int) else pl.multiple_of(slot * n_rows, n_rows)


def _row_gather_start(idx_smem, slot, src_hbm, dst_vmem, sem, n_rows):
    base = _slot_offset(slot, n_rows)

    def group(g, carry):
        r0 = g * ROW_GROUP
        for j in range(ROW_GROUP):
            pltpu.make_async_copy(src_hbm.at[pl.ds(idx_smem[base + r0 + j], 1)], dst_vmem.at[g, pl.ds(j, 1)],
                                  sem).start()
        return carry
    lax.fori_loop(0, n_rows // ROW_GROUP, group, 0)


def _row_gather_wait(dst_vmem, sem):
    pltpu.make_async_copy(dst_vmem, dst_vmem, sem).wait()


def _gather_pipeline(i, n_steps, idx_hbm, idx_smem, isem, src_hbm, buf, gsem, n_rows, first_tile):
    def idx_copy(blk, slot):
        return pltpu.make_async_copy(idx_hbm.at[first_tile + blk],
                                     idx_smem.at[pl.ds(_slot_offset(slot, n_rows), n_rows)], isem.at[slot])

    @pl.when(i == 0)
    def _():
        idx_copy(0, 0).start()
        idx_copy(0, 0).wait()
        _row_gather_start(idx_smem, 0, src_hbm, buf.at[0], gsem.at[0], n_rows)

        @pl.when(n_steps > 1)
        def _():
            idx_copy(1, 1).start()

    nxt = (i + 1) % 2

    @pl.when(i + 1 < n_steps)
    def _():
        idx_copy(i + 1, nxt).wait()
        _row_gather_start(idx_smem, nxt, src_hbm, buf.at[nxt], gsem.at[nxt], n_rows)

    @pl.when(i + 2 < n_steps)
    def _():
        idx_copy(i + 2, i % 2).start()

    _row_gather_wait(buf.at[i % 2], gsem.at[i % 2])


def _dispatch_kernel(pend_ref, nused_ref, dest_ref, h2_ref, xs_ref, zbuf, idx_smem, isem, csem, zsem, *,
                     tm, n_blocks):
    i = pl.program_id(0)
    n = pl.num_programs(0)
    rows = TOP_K * tm

    def idx_copy(blk, slot):
        return pltpu.make_async_copy(dest_ref.at[blk], idx_smem.at[pl.ds(_slot_offset(slot, rows), rows)],
                                     isem.at[slot])

    def zero_copy(block_start):
        start = pl.multiple_of(block_start, MOE_ROWS)
        return pltpu.make_async_copy(zbuf, xs_ref.at[pl.ds(start, MOE_ROWS)], zsem)

    def rows_done():
        return pltpu.make_async_copy(xs_ref.at[pl.ds(0, rows)], xs_ref.at[pl.ds(0, rows)], csem)

    @pl.when(i == 0)
    def _():
        idx_copy(0, 0).start()
        zbuf[...] = jnp.zeros(zbuf.shape, F32)
        n_used = nused_ref[0]

        def last_block(e, carry):
            zero_copy(jnp.maximum(pend_ref[e] - MOE_ROWS, 0)).start()
            return carry
        lax.fori_loop(0, N_EXPERTS, last_block, 0)

        def tail_block(b, carry):
            zero_copy(b * MOE_ROWS).start()
            return carry
        lax.fori_loop(n_used, n_blocks, tail_block, 0)

        def drain(b, carry):
            zero_copy(0).wait()
            return carry
        lax.fori_loop(0, N_EXPERTS + n_blocks - n_used, drain, 0)

    slot = i % 2
    idx_copy(i, slot).wait()

    @pl.when(i + 1 < n)
    def _():
        idx_copy(i + 1, 1 - slot).start()

    base = _slot_offset(slot, rows)

    def group(g, carry):
        r0 = g * ROW_GROUP
        for j in range(ROW_GROUP):
            src = h2_ref.at[g, pl.ds(j, 1)]
            for kk in range(TOP_K):
                dst = xs_ref.at[pl.ds(idx_smem[base + kk * tm + r0 + j], 1)]
                pltpu.make_async_copy(src, dst, csem).start()
        return carry
    lax.fori_loop(0, tm // ROW_GROUP, group, 0)
    rows_done().wait()


def _dispatch(pend, n_used, dest_tiles, h2_all, *, tm, n_blocks):
    t_all, d = h2_all.shape
    grid_spec = pltpu.PrefetchScalarGridSpec(
        num_scalar_prefetch=2,
        grid=(t_all // tm,),
        in_specs=[pl.BlockSpec(memory_space=pl.ANY),
                  pl.BlockSpec((tm // ROW_GROUP, ROW_GROUP, d), lambda i, pe, nu: (i, 0, 0))],
        out_specs=pl.BlockSpec(memory_space=pl.ANY),
        scratch_shapes=[pltpu.VMEM((MOE_ROWS, d), F32),
                        pltpu.SMEM((2 * TOP_K * tm,), jnp.int32),
                        pltpu.SemaphoreType.DMA((2,)),
                        pltpu.SemaphoreType.DMA(()),
                        pltpu.SemaphoreType.DMA(())],
    )
    return pl.pallas_call(
        functools.partial(_dispatch_kernel, tm=tm, n_blocks=n_blocks),
        grid_spec=grid_spec,
        out_shape=jax.ShapeDtypeStruct((n_blocks * MOE_ROWS, d), F32),
        compiler_params=_params(("arbitrary",)),
        name="dispatch",
    )(pend, n_used, dest_tiles, h2_all.reshape(t_all // ROW_GROUP, ROW_GROUP, d))


def _moe_kernel(be_ref, nused_ref, xs_ref, wgu_ref, bgu_ref, wdn_ref, bdn_ref, y_ref, wgu_bf, wdn_bf, *, d_model):
    i = pl.program_id(0)
    n_used = nused_ref[0]

    @pl.when(i < n_used)
    def _():
        prev = be_ref[jnp.maximum(i - 1, 0)]

        @pl.when((i == 0) | (be_ref[i] != prev))
        def _():
            wgu_bf[...] = wgu_ref[...].astype(BF16)
            wdn_bf[...] = wdn_ref[...].astype(BF16)

        xb = xs_ref[...].astype(BF16)
        gu = _dot(xb, wgu_bf[...]) + bgu_ref[...]
        g = jnp.minimum(gu[:, :d_model], SWIGLU_LIMIT)
        lin = jnp.clip(gu[:, d_model:], -SWIGLU_LIMIT, SWIGLU_LIMIT)
        act = g * jax.nn.sigmoid(SWIGLU_ALPHA * g) * (lin + 1.0)
        y_ref[...] = _dot(act.astype(BF16), wdn_bf[...]) + bdn_ref[...]

    @pl.when(i >= n_used)
    def _():
        y_ref[...] = jnp.zeros(y_ref.shape, F32)


def _moe_experts(block_e, n_used, xs, w_gu, b_gu, w_dn, b_dn):
    n_blocks = xs.shape[0] // MOE_ROWS
    e, d, d2 = w_gu.shape
    grid_spec = pltpu.PrefetchScalarGridSpec(
        num_scalar_prefetch=2,
        grid=(n_blocks,),
        in_specs=[pl.BlockSpec((MOE_ROWS, d), lambda i, be, nu: (jnp.minimum(i, nu[0] - 1), 0)),
                  pl.BlockSpec((None, d, d2), lambda i, be, nu: (be[i], 0, 0)),
                  pl.BlockSpec((None, 1, d2), lambda i, be, nu: (be[i], 0, 0)),
                  pl.BlockSpec((None, d, d), lambda i, be, nu: (be[i], 0, 0)),
                  pl.BlockSpec((None, 1, d), lambda i, be, nu: (be[i], 0, 0))],
        out_specs=pl.BlockSpec((MOE_ROWS, d), lambda i, be, nu: (i, 0)),
        scratch_shapes=[pltpu.VMEM((d, d2), BF16),
                        pltpu.VMEM((d, d), BF16)],
    )
    return pl.pallas_call(
        functools.partial(_moe_kernel, d_model=d),
        grid_spec=grid_spec,
        out_shape=jax.ShapeDtypeStruct((n_blocks * MOE_ROWS, d), F32),
        compiler_params=_params(("arbitrary",)),
        name="moe_experts",
    )(block_e, n_used, xs, w_gu, b_gu.reshape(e, 1, d2), w_dn, b_dn.reshape(e, 1, d))


def _combine_kernel(pos_ref, ys_ref, x1_ref, gate_ref, gm_ref, gfin_ref, y_ref, ybuf, idx_smem, isem, gsem, *,
                    tm, first_tile, final_norm):
    i = pl.program_id(0)
    _gather_pipeline(i, pl.num_programs(0), pos_ref, idx_smem, isem, ys_ref, ybuf, gsem, TOP_K * tm, first_tile)
    gate = gate_ref[...]
    f = jnp.zeros(x1_ref.shape, F32)
    groups = tm // ROW_GROUP
    for kk in range(TOP_K):
        rows = ybuf[i % 2, kk * groups:(kk + 1) * groups].reshape(x1_ref.shape)
        f = f + gate[:, kk:kk + 1] * rows
    x2 = x1_ref[...] + gm_ref[...] * f
    y_ref[...] = _rms(x2, gfin_ref[...]) if final_norm else x2


def _combine(pos_tiles, ys, x1, gate, g_m, final_g, *, tm, first_tile, per_row, tiles_per_batch, final_norm):
    t, d = x1.shape
    row = lambda w: pl.BlockSpec((tm, w), lambda i: (i, 0))
    return pl.pallas_call(
        functools.partial(_combine_kernel, tm=tm, first_tile=first_tile, final_norm=final_norm),
        grid=(t // tm,),
        in_specs=[pl.BlockSpec(memory_space=pl.ANY), pl.BlockSpec(memory_space=pl.ANY), row(d),
                  pl.BlockSpec((tm, TOP_K), lambda i: (first_tile + i, 0)),
                  _mod_spec(per_row, tm, d, tiles_per_batch), _const_spec((1, d))],
        out_specs=row(d),
        out_shape=jax.ShapeDtypeStruct((t, d), F32),
        scratch_shapes=[pltpu.VMEM((2, TOP_K * tm // ROW_GROUP, ROW_GROUP, d), F32),
                        pltpu.SMEM((2 * TOP_K * tm,), jnp.int32),
                        pltpu.SemaphoreType.DMA((2,)),
                        pltpu.SemaphoreType.DMA((2,))],
        compiler_params=_params(("arbitrary",)),
        name="combine",
    )(pos_tiles, ys, x1, gate, g_m, final_g)


def _rope_tables(pos):
    inv = ROPE_THETA ** (-jnp.arange(0, QK_ROPE, 2, dtype=F32) / QK_ROPE)
    ang = pos.astype(F32)[:, None] * inv[None, :]
    cos, sin = jnp.cos(ang), jnp.sin(ang)
    n = pos.shape[0]
    cc = jnp.concatenate([cos, cos, jnp.ones((n, LANES - QK_ROPE), F32)], axis=1)
    ss = jnp.concatenate([sin, sin, jnp.zeros((n, LANES - QK_ROPE), F32)], axis=1)
    return cc, ss


def _swap_halves(w):
    half = QK_ROPE // 2
    return jnp.concatenate([-w[..., half:], w[..., :half]], axis=-1)


def _layer_weights(l, w_in, norm_mix_g, q_norm_g, w_uq, kv_norm_g, w_uk, w_uv, w_pa, sg_norm_g, sg_norm_b,
                   w_pb, w_o, norm_ffn_g, router_w, router_b):
    d = w_in.shape[1]
    wi = w_in[l]
    o_kr = Q_RANK + KV_RANK
    o_u = o_kr + QK_ROPE
    o_v = o_u + SG_WIDTH
    o_ga = o_v + SG_WIDTH
    kr = wi[:, o_kr:o_u]
    zpad = jnp.zeros((d, LANES - QK_ROPE), F32)
    w_in_r = jnp.concatenate([wi[:, :o_kr], wi[:, o_u:o_ga], wi[:, o_ga:],
                              kr, zpad, _swap_halves(kr), zpad], axis=1).astype(BF16)
    uq = w_uq[l]
    nope, rope = uq[..., :QK_NOPE], uq[..., QK_NOPE:]
    z32 = jnp.zeros(rope.shape[:2] + (HEAD_SLOT - QK_NOPE - QK_ROPE,), F32)
    wq = jnp.concatenate([rope, nope, z32], axis=-1).reshape(Q_RANK, QK_WIDTH).astype(BF16)
    wqs = jnp.concatenate([_swap_halves(rope), jnp.zeros_like(nope), z32], axis=-1)
    wqs = wqs.reshape(Q_RANK, QK_WIDTH).astype(BF16)
    uk = w_uk[l]
    zk_lo = jnp.zeros(uk.shape[:2] + (QK_ROPE,), F32)
    zk_hi = jnp.zeros(uk.shape[:2] + (HEAD_SLOT - QK_NOPE - QK_ROPE,), F32)
    wk = jnp.concatenate([zk_lo, uk, zk_hi], axis=-1).reshape(KV_RANK, QK_WIDTH).astype(BF16)
    wv = w_uv[l].reshape(KV_RANK, V_WIDTH).astype(BF16)
    wv_slot = jnp.concatenate([w_uv[l], jnp.zeros_like(w_uv[l])], axis=-1).reshape(KV_RANK, QK_WIDTH).astype(BF16)
    vone = jnp.tile(jnp.concatenate([jnp.zeros((V_HEAD,), F32), jnp.ones((HEAD_SLOT - V_HEAD,), F32)]),
                    MLA_HEADS).reshape(1, QK_WIDTH)
    ukt = jnp.transpose(uk, (1, 2, 0))
    eye = jnp.broadcast_to(jnp.eye(QK_ROPE, LANES, dtype=F32), (MLA_HEADS, QK_ROPE, LANES))
    top = jnp.concatenate([jnp.zeros((MLA_HEADS, QK_ROPE, KV_RANK), F32), eye], axis=-1)
    mid = jnp.concatenate([ukt, jnp.zeros((MLA_HEADS, QK_NOPE, LANES), F32)], axis=-1)
    bot = jnp.zeros((MLA_HEADS, HEAD_SLOT - QK_NOPE - QK_ROPE, KV_RANK + LANES), F32)
    mabs = jnp.concatenate([top, mid, bot], axis=1).astype(BF16)
    rw = jnp.pad(router_w[l], ((0, 0), (0, ROUTER_PAD - N_EXPERTS)))
    rw_hi = rw.astype(BF16)
    rw_lo = (rw - rw_hi.astype(F32)).astype(BF16)
    rb = jnp.concatenate([router_b[l], jnp.full((ROUTER_PAD - N_EXPERTS,), NEG_BIG, F32)]).reshape(1, ROUTER_PAD)
    return dict(
        w_in_r=w_in_r, gmix=norm_mix_g[l].reshape(1, d),
        gq=(q_norm_g[l] * (ATTN_SCALE * LOG2_E)).reshape(1, Q_RANK),
        gkv=kv_norm_g[l].reshape(1, KV_RANK), wq=wq, wqs=wqs, wk=wk, wv=wv, wv_slot=wv_slot, vone=vone, mabs=mabs,
        sgg=sg_norm_g[l].reshape(1, SG_WIDTH), sgb=sg_norm_b[l].reshape(1, SG_WIDTH),
        w_pa=w_pa[l].astype(BF16), w_pb=w_pb[l].astype(BF16), w_o=w_o[l].astype(BF16),
        gffn=norm_ffn_g[l].reshape(1, d), rw_hi=rw_hi, rw_lo=rw_lo, rb=rb)


def _spatial_mix_weights(w_s, b_s, seq, n_batch):
    gw = SG_WIDTH // SG_GROUPS
    tril = jnp.tril(jnp.ones((SG_CHUNK, SG_CHUNK), dtype=bool))
    w = jnp.where(tril[None], w_s, 0.0)
    if seq % SG_CHUNK == 0:
        mixw = w
        bias_t = b_s
    else:
        assert seq < SG_CHUNK
        blk = w[:, :seq, :seq]
        eye = jnp.eye(n_batch, dtype=F32)
        mixw = jnp.einsum("ab,gts->gatbs", eye, blk).reshape(SG_GROUPS, n_batch * seq, n_batch * seq)
        bias_t = jnp.tile(b_s[:, :seq], (1, n_batch))
    bias = jnp.repeat(jnp.transpose(bias_t), gw, axis=1)
    return mixw.astype(BF16), bias


def _routing_tables(idx, rank, counts_f, n_blocks):
    counts = counts_f[:, 0].astype(jnp.int32)
    padded = (counts + MOE_ROWS - 1) // MOE_ROWS * MOE_ROWS
    pend = jnp.cumsum(padded).astype(jnp.int32)
    pstart = pend - padded
    experts = jnp.arange(N_EXPERTS, dtype=jnp.int32)
    dest = rank + jnp.sum(jnp.where(idx[..., None] == experts, pstart, 0), axis=-1)
    block_start = jnp.arange(n_blocks, dtype=jnp.int32) * MOE_ROWS
    block_e = jnp.minimum(jnp.sum((pend[None, :] <= block_start[:, None]).astype(jnp.int32), axis=1), N_EXPERTS - 1)
    n_used = (pend[-1:] // MOE_ROWS).astype(jnp.int32)
    return dest.astype(jnp.int32), pend, block_e.astype(jnp.int32), n_used


def _pos_tiles(pos, tm):
    t = pos.shape[0]
    return jnp.transpose(pos.reshape(t // tm, tm, TOP_K), (0, 2, 1)).reshape(t // tm, TOP_K * tm)


def _stack_layers(per_layer):
    return per_layer[0][None] if len(per_layer) == 1 else jnp.stack(per_layer)


def _pick_tile(n, pref):
    t = min(n, pref)
    assert n % t == 0 and t % 8 == 0
    return t


def kernel(x_prompt, x_sample, cache_ckv, cache_krope, c_prompt, c_sample, ada_w, ada_b, norm_mix_g, w_in, q_norm_g, w_uq, kv_norm_g, w_uk, w_uv, w_pa, sg_norm_g, sg_norm_b, w_spatial, b_spatial, w_pb, w_o, norm_ffn_g, router_w, router_b, w_gu, b_gu, w_dn, b_dn, final_g):
    bp, lp, d = x_prompt.shape
    bs, ls, _ = x_sample.shape
    depth = w_in.shape[0]
    past = cache_ckv.shape[2]
    tp, ts = bp * lp, bs * ls
    assert lp % SG_CHUNK == 0 and ls <= SG_CHUNK

    tm_p = _pick_tile(lp, INPROJ_TILE)
    tm_s = _pick_tile(ts, SAMPLE_TILE)
    t_attn = _pick_tile(lp, ATTN_TILE)
    tpb = lp // tm_p

    cc_p, ss_p = _rope_tables(jnp.arange(lp, dtype=jnp.int32))
    cc_s, ss_s = _rope_tables(past + jnp.arange(ls, dtype=jnp.int32))
    cc_s, ss_s = jnp.tile(cc_s, (bs, 1)), jnp.tile(ss_s, (bs, 1))

    b_all = bp + bs
    b_pad = -(-b_all // 8) * 8
    c_all = jnp.concatenate([c_prompt, c_sample, jnp.zeros((b_pad - b_all, d), F32)], axis=0)

    xp = x_prompt.reshape(tp, d)
    xs = x_sample.reshape(ts, d)
    outs = dict(ckv_p=[], kr_p=[], ckv_s=[], kr_s=[], v_s=[])
    final_g2 = final_g.reshape(1, d)
    for l in range(depth):
        wts = _layer_weights(l, w_in, norm_mix_g, q_norm_g, w_uq, kv_norm_g, w_uk, w_uv, w_pa, sg_norm_g,
                             sg_norm_b, w_pb, w_o, norm_ffn_g, router_w, router_b)
        mod = _adaln(c_all, ada_w[l], ada_b[l])
        mod_p = [mod[:bp, j * d:(j + 1) * d].reshape(bp, 1, d) for j in range(6)]
        mod_s = [jnp.repeat(mod[bp:b_all, j * d:(j + 1) * d], ls, axis=0) for j in range(6)]

        q, k, v, ckv, kr, u, vn, ga, gb = _inproj(xp, mod_p[0], mod_p[1], cc_p, ss_p, wts, tm=tm_p, per_row=False,
                                                  tiles_per_batch=tpb, vn_dtype=BF16)
        o = _attn_prompt(q.reshape(bp, lp, QK_WIDTH), k.reshape(bp, lp, QK_WIDTH), v.reshape(bp, lp, QK_WIDTH),
                         tq=t_attn, tk=t_attn).reshape(tp, V_WIDTH)
        mixw, bias = _spatial_mix_weights(w_spatial[l], b_spatial[l], lp, bp)
        tm_m = _pick_tile(lp, MERGE_TILE)
        x1p = _merge(o, u, vn, ga, gb, xp, mod_p[2], mixw, bias, wts, tm=tm_m, chunk=SG_CHUNK, per_row=False,
                     tiles_per_batch=lp // tm_m)
        outs["ckv_p"].append(ckv.reshape(bp, lp, KV_RANK))
        outs["kr_p"].append(kr.reshape(bp, lp, QK_ROPE))

        q, k, v, ckv, kr, u, vn, ga, gb = _inproj(xs, mod_s[0], mod_s[1], cc_s, ss_s, wts, tm=tm_s, per_row=True,
                                                  tiles_per_batch=1, vn_dtype=F32)
        ckv3, kr3 = ckv.reshape(bs, ls, KV_RANK), kr.reshape(bs, ls, QK_ROPE)
        o = _attn_sample(q.reshape(bs, ls, QK_WIDTH), cache_ckv[l], cache_krope[l], ckv3, kr3,
                         wts["mabs"], wts["wv"]).reshape(ts, V_WIDTH)
        mixw, bias = _spatial_mix_weights(w_spatial[l], b_spatial[l], ls, tm_s // ls)
        x1s = _merge(o, u, vn, ga, gb, xs, mod_s[2], mixw, bias, wts, tm=tm_s, chunk=tm_s, per_row=True,
                     tiles_per_batch=1)
        outs["ckv_s"].append(ckv3)
        outs["kr_s"].append(kr3)
        outs["v_s"].append(vn.reshape(bs, ls, SG_WIDTH))

        t_all = tp + ts
        tm_r = _pick_tile(math.gcd(tp, ts), ROW_TILE)
        h2_all, idx, gate, rank, counts = _router(x1p, mod_p[3], mod_p[4], x1s, mod_s[3], mod_s[4], wts, tm=tm_r,
                                                  tiles_per_batch=lp // tm_r)
        n_blocks = -(-(t_all * TOP_K) // MOE_ROWS) + N_EXPERTS
        pos, pend, block_e, n_used = _routing_tables(idx, rank, counts, n_blocks)
        pos_tiles = _pos_tiles(pos, tm_r)
        x_sorted = _dispatch(pend, n_used, pos_tiles, h2_all, tm=tm_r, n_blocks=n_blocks)
        ys = _moe_experts(block_e, n_used, x_sorted, w_gu[l], b_gu[l], w_dn[l], b_dn[l])
        last = l == depth - 1
        xp = _combine(pos_tiles, ys, x1p, gate, mod_p[5], final_g2, tm=tm_r, first_tile=0, per_row=False,
                      tiles_per_batch=lp // tm_r, final_norm=last)
        xs = _combine(pos_tiles, ys, x1s, gate, mod_s[5], final_g2, tm=tm_r, first_tile=tp // tm_r, per_row=True,
                      tiles_per_batch=1, final_norm=last)
    return (xp.reshape(bp, lp, d), xs.reshape(bs, ls, d),
            _stack_layers(outs["ckv_p"]), _stack_layers(outs["kr_p"]),
            _stack_layers(outs["ckv_s"]), _stack_layers(outs["kr_s"]), _stack_layers(outs["v_s"]))
```

```python
import functools
import math

import jax
import jax.numpy as jnp
from jax import lax
from jax.experimental import pallas as pl
from jax.experimental.pallas import tpu as pltpu

F32 = jnp.float32
BF16 = jnp.bfloat16

LANES = 128
VMEM_LIMIT_BYTES = 56 * 1024 * 1024

CHUNK = 64
CHUNK_SHIFT = 6
MLA_HEADS = 8
QK_NOPE = 64
QK_ROPE = 32
V_HEAD = 64
V_HEAD_SHIFT = 6
Q_RANK = 384
KV_RANK = 256
ROPE_THETA = 10000.0
ATTN_SCALE = 1.0 / math.sqrt(QK_NOPE + QK_ROPE)
LOG2_E = math.log2(math.e)
SG_CHUNK = 128
SG_GROUPS = 4
SG_WIDTH = 512
N_EXPERTS = 32
TOP_K = 4
SWIGLU_LIMIT = 7.0
SWIGLU_ALPHA = 1.702
EPS = 1e-6

HEAD_SLOT = LANES
QK_WIDTH = MLA_HEADS * HEAD_SLOT
V_WIDTH = MLA_HEADS * V_HEAD
MOE_ROWS = 512
ROW_TILE = 256
INPROJ_TILE = 512
MERGE_TILE = 512
SAMPLE_TILE = 512
ATTN_TILE = 1024
ATTN_SUB_KEYS = 256
ROUTER_PAD = LANES
NEG_BIG = -1e30

_C_CQ = 0
_C_CKV = _C_CQ + Q_RANK
_C_U = _C_CKV + KV_RANK
_C_V = _C_U + SG_WIDTH
_C_GA = _C_V + SG_WIDTH


def _params(sem):
    return pltpu.CompilerParams(dimension_semantics=sem, vmem_limit_bytes=VMEM_LIMIT_BYTES)


def _dot(a, b):
    return jnp.dot(a, b, preferred_element_type=F32)


def _dot_nt(a, b):
    return lax.dot_general(a, b, (((1,), (1,)), ((), ())), preferred_element_type=F32)


def _rms(x, g):
    return x * lax.rsqrt(jnp.mean(x * x, axis=-1, keepdims=True) + EPS) * g


def _adaln_kernel(c_ref, w_ref, b_ref, o_ref):
    c = c_ref[...]
    s = (c * jax.nn.sigmoid(c)).astype(BF16)
    o_ref[...] = _dot(s, w_ref[...].astype(BF16)) + b_ref[...]


def _adaln(c_all, ada_w, ada_b):
    bp, d = c_all.shape
    n = ada_w.shape[1]
    return pl.pallas_call(
        _adaln_kernel,
        grid=(n // d,),
        in_specs=[pl.BlockSpec((bp, d), lambda j: (0, 0)),
                  pl.BlockSpec((d, d), lambda j: (0, j)),
                  pl.BlockSpec((1, d), lambda j: (0, j))],
        out_specs=pl.BlockSpec((bp, d), lambda j: (0, j)),
        out_shape=jax.ShapeDtypeStruct((bp, n), F32),
        compiler_params=_params(("arbitrary",)),
        name="adaln",
    )(c_all, ada_w, ada_b.reshape(1, n))


def _inproj_kernel(x_ref, sh_ref, sc_ref, gmix_ref, cc_ref, ss_ref, win_ref, gq_ref, gkv_ref,
                   wq_ref, wqs_ref, wk_ref, wv_ref, vone_ref, sgg_ref, sgb_ref,
                   q_ref, k_ref, v_ref, ckv_ref, kr_ref, u_ref, vn_ref, ga_ref, gb_ref, *, d_model):
    x = x_ref[...]
    h = (_rms(x, gmix_ref[...]) * (1.0 + sc_ref[...]) + sh_ref[...]).astype(BF16)

    def proj(lo, width):
        return _dot(h, win_ref[:, lo:lo + width])

    cc = cc_ref[...]
    ss = ss_ref[...]
    c_gb = _C_GA + d_model
    c_kra = c_gb + d_model
    c_krb = c_kra + LANES

    cqn = _rms(proj(_C_CQ, Q_RANK), gq_ref[...]).astype(BF16)
    qa = _dot(cqn, wq_ref[...])
    qb = _dot(cqn, wqs_ref[...])
    for hd in range(MLA_HEADS):
        sl = slice(hd * HEAD_SLOT, (hd + 1) * HEAD_SLOT)
        q_ref[:, sl] = (qa[:, sl] * cc + qb[:, sl] * ss).astype(BF16)

    ckvn = _rms(proj(_C_CKV, KV_RANK), gkv_ref[...])
    ckv_ref[...] = ckvn
    ckvb = ckvn.astype(BF16)
    krs = proj(c_kra, LANES) * cc + proj(c_krb, LANES) * ss
    kr_ref[...] = krs[:, :QK_ROPE]
    kn = _dot(ckvb, wk_ref[...])
    for hd in range(MLA_HEADS):
        sl = slice(hd * HEAD_SLOT, (hd + 1) * HEAD_SLOT)
        k_ref[:, sl] = (kn[:, sl] + krs).astype(BF16)
    v_ref[...] = (_dot(ckvb, wv_ref[...]) + vone_ref[...]).astype(BF16)

    u_ref[...] = proj(_C_U, SG_WIDTH).astype(u_ref.dtype)
    vv = proj(_C_V, SG_WIDTH)
    mu = jnp.mean(vv, axis=-1, keepdims=True)
    vc = vv - mu
    var = jnp.mean(vc * vc, axis=-1, keepdims=True)
    vn_ref[...] = (vc * lax.rsqrt(var + EPS) * sgg_ref[...] + sgb_ref[...]).astype(vn_ref.dtype)
    ga_ref[...] = proj(_C_GA, d_model).astype(BF16)
    gb_ref[...] = proj(c_gb, d_model).astype(BF16)


def _mod_spec(per_row, tm, d, tiles_per_batch):
    if per_row:
        return pl.BlockSpec((tm, d), lambda i: (i, 0))
    return pl.BlockSpec((None, 1, d), lambda i: (i // tiles_per_batch, 0, 0))


def _const_spec(shape):
    nd = len(shape)
    return pl.BlockSpec(shape, lambda i: (0,) * nd)


def _inproj(x2d, shift, scale, cc, ss, wts, *, tm, per_row, tiles_per_batch, vn_dtype):
    t, d = x2d.shape
    n_tab = cc.shape[0] // tm
    row = lambda w: pl.BlockSpec((tm, w), lambda i: (i, 0))
    tab = pl.BlockSpec((tm, LANES), lambda i: (i % n_tab, 0))
    mod = _mod_spec(per_row, tm, d, tiles_per_batch)
    consts = [wts["w_in_r"], wts["gq"], wts["gkv"], wts["wq"], wts["wqs"], wts["wk"], wts["wv_slot"],
              wts["vone"], wts["sgg"], wts["sgb"]]
    out_shapes = [jax.ShapeDtypeStruct((t, QK_WIDTH), BF16), jax.ShapeDtypeStruct((t, QK_WIDTH), BF16),
                  jax.ShapeDtypeStruct((t, QK_WIDTH), BF16), jax.ShapeDtypeStruct((t, KV_RANK), F32),
                  jax.ShapeDtypeStruct((t, QK_ROPE), F32), jax.ShapeDtypeStruct((t, SG_WIDTH), BF16),
                  jax.ShapeDtypeStruct((t, SG_WIDTH), vn_dtype), jax.ShapeDtypeStruct((t, d), BF16),
                  jax.ShapeDtypeStruct((t, d), BF16)]
    return pl.pallas_call(
        functools.partial(_inproj_kernel, d_model=d),
        grid=(t // tm,),
        in_specs=[row(d), mod, mod, _const_spec((1, d)), tab, tab] + [_const_spec(c.shape) for c in consts],
        out_specs=[row(s.shape[1]) for s in out_shapes],
        out_shape=out_shapes,
        compiler_params=_params(("arbitrary",)),
        name="inproj",
    )(x2d, shift, scale, wts["gmix"], cc, ss, *consts)


def _attn_kernel(qi_ref, kj_ref, flag_ref, q_ref, k_ref, v_ref, o_ref, m_sc, acc_sc, *, tq, tk, sub):
    s_id = pl.program_id(1)
    qi = qi_ref[s_id]
    kj = kj_ref[s_id]
    flags = flag_ref[s_id]

    @pl.when(kj == 0)
    def _():
        m_sc[...] = jnp.full(m_sc.shape, -jnp.inf, F32)
        acc_sc[...] = jnp.zeros(acc_sc.shape, F32)

    def sweep(bias):
        for kb in range(tk // sub):
            keys = slice(kb * sub, (kb + 1) * sub)
            rows = slice(kb * sub if bias is not None else 0, tq)
            for hd in range(MLA_HEADS):
                sl = slice(hd * HEAD_SLOT, (hd + 1) * HEAD_SLOT)
                s = _dot_nt(q_ref[rows, sl], k_ref[keys, sl])
                if bias is not None:
                    s = s + bias[rows, keys]
                tiles = [s[:, c * LANES:(c + 1) * LANES] for c in range(sub // LANES)]
                m_tile = tiles[0]
                for t in tiles[1:]:
                    m_tile = jnp.maximum(m_tile, t)
                m_old = m_sc[hd, rows]
                m_new = jnp.maximum(m_old, jnp.max(m_tile, axis=-1, keepdims=True))
                alpha = jnp.exp2(m_old - m_new)
                p = jnp.concatenate([jnp.exp2(t - m_new).astype(BF16) for t in tiles], axis=1)
                acc_sc[hd, rows] = alpha * acc_sc[hd, rows] + _dot(p, v_ref[keys, sl])
                m_sc[hd, rows] = m_new

    @pl.when((flags & 2) == 0)
    def _():
        sweep(None)

    @pl.when((flags & 2) != 0)
    def _():
        row = lax.broadcasted_iota(jnp.int32, (tq, tk), 0) + qi * tq
        col = lax.broadcasted_iota(jnp.int32, (tq, tk), 1) + kj * tk
        sweep(jnp.where((col >> CHUNK_SHIFT) <= (row >> CHUNK_SHIFT), 0.0, -jnp.inf))

    @pl.when((flags & 1) != 0)
    def _():
        lane = lax.broadcasted_iota(jnp.int32, (tq, LANES), 1)
        for pr in range(MLA_HEADS // 2):
            outs = []
            for hd in (2 * pr, 2 * pr + 1):
                acc = acc_sc[hd]
                outs.append(acc / pltpu.roll(acc, V_HEAD, axis=1))
            pair = jnp.where(lane < V_HEAD, outs[0], pltpu.roll(outs[1], V_HEAD, axis=1))
            o_ref[:, pr * LANES:(pr + 1) * LANES] = pair.astype(BF16)


def _attn_prompt(q, k, v, *, tq, tk):
    b, l, _ = q.shape
    assert tq == tk
    nq = l // tq
    qi_l, kj_l, flag_l = [], [], []
    for i in range(nq):
        n_kv = ((i + 1) * tq - 1) // tk + 1
        for j in range(n_kv):
            qi_l.append(i)
            kj_l.append(j)
            masked = ((j + 1) * tk - 1) // CHUNK > (i * tq) // CHUNK
            flag_l.append((1 if j == n_kv - 1 else 0) | (2 if masked else 0))
    steps = len(qi_l)
    grid_spec = pltpu.PrefetchScalarGridSpec(
        num_scalar_prefetch=3,
        grid=(b, steps),
        in_specs=[pl.BlockSpec((None, tq, QK_WIDTH), lambda bi, s, qi, kj, fl: (bi, qi[s], 0)),
                  pl.BlockSpec((None, tk, QK_WIDTH), lambda bi, s, qi, kj, fl: (bi, kj[s], 0)),
                  pl.BlockSpec((None, tk, QK_WIDTH), lambda bi, s, qi, kj, fl: (bi, kj[s], 0))],
        out_specs=pl.BlockSpec((None, tq, V_WIDTH), lambda bi, s, qi, kj, fl: (bi, qi[s], 0)),
        scratch_shapes=[pltpu.VMEM((MLA_HEADS, tq, LANES), F32), pltpu.VMEM((MLA_HEADS, tq, LANES), F32)],
    )
    return pl.pallas_call(
        functools.partial(_attn_kernel, tq=tq, tk=tk, sub=min(tk, ATTN_SUB_KEYS)),
        grid_spec=grid_spec,
        out_shape=jax.ShapeDtypeStruct((b, l, V_WIDTH), BF16),
        compiler_params=_params(("arbitrary", "arbitrary")),
        name="attn_prompt",
    )(jnp.asarray(qi_l, jnp.int32), jnp.asarray(kj_l, jnp.int32), jnp.asarray(flag_l, jnp.int32), q, k, v)


def _attn_sample_kernel(q_ref, pckv_ref, pkr_ref, nckv_ref, nkr_ref, mabs_ref, wv_ref, o_ref, *, ls, past):
    hl = MLA_HEADS * ls
    qcat = jnp.concatenate(
        [_dot(q_ref[:, hd * HEAD_SLOT:(hd + 1) * HEAD_SLOT], mabs_ref[hd]) for hd in range(MLA_HEADS)],
        axis=0).astype(BF16)
    q_abs = qcat[:, :KV_RANK]
    q_rope = qcat[:, KV_RANK:]
    pckv = pckv_ref[...].astype(BF16)
    nckv = nckv_ref[...].astype(BF16)

    def pad_lanes(kr):
        return jnp.concatenate([kr, jnp.zeros((kr.shape[0], LANES - QK_ROPE), kr.dtype)], axis=1).astype(BF16)

    s_past = _dot_nt(q_abs, pckv) + _dot_nt(q_rope, pad_lanes(pkr_ref[...]))
    s_new = _dot_nt(q_abs, nckv) + _dot_nt(q_rope, pad_lanes(nkr_ref[...]))

    qpos_1 = lax.broadcasted_iota(jnp.int32, (ls, 1), 0) + past
    qchunk = jnp.concatenate([qpos_1] * MLA_HEADS, axis=0) >> CHUNK_SHIFT
    kchunk_past = lax.broadcasted_iota(jnp.int32, (hl, past), 1) >> CHUNK_SHIFT
    kchunk_new = (lax.broadcasted_iota(jnp.int32, (hl, ls), 1) + past) >> CHUNK_SHIFT
    s_past = jnp.where(kchunk_past <= qchunk, s_past, -jnp.inf)
    s_new = jnp.where(kchunk_new <= qchunk, s_new, -jnp.inf)

    m = jnp.maximum(jnp.max(s_past, axis=-1, keepdims=True), jnp.max(s_new, axis=-1, keepdims=True))
    p_past = jnp.exp2(s_past - m)
    p_new = jnp.exp2(s_new - m)
    denom = jnp.sum(p_past, axis=-1, keepdims=True) + jnp.sum(p_new, axis=-1, keepdims=True)
    olat = (_dot(p_past.astype(BF16), pckv) + _dot(p_new.astype(BF16), nckv)) / denom
    ofull = _dot(olat.astype(BF16), wv_ref[...])
    col_head = lax.broadcasted_iota(jnp.int32, (ls, V_WIDTH), 1) >> V_HEAD_SHIFT
    out = jnp.zeros((ls, V_WIDTH), F32)
    for hd in range(MLA_HEADS):
        out = out + jnp.where(col_head == hd, ofull[hd * ls:(hd + 1) * ls], 0.0)
    o_ref[...] = out.astype(BF16)


def _attn_sample(q, past_ckv, past_kr, new_ckv, new_kr, mabs, wv):
    b, ls, _ = q.shape
    past = past_ckv.shape[1]
    blk = lambda n, w: pl.BlockSpec((None, n, w), lambda i: (i, 0, 0))
    return pl.pallas_call(
        functools.partial(_attn_sample_kernel, ls=ls, past=past),
        grid=(b,),
        in_specs=[blk(ls, QK_WIDTH), blk(past, KV_RANK), blk(past, QK_ROPE), blk(ls, KV_RANK), blk(ls, QK_ROPE),
                  _const_spec(mabs.shape), _const_spec(wv.shape)],
        out_specs=blk(ls, V_WIDTH),
        out_shape=jax.ShapeDtypeStruct((b, ls, V_WIDTH), BF16),
        compiler_params=_params(("arbitrary",)),
        name="attn_sample",
    )(q, past_ckv, past_kr, new_ckv, new_kr, mabs, wv)


def _merge_kernel(o_ref, u_ref, vn_ref, ga_ref, gb_ref, x_ref, gate_ref, mix_ref, bias_ref,
                  wpa_ref, wpb_ref, wo_ref, x1_ref, sg_sc, *, chunk):
    tm = x_ref.shape[0]
    gw = SG_WIDTH // SG_GROUPS
    for c in range(tm // chunk):
        rows = slice(c * chunk, (c + 1) * chunk)
        for g in range(SG_GROUPS):
            cols = slice(g * gw, (g + 1) * gw)
            mixed = _dot(mix_ref[g], vn_ref[rows, cols].astype(BF16)) + bias_ref[:, cols]
            sg_sc[rows, cols] = (u_ref[rows, cols].astype(F32) * mixed).astype(BF16)
    ya = _dot(o_ref[...], wpa_ref[...])
    yb = _dot(sg_sc[...], wpb_ref[...])
    m = jax.nn.sigmoid(ga_ref[...].astype(F32)) * ya + jax.nn.sigmoid(gb_ref[...].astype(F32)) * yb
    x1_ref[...] = x_ref[...] + gate_ref[...] * _dot(m.astype(BF16), wo_ref[...])


def _merge(o, u, vn, ga, gb, x2d, gate, mixw, bias, wts, *, tm, chunk, per_row, tiles_per_batch):
    t, d = x2d.shape
    row = lambda w: pl.BlockSpec((tm, w), lambda i: (i, 0))
    consts = [mixw, bias, wts["w_pa"], wts["w_pb"], wts["w_o"]]
    return pl.pallas_call(
        functools.partial(_merge_kernel, chunk=chunk),
        grid=(t // tm,),
        in_specs=[row(V_WIDTH), row(SG_WIDTH), row(SG_WIDTH), row(d), row(d), row(d),
                  _mod_spec(per_row, tm, d, tiles_per_batch)] + [_const_spec(c.shape) for c in consts],
        out_specs=row(d),
        out_shape=jax.ShapeDtypeStruct((t, d), F32),
        scratch_shapes=[pltpu.VMEM((tm, SG_WIDTH), BF16)],
        compiler_params=_params(("arbitrary",)),
        name="merge",
    )(o, u, vn, ga, gb, x2d, gate, *consts)


def _router_kernel(xp_ref, shp_ref, scp_ref, xs_ref, shs_ref, scs_ref, g_ref, whi_ref, wlo_ref, rb_ref,
                   h2_ref, idx_ref, gate_ref, rank_ref, cnt_ref, carry_sc, *, n_prompt_tiles):
    i = pl.program_id(0)
    out_refs = (g_ref, whi_ref, wlo_ref, rb_ref, h2_ref, idx_ref, gate_ref, rank_ref, cnt_ref, carry_sc)

    @pl.when(i == 0)
    def _():
        carry_sc[...] = jnp.zeros(carry_sc.shape, F32)

    @pl.when(i < n_prompt_tiles)
    def _():
        _route_rows(xp_ref, shp_ref, scp_ref, *out_refs)

    @pl.when(i >= n_prompt_tiles)
    def _():
        _route_rows(xs_ref, shs_ref, scs_ref, *out_refs)


def _route_rows(x1_ref, sh_ref, sc_ref, g_ref, whi_ref, wlo_ref, rb_ref, h2_ref, idx_ref, gate_ref, rank_ref,
                cnt_ref, carry_sc):
    h2 = _rms(x1_ref[...], g_ref[...]) * (1.0 + sc_ref[...]) + sh_ref[...]
    h2_ref[...] = h2
    hi = h2.astype(BF16)
    lo = (h2 - hi.astype(F32)).astype(BF16)
    logits = _dot(hi, whi_ref[...]) + _dot(lo, whi_ref[...]) + _dot(hi, wlo_ref[...]) + rb_ref[...]
    tm = logits.shape[0]
    work = jnp.transpose(logits)[:N_EXPERTS]
    expert = lax.broadcasted_iota(jnp.int32, work.shape, 0)
    vals, idxs = [], []
    for _ in range(TOP_K):
        mx = jnp.max(work, axis=0, keepdims=True)
        ix = jnp.min(jnp.where(work == mx, expert, N_EXPERTS), axis=0, keepdims=True)
        vals.append(mx)
        idxs.append(ix)
        work = jnp.where(expert == ix, -jnp.inf, work)
    es = [jnp.exp(v - vals[0]) for v in vals]
    tot = es[0]
    for e in es[1:]:
        tot = tot + e

    onehot = jnp.zeros(work.shape, F32)
    for j in range(TOP_K):
        onehot = jnp.where(expert == idxs[j], 1.0, onehot)
    earlier = (lax.broadcasted_iota(jnp.int32, (tm, tm), 0) < lax.broadcasted_iota(jnp.int32, (tm, tm), 1))
    within = _dot(onehot.astype(BF16), jnp.where(earlier, 1.0, 0.0).astype(BF16))
    carry = carry_sc[...]
    rank_full = within + (jnp.tile(carry, (1, tm // LANES)) if tm >= LANES else carry[:, :tm])
    ranks = [jnp.sum(jnp.where(expert == idxs[j], rank_full, 0.0), axis=0, keepdims=True) for j in range(TOP_K)]
    carry_sc[...] = carry_sc[...] + jnp.sum(onehot, axis=1, keepdims=True)
    cnt_ref[...] = carry_sc[...]

    row = lax.broadcasted_iota(jnp.int32, (ROW_GROUP, tm), 0)
    idx8 = jnp.zeros((ROW_GROUP, tm), jnp.int32)
    gate8 = jnp.zeros((ROW_GROUP, tm), F32)
    rank8 = jnp.zeros((ROW_GROUP, tm), F32)
    for j in range(TOP_K):
        idx8 = jnp.where(row == j, idxs[j], idx8)
        gate8 = jnp.where(row == j, es[j] / tot, gate8)
        rank8 = jnp.where(row == j, ranks[j], rank8)
    idx_ref[...] = idx8
    gate_ref[...] = gate8
    rank_ref[...] = rank8.astype(jnp.int32)


def _router(x1p, shift_p, scale_p, x1s, shift_s, scale_s, wts, *, tm, tiles_per_batch):
    tp, d = x1p.shape
    ts = x1s.shape[0]
    n_p, n_s = tp // tm, ts // tm
    t_all = tp + ts
    row = lambda w: pl.BlockSpec((tm, w), lambda i: (i, 0))
    p_row = pl.BlockSpec((tm, d), lambda i: (jnp.minimum(i, n_p - 1), 0))
    p_mod = pl.BlockSpec((None, 1, d), lambda i: (jnp.minimum(i, n_p - 1) // tiles_per_batch, 0, 0))
    s_row = pl.BlockSpec((tm, d), lambda i: (jnp.maximum(i - n_p, 0), 0))
    by_choice = pl.BlockSpec((ROW_GROUP, tm), lambda i: (0, i))
    consts = [wts["gffn"], wts["rw_hi"], wts["rw_lo"], wts["rb"]]
    return pl.pallas_call(
        functools.partial(_router_kernel, n_prompt_tiles=n_p),
        grid=(n_p + n_s,),
        in_specs=[p_row, p_mod, p_mod, s_row, s_row, s_row] + [_const_spec(c.shape) for c in consts],
        out_specs=[row(d), by_choice, by_choice, by_choice, _const_spec((N_EXPERTS, LANES))],
        out_shape=[jax.ShapeDtypeStruct((t_all, d), F32), jax.ShapeDtypeStruct((ROW_GROUP, t_all), jnp.int32),
                   jax.ShapeDtypeStruct((ROW_GROUP, t_all), F32), jax.ShapeDtypeStruct((ROW_GROUP, t_all), jnp.int32),
                   jax.ShapeDtypeStruct((N_EXPERTS, LANES), F32)],
        scratch_shapes=[pltpu.VMEM((N_EXPERTS, LANES), F32)],
        compiler_params=_params(("arbitrary",)),
        name="router",
    )(x1p, shift_p, scale_p, x1s, shift_s, scale_s, *consts)


ROW_GROUP = 8


def _slot_offset(slot, n_rows):
    return slot * n_rows if isinstance(slot, int) else pl.multiple_of(slot * n_rows, n_rows)


def _row_gather_start(idx_smem, slot, src_hbm, dst_vmem, sem, n_rows):
    base = _slot_offset(slot, n_rows)

    def group(g, carry):
        r0 = g * ROW_GROUP
        for j in range(ROW_GROUP):
            pltpu.make_async_copy(src_hbm.at[pl.ds(idx_smem[base + r0 + j], 1)], dst_vmem.at[g, pl.ds(j, 1)],
                                  sem).start(priority=j % 2)
        return carry
    lax.fori_loop(0, n_rows // ROW_GROUP, group, 0)


def _row_gather_wait(dst_vmem, sem):
    pltpu.make_async_copy(dst_vmem, dst_vmem, sem).wait()


def _gather_pipeline(i, n_steps, idx_hbm, idx_smem, isem, src_hbm, buf, gsem, n_rows, first_tile):
    def idx_copy(blk, slot):
        return pltpu.make_async_copy(idx_hbm.at[first_tile + blk],
                                     idx_smem.at[pl.ds(_slot_offset(slot, n_rows), n_rows)], isem.at[slot])

    @pl.when(i == 0)
    def _():
        idx_copy(0, 0).start()
        idx_copy(0, 0).wait()
        _row_gather_start(idx_smem, 0, src_hbm, buf.at[0], gsem.at[0], n_rows)

        @pl.when(n_steps > 1)
        def _():
            idx_copy(1, 1).start()

    nxt = (i + 1) % 2

    @pl.when(i + 1 < n_steps)
    def _():
        idx_copy(i + 1, nxt).wait()
        _row_gather_start(idx_smem, nxt, src_hbm, buf.at[nxt], gsem.at[nxt], n_rows)

    @pl.when(i + 2 < n_steps)
    def _():
        idx_copy(i + 2, i % 2).start()

    _row_gather_wait(buf.at[i % 2], gsem.at[i % 2])


def _dispatch_kernel(pend_ref, nused_ref, dest_ref, h2_ref, xs_ref, zbuf, idx_smem, isem, csem, zsem, *,
                     tm, n_blocks):
    i = pl.program_id(0)
    n = pl.num_programs(0)
    rows = TOP_K * tm

    def idx_copy(blk, slot):
        return pltpu.make_async_copy(dest_ref.at[blk], idx_smem.at[pl.ds(_slot_offset(slot, rows), rows)],
                                     isem.at[slot])

    def zero_copy(block_start):
        start = pl.multiple_of(block_start, MOE_ROWS)
        return pltpu.make_async_copy(zbuf, xs_ref.at[pl.ds(start, MOE_ROWS)], zsem)

    def rows_done():
        return pltpu.make_async_copy(xs_ref.at[pl.ds(0, rows)], xs_ref.at[pl.ds(0, rows)], csem)

    @pl.when(i == 0)
    def _():
        idx_copy(0, 0).start()
        zbuf[...] = jnp.zeros(zbuf.shape, F32)
        n_used = nused_ref[0]

        def last_block(e, carry):
            zero_copy(jnp.maximum(pend_ref[e] - MOE_ROWS, 0)).start()
            return carry
        lax.fori_loop(0, N_EXPERTS, last_block, 0)

        def tail_block(b, carry):
            zero_copy(b * MOE_ROWS).start()
            return carry
        lax.fori_loop(n_used, n_blocks, tail_block, 0)

        def drain(b, carry):
            zero_copy(0).wait()
            return carry
        lax.fori_loop(0, N_EXPERTS + n_blocks - n_used, drain, 0)

    slot = i % 2
    idx_copy(i, slot).wait()

    @pl.when(i + 1 < n)
    def _():
        idx_copy(i + 1, 1 - slot).start()

    base = _slot_offset(slot, rows)

    def group(g, carry):
        r0 = g * ROW_GROUP
        for j in range(ROW_GROUP):
            src = h2_ref.at[g, pl.ds(j, 1)]
            for kk in range(TOP_K):
                dst = xs_ref.at[pl.ds(idx_smem[base + kk * tm + r0 + j], 1)]
                pltpu.make_async_copy(src, dst, csem).start(priority=kk % 2)
        return carry
    lax.fori_loop(0, tm // ROW_GROUP, group, 0)
    rows_done().wait()


def _dispatch(pend, n_used, dest_tiles, h2_all, *, tm, n_blocks):
    t_all, d = h2_all.shape
    grid_spec = pltpu.PrefetchScalarGridSpec(
        num_scalar_prefetch=2,
        grid=(t_all // tm,),
        in_specs=[pl.BlockSpec(memory_space=pl.ANY),
                  pl.BlockSpec((tm // ROW_GROUP, ROW_GROUP, d), lambda i, pe, nu: (i, 0, 0))],
        out_specs=pl.BlockSpec(memory_space=pl.ANY),
        scratch_shapes=[pltpu.VMEM((MOE_ROWS, d), F32),
                        pltpu.SMEM((2 * TOP_K * tm,), jnp.int32),
                        pltpu.SemaphoreType.DMA((2,)),
                        pltpu.SemaphoreType.DMA(()),
                        pltpu.SemaphoreType.DMA(())],
    )
    return pl.pallas_call(
        functools.partial(_dispatch_kernel, tm=tm, n_blocks=n_blocks),
        grid_spec=grid_spec,
        out_shape=jax.ShapeDtypeStruct((n_blocks * MOE_ROWS, d), F32),
        compiler_params=_params(("arbitrary",)),
        name="dispatch",
    )(pend, n_used, dest_tiles, h2_all.reshape(t_all // ROW_GROUP, ROW_GROUP, d))


def _moe_kernel(be_ref, nused_ref, xs_ref, wgu_ref, bgu_ref, wdn_ref, bdn_ref, y_ref, wgu_bf, wdn_bf, *, d_model):
    i = pl.program_id(0)
    n_used = nused_ref[0]

    @pl.when(i < n_used)
    def _():
        prev = be_ref[jnp.maximum(i - 1, 0)]

        @pl.when((i == 0) | (be_ref[i] != prev))
        def _():
            wgu_bf[...] = wgu_ref[...].astype(BF16)
            wdn_bf[...] = wdn_ref[...].astype(BF16)

        xb = xs_ref[...].astype(BF16)
        gu = _dot(xb, wgu_bf[...]) + bgu_ref[...]
        g = jnp.minimum(gu[:, :d_model], SWIGLU_LIMIT)
        lin = jnp.clip(gu[:, d_model:], -SWIGLU_LIMIT, SWIGLU_LIMIT)
        act = g * jax.nn.sigmoid(SWIGLU_ALPHA * g) * (lin + 1.0)
        y_ref[...] = _dot(act.astype(BF16), wdn_bf[...]) + bdn_ref[...]

    @pl.when(i >= n_used)
    def _():
        y_ref[...] = jnp.zeros(y_ref.shape, F32)


def _moe_experts(block_e, n_used, xs, w_gu, b_gu, w_dn, b_dn):
    n_blocks = xs.shape[0] // MOE_ROWS
    e, d, d2 = w_gu.shape
    grid_spec = pltpu.PrefetchScalarGridSpec(
        num_scalar_prefetch=2,
        grid=(n_blocks,),
        in_specs=[pl.BlockSpec((MOE_ROWS, d), lambda i, be, nu: (jnp.minimum(i, nu[0] - 1), 0)),
                  pl.BlockSpec((None, d, d2), lambda i, be, nu: (be[i], 0, 0)),
                  pl.BlockSpec((None, 1, d2), lambda i, be, nu: (be[i], 0, 0)),
                  pl.BlockSpec((None, d, d), lambda i, be, nu: (be[i], 0, 0)),
                  pl.BlockSpec((None, 1, d), lambda i, be, nu: (be[i], 0, 0))],
        out_specs=pl.BlockSpec((MOE_ROWS, d), lambda i, be, nu: (i, 0)),
        scratch_shapes=[pltpu.VMEM((d, d2), BF16),
                        pltpu.VMEM((d, d), BF16)],
    )
    return pl.pallas_call(
        functools.partial(_moe_kernel, d_model=d),
        grid_spec=grid_spec,
        out_shape=jax.ShapeDtypeStruct((n_blocks * MOE_ROWS, d), F32),
        compiler_params=_params(("arbitrary",)),
        name="moe_experts",
    )(block_e, n_used, xs, w_gu, b_gu.reshape(e, 1, d2), w_dn, b_dn.reshape(e, 1, d))


def _combine_kernel(pos_ref, ys_ref, x1_ref, gate_ref, gm_ref, gfin_ref, y_ref, ybuf, idx_smem, isem, gsem, *,
                    tm, first_tile, final_norm):
    i = pl.program_id(0)
    _gather_pipeline(i, pl.num_programs(0), pos_ref, idx_smem, isem, ys_ref, ybuf, gsem, TOP_K * tm, first_tile)
    gate = jnp.transpose(jnp.concatenate([gate_ref[...], jnp.zeros((LANES - ROW_GROUP, tm), F32)], axis=0))
    f = jnp.zeros(x1_ref.shape, F32)
    groups = tm // ROW_GROUP
    for kk in range(TOP_K):
        rows = ybuf[i % 2, kk * groups:(kk + 1) * groups].reshape(x1_ref.shape)
        f = f + gate[:, kk:kk + 1] * rows
    x2 = x1_ref[...] + gm_ref[...] * f
    y_ref[...] = _rms(x2, gfin_ref[...]) if final_norm else x2


def _combine(pos_tiles, ys, x1, gate, g_m, final_g, *, tm, first_tile, per_row, tiles_per_batch, final_norm):
    t, d = x1.shape
    row = lambda w: pl.BlockSpec((tm, w), lambda i: (i, 0))
    return pl.pallas_call(
        functools.partial(_combine_kernel, tm=tm, first_tile=first_tile, final_norm=final_norm),
        grid=(t // tm,),
        in_specs=[pl.BlockSpec(memory_space=pl.ANY), pl.BlockSpec(memory_space=pl.ANY), row(d),
                  pl.BlockSpec((ROW_GROUP, tm), lambda i: (0, first_tile + i)),
                  _mod_spec(per_row, tm, d, tiles_per_batch), _const_spec((1, d))],
        out_specs=row(d),
        out_shape=jax.ShapeDtypeStruct((t, d), F32),
        scratch_shapes=[pltpu.VMEM((2, TOP_K * tm // ROW_GROUP, ROW_GROUP, d), F32),
                        pltpu.SMEM((2 * TOP_K * tm,), jnp.int32),
                        pltpu.SemaphoreType.DMA((2,)),
                        pltpu.SemaphoreType.DMA((2,))],
        compiler_params=_params(("arbitrary",)),
        name="combine",
    )(pos_tiles, ys, x1, gate, g_m, final_g)


def _rope_tables(pos):
    inv = ROPE_THETA ** (-jnp.arange(0, QK_ROPE, 2, dtype=F32) / QK_ROPE)
    ang = pos.astype(F32)[:, None] * inv[None, :]
    cos, sin = jnp.cos(ang), jnp.sin(ang)
    n = pos.shape[0]
    cc = jnp.concatenate([cos, cos, jnp.ones((n, LANES - QK_ROPE), F32)], axis=1)
    ss = jnp.concatenate([sin, sin, jnp.zeros((n, LANES - QK_ROPE), F32)], axis=1)
    return cc, ss


def _swap_halves(w):
    half = QK_ROPE // 2
    return jnp.concatenate([-w[..., half:], w[..., :half]], axis=-1)


def _layer_weights(l, w_in, norm_mix_g, q_norm_g, w_uq, kv_norm_g, w_uk, w_uv, w_pa, sg_norm_g, sg_norm_b,
                   w_pb, w_o, norm_ffn_g, router_w, router_b):
    d = w_in.shape[1]
    wi = w_in[l]
    o_kr = Q_RANK + KV_RANK
    o_u = o_kr + QK_ROPE
    o_v = o_u + SG_WIDTH
    o_ga = o_v + SG_WIDTH
    kr = wi[:, o_kr:o_u]
    zpad = jnp.zeros((d, LANES - QK_ROPE), F32)
    w_in_r = jnp.concatenate([wi[:, :o_kr], wi[:, o_u:o_ga], wi[:, o_ga:],
                              kr, zpad, _swap_halves(kr), zpad], axis=1).astype(BF16)
    uq = w_uq[l]
    nope, rope = uq[..., :QK_NOPE], uq[..., QK_NOPE:]
    z32 = jnp.zeros(rope.shape[:2] + (HEAD_SLOT - QK_NOPE - QK_ROPE,), F32)
    wq = jnp.concatenate([rope, nope, z32], axis=-1).reshape(Q_RANK, QK_WIDTH).astype(BF16)
    wqs = jnp.concatenate([_swap_halves(rope), jnp.zeros_like(nope), z32], axis=-1)
    wqs = wqs.reshape(Q_RANK, QK_WIDTH).astype(BF16)
    uk = w_uk[l]
    zk_lo = jnp.zeros(uk.shape[:2] + (QK_ROPE,), F32)
    zk_hi = jnp.zeros(uk.shape[:2] + (HEAD_SLOT - QK_NOPE - QK_ROPE,), F32)
    wk = jnp.concatenate([zk_lo, uk, zk_hi], axis=-1).reshape(KV_RANK, QK_WIDTH).astype(BF16)
    wv = w_uv[l].reshape(KV_RANK, V_WIDTH).astype(BF16)
    wv_slot = jnp.concatenate([w_uv[l], jnp.zeros_like(w_uv[l])], axis=-1).reshape(KV_RANK, QK_WIDTH).astype(BF16)
    vone = jnp.tile(jnp.concatenate([jnp.zeros((V_HEAD,), F32), jnp.ones((HEAD_SLOT - V_HEAD,), F32)]),
                    MLA_HEADS).reshape(1, QK_WIDTH)
    ukt = jnp.transpose(uk, (1, 2, 0))
    eye = jnp.broadcast_to(jnp.eye(QK_ROPE, LANES, dtype=F32), (MLA_HEADS, QK_ROPE, LANES))
    top = jnp.concatenate([jnp.zeros((MLA_HEADS, QK_ROPE, KV_RANK), F32), eye], axis=-1)
    mid = jnp.concatenate([ukt, jnp.zeros((MLA_HEADS, QK_NOPE, LANES), F32)], axis=-1)
    bot = jnp.zeros((MLA_HEADS, HEAD_SLOT - QK_NOPE - QK_ROPE, KV_RANK + LANES), F32)
    mabs = jnp.concatenate([top, mid, bot], axis=1).astype(BF16)
    rw = jnp.pad(router_w[l], ((0, 0), (0, ROUTER_PAD - N_EXPERTS)))
    rw_hi = rw.astype(BF16)
    rw_lo = (rw - rw_hi.astype(F32)).astype(BF16)
    rb = jnp.concatenate([router_b[l], jnp.full((ROUTER_PAD - N_EXPERTS,), NEG_BIG, F32)]).reshape(1, ROUTER_PAD)
    return dict(
        w_in_r=w_in_r, gmix=norm_mix_g[l].reshape(1, d),
        gq=(q_norm_g[l] * (ATTN_SCALE * LOG2_E)).reshape(1, Q_RANK),
        gkv=kv_norm_g[l].reshape(1, KV_RANK), wq=wq, wqs=wqs, wk=wk, wv=wv, wv_slot=wv_slot, vone=vone, mabs=mabs,
        sgg=sg_norm_g[l].reshape(1, SG_WIDTH), sgb=sg_norm_b[l].reshape(1, SG_WIDTH),
        w_pa=w_pa[l].astype(BF16), w_pb=w_pb[l].astype(BF16), w_o=w_o[l].astype(BF16),
        gffn=norm_ffn_g[l].reshape(1, d), rw_hi=rw_hi, rw_lo=rw_lo, rb=rb)


def _spatial_mix_weights(w_s, b_s, seq, n_batch):
    gw = SG_WIDTH // SG_GROUPS
    tril = jnp.tril(jnp.ones((SG_CHUNK, SG_CHUNK), dtype=bool))
    w = jnp.where(tril[None], w_s, 0.0)
    if seq % SG_CHUNK == 0:
        mixw = w
        bias_t = b_s
    else:
        assert seq < SG_CHUNK
        blk = w[:, :seq, :seq]
        eye = jnp.eye(n_batch, dtype=F32)
        mixw = jnp.einsum("ab,gts->gatbs", eye, blk).reshape(SG_GROUPS, n_batch * seq, n_batch * seq)
        bias_t = jnp.tile(b_s[:, :seq], (1, n_batch))
    bias = jnp.repeat(jnp.transpose(bias_t), gw, axis=1)
    return mixw.astype(BF16), bias


def _routing_tables(idx, rank, counts_f, n_blocks):
    idx, rank = idx[:TOP_K], rank[:TOP_K]
    counts = counts_f[:, 0].astype(jnp.int32)
    padded = (counts + MOE_ROWS - 1) // MOE_ROWS * MOE_ROWS
    pend = jnp.cumsum(padded).astype(jnp.int32)
    pstart = pend - padded
    experts = jnp.arange(N_EXPERTS, dtype=jnp.int32)
    dest = rank + jnp.sum(jnp.where(idx[..., None] == experts, pstart, 0), axis=-1)
    block_start = jnp.arange(n_blocks, dtype=jnp.int32) * MOE_ROWS
    block_e = jnp.minimum(jnp.sum((pend[None, :] <= block_start[:, None]).astype(jnp.int32), axis=1), N_EXPERTS - 1)
    n_used = (pend[-1:] // MOE_ROWS).astype(jnp.int32)
    return dest.astype(jnp.int32), pend, block_e.astype(jnp.int32), n_used


def _pos_tiles(pos, tm):
    t = pos.shape[1]
    return jnp.transpose(pos.reshape(TOP_K, t // tm, tm), (1, 0, 2)).reshape(t // tm, TOP_K * tm)


def _stack_layers(per_layer):
    return per_layer[0][None] if len(per_layer) == 1 else jnp.stack(per_layer)


def _pick_tile(n, pref):
    t = min(n, pref)
    assert n % t == 0 and t % 8 == 0
    return t


def kernel(x_prompt, x_sample, cache_ckv, cache_krope, c_prompt, c_sample, ada_w, ada_b, norm_mix_g, w_in, q_norm_g, w_uq, kv_norm_g, w_uk, w_uv, w_pa, sg_norm_g, sg_norm_b, w_spatial, b_spatial, w_pb, w_o, norm_ffn_g, router_w, router_b, w_gu, b_gu, w_dn, b_dn, final_g):
    bp, lp, d = x_prompt.shape
    bs, ls, _ = x_sample.shape
    depth = w_in.shape[0]
    past = cache_ckv.shape[2]
    tp, ts = bp * lp, bs * ls
    assert lp % SG_CHUNK == 0 and ls <= SG_CHUNK

    tm_p = _pick_tile(lp, INPROJ_TILE)
    tm_s = _pick_tile(ts, SAMPLE_TILE)
    t_attn = _pick_tile(lp, ATTN_TILE)
    tpb = lp // tm_p

    cc_p, ss_p = _rope_tables(jnp.arange(lp, dtype=jnp.int32))
    cc_s, ss_s = _rope_tables(past + jnp.arange(ls, dtype=jnp.int32))
    cc_s, ss_s = jnp.tile(cc_s, (bs, 1)), jnp.tile(ss_s, (bs, 1))

    b_all = bp + bs
    b_pad = -(-b_all // 8) * 8
    c_all = jnp.concatenate([c_prompt, c_sample, jnp.zeros((b_pad - b_all, d), F32)], axis=0)

    xp = x_prompt.reshape(tp, d)
    xs = x_sample.reshape(ts, d)
    outs = dict(ckv_p=[], kr_p=[], ckv_s=[], kr_s=[], v_s=[])
    final_g2 = final_g.reshape(1, d)
    for l in range(depth):
        wts = _layer_weights(l, w_in, norm_mix_g, q_norm_g, w_uq, kv_norm_g, w_uk, w_uv, w_pa, sg_norm_g,
                             sg_norm_b, w_pb, w_o, norm_ffn_g, router_w, router_b)
        mod = _adaln(c_all, ada_w[l], ada_b[l])
        mod_p = [mod[:bp, j * d:(j + 1) * d].reshape(bp, 1, d) for j in range(6)]
        mod_s = [jnp.repeat(mod[bp:b_all, j * d:(j + 1) * d], ls, axis=0) for j in range(6)]

        q, k, v, ckv, kr, u, vn, ga, gb = _inproj(xp, mod_p[0], mod_p[1], cc_p, ss_p, wts, tm=tm_p, per_row=False,
                                                  tiles_per_batch=tpb, vn_dtype=BF16)
        o = _attn_prompt(q.reshape(bp, lp, QK_WIDTH), k.reshape(bp, lp, QK_WIDTH), v.reshape(bp, lp, QK_WIDTH),
                         tq=t_attn, tk=t_attn).reshape(tp, V_WIDTH)
        mixw, bias = _spatial_mix_weights(w_spatial[l], b_spatial[l], lp, bp)
        tm_m = _pick_tile(lp, MERGE_TILE)
        x1p = _merge(o, u, vn, ga, gb, xp, mod_p[2], mixw, bias, wts, tm=tm_m, chunk=SG_CHUNK, per_row=False,
                     tiles_per_batch=lp // tm_m)
        outs["ckv_p"].append(ckv.reshape(bp, lp, KV_RANK))
        outs["kr_p"].append(kr.reshape(bp, lp, QK_ROPE))

        q, k, v, ckv, kr, u, vn, ga, gb = _inproj(xs, mod_s[0], mod_s[1], cc_s, ss_s, wts, tm=tm_s, per_row=True,
                                                  tiles_per_batch=1, vn_dtype=F32)
        ckv3, kr3 = ckv.reshape(bs, ls, KV_RANK), kr.reshape(bs, ls, QK_ROPE)
        o = _attn_sample(q.reshape(bs, ls, QK_WIDTH), cache_ckv[l], cache_krope[l], ckv3, kr3,
                         wts["mabs"], wts["wv"]).reshape(ts, V_WIDTH)
        mixw, bias = _spatial_mix_weights(w_spatial[l], b_spatial[l], ls, tm_s // ls)
        x1s = _merge(o, u, vn, ga, gb, xs, mod_s[2], mixw, bias, wts, tm=tm_s, chunk=tm_s, per_row=True,
                     tiles_per_batch=1)
        outs["ckv_s"].append(ckv3)
        outs["kr_s"].append(kr3)
        outs["v_s"].append(vn.reshape(bs, ls, SG_WIDTH))

        t_all = tp + ts
        tm_r = _pick_tile(math.gcd(tp, ts), ROW_TILE)
        h2_all, idx, gate, rank, counts = _router(x1p, mod_p[3], mod_p[4], x1s, mod_s[3], mod_s[4], wts, tm=tm_r,
                                                  tiles_per_batch=lp // tm_r)
        n_blocks = -(-(t_all * TOP_K) // MOE_ROWS) + N_EXPERTS
        pos, pend, block_e, n_used = _routing_tables(idx, rank, counts, n_blocks)
        pos_tiles = _pos_tiles(pos, tm_r)
        x_sorted = _dispatch(pend, n_used, pos_tiles, h2_all, tm=tm_r, n_blocks=n_blocks)
        ys = _moe_experts(block_e, n_used, x_sorted, w_gu[l], b_gu[l], w_dn[l], b_dn[l])
        last = l == depth - 1
        xp = _combine(pos_tiles, ys, x1p, gate, mod_p[5], final_g2, tm=tm_r, first_tile=0, per_row=False,
                      tiles_per_batch=lp // tm_r, final_norm=last)
        xs = _combine(pos_tiles, ys, x1s, gate, mod_s[5], final_g2, tm=tm_r, first_tile=tp // tm_r, per_row=True,
                      tiles_per_batch=1, final_norm=last)
    return (xp.reshape(bp, lp, d), xs.reshape(bs, ls, d),
            _stack_layers(outs["ckv_p"]), _stack_layers(outs["kr_p"]),
            _stack_layers(outs["ckv_s"]), _stack_layers(outs["kr_s"]), _stack_layers(outs["v_s"]))
```

```python
import functools
import math

import jax
import jax.numpy as jnp
from jax import lax
from jax.experimental import pallas as pl
from jax.experimental.pallas import tpu as pltpu

F32 = jnp.float32
BF16 = jnp.bfloat16

LANES = 128
VMEM_LIMIT_BYTES = 56 * 1024 * 1024

CHUNK = 64
CHUNK_SHIFT = 6
MLA_HEADS = 8
QK_NOPE = 64
QK_ROPE = 32
V_HEAD = 64
V_HEAD_SHIFT = 6
Q_RANK = 384
KV_RANK = 256
ROPE_THETA = 10000.0
ATTN_SCALE = 1.0 / math.sqrt(QK_NOPE + QK_ROPE)
LOG2_E = math.log2(math.e)
SG_CHUNK = 128
SG_GROUPS = 4
SG_WIDTH = 512
N_EXPERTS = 32
TOP_K = 4
SWIGLU_LIMIT = 7.0
SWIGLU_ALPHA = 1.702
EPS = 1e-6

HEAD_SLOT = LANES
QK_WIDTH = MLA_HEADS * HEAD_SLOT
V_WIDTH = MLA_HEADS * V_HEAD
MOE_ROWS = 512
ROW_TILE = 512
INPROJ_TILE = 512
MERGE_TILE = 512
SAMPLE_TILE = 512
ATTN_TILE = 1024
ATTN_SUB_KEYS = 256
ROUTER_PAD = LANES
NEG_BIG = -1e30

_C_CQ = 0
_C_CKV = _C_CQ + Q_RANK
_C_U = _C_CKV + KV_RANK
_C_V = _C_U + SG_WIDTH
_C_GA = _C_V + SG_WIDTH


def _params(sem):
    return pltpu.CompilerParams(dimension_semantics=sem, vmem_limit_bytes=VMEM_LIMIT_BYTES)


def _dot(a, b):
    return jnp.dot(a, b, preferred_element_type=F32)


def _dot_nt(a, b):
    return lax.dot_general(a, b, (((1,), (1,)), ((), ())), preferred_element_type=F32)


def _rms(x, g):
    return x * lax.rsqrt(jnp.mean(x * x, axis=-1, keepdims=True) + EPS) * g


def _adaln_kernel(c_ref, w_ref, b_ref, o_ref):
    c = c_ref[...]
    s = (c * jax.nn.sigmoid(c)).astype(BF16)
    o_ref[...] = _dot(s, w_ref[...].astype(BF16)) + b_ref[...]


def _adaln(c_all, ada_w, ada_b):
    bp, d = c_all.shape
    n = ada_w.shape[1]
    return pl.pallas_call(
        _adaln_kernel,
        grid=(n // d,),
        in_specs=[pl.BlockSpec((bp, d), lambda j: (0, 0)),
                  pl.BlockSpec((d, d), lambda j: (0, j)),
                  pl.BlockSpec((1, d), lambda j: (0, j))],
        out_specs=pl.BlockSpec((bp, d), lambda j: (0, j)),
        out_shape=jax.ShapeDtypeStruct((bp, n), F32),
        compiler_params=_params(("arbitrary",)),
        name="adaln",
    )(c_all, ada_w, ada_b.reshape(1, n))


def _inproj_kernel(x_ref, sh_ref, sc_ref, gmix_ref, cc_ref, ss_ref, win_ref, gq_ref, gkv_ref,
                   wq_ref, wqs_ref, wk_ref, wv_ref, vone_ref, sgg_ref, sgb_ref,
                   q_ref, k_ref, v_ref, ckv_ref, kr_ref, u_ref, vn_ref, ga_ref, gb_ref, *, d_model):
    x = x_ref[...]
    h = (_rms(x, gmix_ref[...]) * (1.0 + sc_ref[...]) + sh_ref[...]).astype(BF16)

    def proj(lo, width):
        return _dot(h, win_ref[:, lo:lo + width])

    cc = cc_ref[...]
    ss = ss_ref[...]
    c_gb = _C_GA + d_model
    c_kra = c_gb + d_model
    c_krb = c_kra + LANES

    cqn = _rms(proj(_C_CQ, Q_RANK), gq_ref[...]).astype(BF16)
    qa = _dot(cqn, wq_ref[...])
    qb = _dot(cqn, wqs_ref[...])
    for hd in range(MLA_HEADS):
        sl = slice(hd * HEAD_SLOT, (hd + 1) * HEAD_SLOT)
        q_ref[:, sl] = (qa[:, sl] * cc + qb[:, sl] * ss).astype(BF16)

    ckvn = _rms(proj(_C_CKV, KV_RANK), gkv_ref[...])
    ckv_ref[...] = ckvn
    ckvb = ckvn.astype(BF16)
    krs = proj(c_kra, LANES) * cc + proj(c_krb, LANES) * ss
    kr_ref[...] = jnp.transpose(krs)[:QK_ROPE]
    kn = _dot(ckvb, wk_ref[...])
    for hd in range(MLA_HEADS):
        sl = slice(hd * HEAD_SLOT, (hd + 1) * HEAD_SLOT)
        k_ref[:, sl] = (kn[:, sl] + krs).astype(BF16)
    v_ref[...] = (_dot(ckvb, wv_ref[...]) + vone_ref[...]).astype(BF16)

    u_ref[...] = proj(_C_U, SG_WIDTH).astype(u_ref.dtype)
    vv = proj(_C_V, SG_WIDTH)
    mu = jnp.mean(vv, axis=-1, keepdims=True)
    vc = vv - mu
    var = jnp.mean(vc * vc, axis=-1, keepdims=True)
    vn_ref[...] = (vc * lax.rsqrt(var + EPS) * sgg_ref[...] + sgb_ref[...]).astype(vn_ref.dtype)
    ga_ref[...] = proj(_C_GA, d_model).astype(BF16)
    gb_ref[...] = proj(c_gb, d_model).astype(BF16)


def _mod_spec(per_row, tm, d, tiles_per_batch):
    if per_row:
        return pl.BlockSpec((tm, d), lambda i: (i, 0))
    return pl.BlockSpec((None, 1, d), lambda i: (i // tiles_per_batch, 0, 0))


def _const_spec(shape):
    nd = len(shape)
    return pl.BlockSpec(shape, lambda i: (0,) * nd)


def _inproj(x2d, shift, scale, cc, ss, wts, *, tm, per_row, tiles_per_batch, vn_dtype):
    t, d = x2d.shape
    n_tab = cc.shape[0] // tm
    row = lambda w: pl.BlockSpec((tm, w), lambda i: (i, 0))
    tab = pl.BlockSpec((tm, LANES), lambda i: (i % n_tab, 0))
    mod = _mod_spec(per_row, tm, d, tiles_per_batch)
    consts = [wts["w_in_r"], wts["gq"], wts["gkv"], wts["wq"], wts["wqs"], wts["wk"], wts["wv_slot"],
              wts["vone"], wts["sgg"], wts["sgb"]]
    if per_row:
        kr_shape = jax.ShapeDtypeStruct((QK_ROPE, t), F32)
        kr_spec = pl.BlockSpec((QK_ROPE, tm), lambda i: (0, i))
    else:
        kr_shape = jax.ShapeDtypeStruct((t // (tiles_per_batch * tm), QK_ROPE, tiles_per_batch * tm), F32)
        kr_spec = pl.BlockSpec((None, QK_ROPE, tm), lambda i: (i // tiles_per_batch, 0, i % tiles_per_batch))
    out_shapes = [jax.ShapeDtypeStruct((t, QK_WIDTH), BF16), jax.ShapeDtypeStruct((t, QK_WIDTH), BF16),
                  jax.ShapeDtypeStruct((t, QK_WIDTH), BF16), jax.ShapeDtypeStruct((t, KV_RANK), F32),
                  kr_shape, jax.ShapeDtypeStruct((t, SG_WIDTH), BF16),
                  jax.ShapeDtypeStruct((t, SG_WIDTH), vn_dtype), jax.ShapeDtypeStruct((t, d), BF16),
                  jax.ShapeDtypeStruct((t, d), BF16)]
    out_specs = [row(s.shape[1]) for s in out_shapes]
    out_specs[4] = kr_spec
    return pl.pallas_call(
        functools.partial(_inproj_kernel, d_model=d),
        grid=(t // tm,),
        in_specs=[row(d), mod, mod, _const_spec((1, d)), tab, tab] + [_const_spec(c.shape) for c in consts],
        out_specs=out_specs,
        out_shape=out_shapes,
        compiler_params=_params(("arbitrary",)),
        name="inproj",
    )(x2d, shift, scale, wts["gmix"], cc, ss, *consts)


def _attn_kernel(qi_ref, kj_ref, flag_ref, q_ref, k_ref, v_ref, o_ref, m_sc, acc_sc, *, tq, tk, sub):
    s_id = pl.program_id(1)
    qi = qi_ref[s_id]
    kj = kj_ref[s_id]
    flags = flag_ref[s_id]

    @pl.when(kj == 0)
    def _():
        m_sc[...] = jnp.full(m_sc.shape, -jnp.inf, F32)
        acc_sc[...] = jnp.zeros(acc_sc.shape, F32)

    def sweep(bias):
        for kb in range(tk // sub):
            keys = slice(kb * sub, (kb + 1) * sub)
            rows = slice(kb * sub if bias is not None else 0, tq)
            for hd in range(MLA_HEADS):
                sl = slice(hd * HEAD_SLOT, (hd + 1) * HEAD_SLOT)
                s = _dot_nt(q_ref[rows, sl], k_ref[keys, sl])
                if bias is not None:
                    s = s + bias[rows, keys]
                tiles = [s[:, c * LANES:(c + 1) * LANES] for c in range(sub // LANES)]
                m_tile = tiles[0]
                for t in tiles[1:]:
                    m_tile = jnp.maximum(m_tile, t)
                m_old = m_sc[hd, rows]
                m_new = jnp.maximum(m_old, jnp.max(m_tile, axis=-1, keepdims=True))
                alpha = jnp.exp2(m_old - m_new)
                p = jnp.concatenate([jnp.exp2(t - m_new).astype(BF16) for t in tiles], axis=1)
                acc_sc[hd, rows] = alpha * acc_sc[hd, rows] + _dot(p, v_ref[keys, sl])
                m_sc[hd, rows] = m_new

    @pl.when((flags & 2) == 0)
    def _():
        sweep(None)

    @pl.when((flags & 2) != 0)
    def _():
        row = lax.broadcasted_iota(jnp.int32, (tq, tk), 0) + qi * tq
        col = lax.broadcasted_iota(jnp.int32, (tq, tk), 1) + kj * tk
        sweep(jnp.where((col >> CHUNK_SHIFT) <= (row >> CHUNK_SHIFT), 0.0, -jnp.inf))

    @pl.when((flags & 1) != 0)
    def _():
        lane = lax.broadcasted_iota(jnp.int32, (tq, LANES), 1)
        for pr in range(MLA_HEADS // 2):
            outs = []
            for hd in (2 * pr, 2 * pr + 1):
                acc = acc_sc[hd]
                outs.append(acc / pltpu.roll(acc, V_HEAD, axis=1))
            pair = jnp.where(lane < V_HEAD, outs[0], pltpu.roll(outs[1], V_HEAD, axis=1))
            o_ref[:, pr * LANES:(pr + 1) * LANES] = pair.astype(BF16)


def _attn_prompt(q, k, v, *, tq, tk):
    b, l, _ = q.shape
    assert tq == tk
    nq = l // tq
    qi_l, kj_l, flag_l = [], [], []
    for i in range(nq):
        n_kv = ((i + 1) * tq - 1) // tk + 1
        for j in range(n_kv):
            qi_l.append(i)
            kj_l.append(j)
            masked = ((j + 1) * tk - 1) // CHUNK > (i * tq) // CHUNK
            flag_l.append((1 if j == n_kv - 1 else 0) | (2 if masked else 0))
    steps = len(qi_l)
    grid_spec = pltpu.PrefetchScalarGridSpec(
        num_scalar_prefetch=3,
        grid=(b, steps),
        in_specs=[pl.BlockSpec((None, tq, QK_WIDTH), lambda bi, s, qi, kj, fl: (bi, qi[s], 0)),
                  pl.BlockSpec((None, tk, QK_WIDTH), lambda bi, s, qi, kj, fl: (bi, kj[s], 0)),
                  pl.BlockSpec((None, tk, QK_WIDTH), lambda bi, s, qi, kj, fl: (bi, kj[s], 0))],
        out_specs=pl.BlockSpec((None, tq, V_WIDTH), lambda bi, s, qi, kj, fl: (bi, qi[s], 0)),
        scratch_shapes=[pltpu.VMEM((MLA_HEADS, tq, LANES), F32), pltpu.VMEM((MLA_HEADS, tq, LANES), F32)],
    )
    return pl.pallas_call(
        functools.partial(_attn_kernel, tq=tq, tk=tk, sub=min(tk, ATTN_SUB_KEYS)),
        grid_spec=grid_spec,
        out_shape=jax.ShapeDtypeStruct((b, l, V_WIDTH), BF16),
        compiler_params=_params(("arbitrary", "arbitrary")),
        name="attn_prompt",
    )(jnp.asarray(qi_l, jnp.int32), jnp.asarray(kj_l, jnp.int32), jnp.asarray(flag_l, jnp.int32), q, k, v)


def _attn_sample_kernel(q_ref, pckv_ref, pkr_ref, nckv_ref, nkr_ref, mabs_ref, wv_ref, o_ref, *, ls, past):
    hl = MLA_HEADS * ls
    qcat = jnp.concatenate(
        [_dot(q_ref[:, hd * HEAD_SLOT:(hd + 1) * HEAD_SLOT], mabs_ref[hd]) for hd in range(MLA_HEADS)],
        axis=0).astype(BF16)
    q_abs = qcat[:, :KV_RANK]
    q_rope = qcat[:, KV_RANK:]
    pckv = pckv_ref[...].astype(BF16)
    nckv = nckv_ref[...].astype(BF16)

    def pad_rows(kr_t):
        return jnp.concatenate([kr_t, jnp.zeros((LANES - QK_ROPE, kr_t.shape[1]), kr_t.dtype)], axis=0).astype(BF16)

    s_past = _dot_nt(q_abs, pckv) + _dot(q_rope, pad_rows(pkr_ref[...]))
    s_new = _dot_nt(q_abs, nckv) + _dot(q_rope, pad_rows(nkr_ref[...]))

    qpos_1 = lax.broadcasted_iota(jnp.int32, (ls, 1), 0) + past
    qchunk = jnp.concatenate([qpos_1] * MLA_HEADS, axis=0) >> CHUNK_SHIFT
    kchunk_past = lax.broadcasted_iota(jnp.int32, (hl, past), 1) >> CHUNK_SHIFT
    kchunk_new = (lax.broadcasted_iota(jnp.int32, (hl, ls), 1) + past) >> CHUNK_SHIFT
    s_past = jnp.where(kchunk_past <= qchunk, s_past, -jnp.inf)
    s_new = jnp.where(kchunk_new <= qchunk, s_new, -jnp.inf)

    m = jnp.maximum(jnp.max(s_past, axis=-1, keepdims=True), jnp.max(s_new, axis=-1, keepdims=True))
    p_past = jnp.exp2(s_past - m)
    p_new = jnp.exp2(s_new - m)
    denom = jnp.sum(p_past, axis=-1, keepdims=True) + jnp.sum(p_new, axis=-1, keepdims=True)
    olat = (_dot(p_past.astype(BF16), pckv) + _dot(p_new.astype(BF16), nckv)) / denom
    ofull = _dot(olat.astype(BF16), wv_ref[...])
    col_head = lax.broadcasted_iota(jnp.int32, (ls, V_WIDTH), 1) >> V_HEAD_SHIFT
    out = jnp.zeros((ls, V_WIDTH), F32)
    for hd in range(MLA_HEADS):
        out = out + jnp.where(col_head == hd, ofull[hd * ls:(hd + 1) * ls], 0.0)
    o_ref[...] = out.astype(BF16)


def _attn_sample(q, past_ckv, past_kr_t, new_ckv, new_kr_t, mabs, wv):
    b, ls, _ = q.shape
    past = past_ckv.shape[1]
    blk = lambda n, w: pl.BlockSpec((None, n, w), lambda i: (i, 0, 0))
    return pl.pallas_call(
        functools.partial(_attn_sample_kernel, ls=ls, past=past),
        grid=(b,),
        in_specs=[blk(ls, QK_WIDTH), blk(past, KV_RANK), blk(QK_ROPE, past), blk(ls, KV_RANK), blk(QK_ROPE, ls),
                  _const_spec(mabs.shape), _const_spec(wv.shape)],
        out_specs=blk(ls, V_WIDTH),
        out_shape=jax.ShapeDtypeStruct((b, ls, V_WIDTH), BF16),
        compiler_params=_params(("arbitrary",)),
        name="attn_sample",
    )(q, past_ckv, past_kr_t, new_ckv, new_kr_t, mabs, wv)


def _merge_kernel(o_ref, u_ref, vn_ref, ga_ref, gb_ref, x_ref, gate_ref, mix_ref, bias_ref,
                  wpa_ref, wpb_ref, wo_ref, x1_ref, sg_sc, *, chunk):
    tm = x_ref.shape[0]
    gw = SG_WIDTH // SG_GROUPS
    for c in range(tm // chunk):
        rows = slice(c * chunk, (c + 1) * chunk)
        for g in range(SG_GROUPS):
            cols = slice(g * gw, (g + 1) * gw)
            mixed = _dot(mix_ref[g], vn_ref[rows, cols].astype(BF16)) + bias_ref[:, cols]
            sg_sc[rows, cols] = (u_ref[rows, cols].astype(F32) * mixed).astype(BF16)
    ya = _dot(o_ref[...], wpa_ref[...])
    yb = _dot(sg_sc[...], wpb_ref[...])
    m = jax.nn.sigmoid(ga_ref[...].astype(F32)) * ya + jax.nn.sigmoid(gb_ref[...].astype(F32)) * yb
    x1_ref[...] = x_ref[...] + gate_ref[...] * _dot(m.astype(BF16), wo_ref[...])


def _merge(o, u, vn, ga, gb, x2d, gate, mixw, bias, wts, *, tm, chunk, per_row, tiles_per_batch):
    t, d = x2d.shape
    row = lambda w: pl.BlockSpec((tm, w), lambda i: (i, 0))
    consts = [mixw, bias, wts["w_pa"], wts["w_pb"], wts["w_o"]]
    return pl.pallas_call(
        functools.partial(_merge_kernel, chunk=chunk),
        grid=(t // tm,),
        in_specs=[row(V_WIDTH), row(SG_WIDTH), row(SG_WIDTH), row(d), row(d), row(d),
                  _mod_spec(per_row, tm, d, tiles_per_batch)] + [_const_spec(c.shape) for c in consts],
        out_specs=row(d),
        out_shape=jax.ShapeDtypeStruct((t, d), F32),
        scratch_shapes=[pltpu.VMEM((tm, SG_WIDTH), BF16)],
        compiler_params=_params(("arbitrary",)),
        name="merge",
    )(o, u, vn, ga, gb, x2d, gate, *consts)


def _router_kernel(xp_ref, shp_ref, scp_ref, xs_ref, shs_ref, scs_ref, g_ref, whi_ref, wlo_ref, rb_ref,
                   h2_ref, idx_ref, gate_ref, rank_ref, cnt_ref, carry_sc, *, n_prompt_tiles):
    i = pl.program_id(0)
    out_refs = (g_ref, whi_ref, wlo_ref, rb_ref, h2_ref, idx_ref, gate_ref, rank_ref, cnt_ref, carry_sc)

    @pl.when(i == 0)
    def _():
        carry_sc[...] = jnp.zeros(carry_sc.shape, F32)

    @pl.when(i < n_prompt_tiles)
    def _():
        _route_rows(xp_ref, shp_ref, scp_ref, *out_refs)

    @pl.when(i >= n_prompt_tiles)
    def _():
        _route_rows(xs_ref, shs_ref, scs_ref, *out_refs)


def _route_rows(x1_ref, sh_ref, sc_ref, g_ref, whi_ref, wlo_ref, rb_ref, h2_ref, idx_ref, gate_ref, rank_ref,
                cnt_ref, carry_sc):
    h2 = _rms(x1_ref[...], g_ref[...]) * (1.0 + sc_ref[...]) + sh_ref[...]
    h2_ref[...] = h2
    hi = h2.astype(BF16)
    lo = (h2 - hi.astype(F32)).astype(BF16)
    logits = _dot(hi, whi_ref[...]) + _dot(lo, whi_ref[...]) + _dot(hi, wlo_ref[...]) + rb_ref[...]
    tm = logits.shape[0]
    work = jnp.transpose(logits)[:N_EXPERTS]
    expert = lax.broadcasted_iota(jnp.int32, work.shape, 0)
    vals, idxs = [], []
    for _ in range(TOP_K):
        mx = jnp.max(work, axis=0, keepdims=True)
        ix = jnp.min(jnp.where(work == mx, expert, N_EXPERTS), axis=0, keepdims=True)
        vals.append(mx)
        idxs.append(ix)
        work = jnp.where(expert == ix, -jnp.inf, work)
    es = [jnp.exp(v - vals[0]) for v in vals]
    tot = es[0]
    for e in es[1:]:
        tot = tot + e

    onehot = jnp.zeros(work.shape, F32)
    for j in range(TOP_K):
        onehot = jnp.where(expert == idxs[j], 1.0, onehot)
    earlier = (lax.broadcasted_iota(jnp.int32, (tm, tm), 0) < lax.broadcasted_iota(jnp.int32, (tm, tm), 1))
    within = _dot(onehot.astype(BF16), jnp.where(earlier, 1.0, 0.0).astype(BF16))
    carry = carry_sc[...]
    rank_full = within + (jnp.tile(carry, (1, tm // LANES)) if tm >= LANES else carry[:, :tm])
    ranks = [jnp.sum(jnp.where(expert == idxs[j], rank_full, 0.0), axis=0, keepdims=True) for j in range(TOP_K)]
    carry_sc[...] = carry_sc[...] + jnp.sum(onehot, axis=1, keepdims=True)
    cnt_ref[...] = carry_sc[...]

    row = lax.broadcasted_iota(jnp.int32, (ROW_GROUP, tm), 0)
    idx8 = jnp.zeros((ROW_GROUP, tm), jnp.int32)
    gate8 = jnp.zeros((ROW_GROUP, tm), F32)
    rank8 = jnp.zeros((ROW_GROUP, tm), F32)
    for j in range(TOP_K):
        idx8 = jnp.where(row == j, idxs[j], idx8)
        gate8 = jnp.where(row == j, es[j] / tot, gate8)
        rank8 = jnp.where(row == j, ranks[j], rank8)
    idx_ref[...] = idx8
    gate_ref[...] = gate8
    rank_ref[...] = rank8.astype(jnp.int32)


def _router(x1p, shift_p, scale_p, x1s, shift_s, scale_s, wts, *, tm, tiles_per_batch):
    tp, d = x1p.shape
    ts = x1s.shape[0]
    n_p, n_s = tp // tm, ts // tm
    t_all = tp + ts
    row = lambda w: pl.BlockSpec((tm, w), lambda i: (i, 0))
    p_row = pl.BlockSpec((tm, d), lambda i: (jnp.minimum(i, n_p - 1), 0))
    p_mod = pl.BlockSpec((None, 1, d), lambda i: (jnp.minimum(i, n_p - 1) // tiles_per_batch, 0, 0))
    s_row = pl.BlockSpec((tm, d), lambda i: (jnp.maximum(i - n_p, 0), 0))
    by_choice = pl.BlockSpec((ROW_GROUP, tm), lambda i: (0, i))
    consts = [wts["gffn"], wts["rw_hi"], wts["rw_lo"], wts["rb"]]
    return pl.pallas_call(
        functools.partial(_router_kernel, n_prompt_tiles=n_p),
        grid=(n_p + n_s,),
        in_specs=[p_row, p_mod, p_mod, s_row, s_row, s_row] + [_const_spec(c.shape) for c in consts],
        out_specs=[row(d), by_choice, by_choice, by_choice, _const_spec((N_EXPERTS, LANES))],
        out_shape=[jax.ShapeDtypeStruct((t_all, d), F32), jax.ShapeDtypeStruct((ROW_GROUP, t_all), jnp.int32),
                   jax.ShapeDtypeStruct((ROW_GROUP, t_all), F32), jax.ShapeDtypeStruct((ROW_GROUP, t_all), jnp.int32),
                   jax.ShapeDtypeStruct((N_EXPERTS, LANES), F32)],
        scratch_shapes=[pltpu.VMEM((N_EXPERTS, LANES), F32)],
        compiler_params=_params(("arbitrary",)),
        name="router",
    )(x1p, shift_p, scale_p, x1s, shift_s, scale_s, *consts)


ROW_GROUP = 8


def _slot_offset(slot, n_rows):
    return slot * n_rows if isinstance(slot, int) else pl.multiple_of(slot * n_rows, n_rows)


def _row_gather_start(idx_smem, slot, src_hbm, dst_vmem, sem, n_rows):
    base = _slot_offset(slot, n_rows)

    def group(g, carry):
        r0 = g * ROW_GROUP
        for j in range(ROW_GROUP):
            pltpu.make_async_copy(src_hbm.at[pl.ds(idx_smem[base + r0 + j], 1)], dst_vmem.at[r0 + j],
                                  sem).start(priority=j % 2)
        return carry
    lax.fori_loop(0, n_rows // ROW_GROUP, group, 0)


def _row_gather_wait(dst_vmem, sem):
    pltpu.make_async_copy(dst_vmem, dst_vmem, sem).wait()


def _gather_pipeline(i, n_steps, idx_hbm, idx_smem, isem, src_hbm, buf, gsem, n_rows, first_tile):
    def idx_copy(blk, slot):
        return pltpu.make_async_copy(idx_hbm.at[first_tile + blk, 0],
                                     idx_smem.at[pl.ds(_slot_offset(slot, n_rows), n_rows)], isem.at[slot])

    @pl.when(i == 0)
    def _():
        idx_copy(0, 0).start()
        idx_copy(0, 0).wait()
        _row_gather_start(idx_smem, 0, src_hbm, buf.at[0], gsem.at[0], n_rows)

        @pl.when(n_steps > 1)
        def _():
            idx_copy(1, 1).start()

    nxt = (i + 1) % 2

    @pl.when(i + 1 < n_steps)
    def _():
        idx_copy(i + 1, nxt).wait()
        _row_gather_start(idx_smem, nxt, src_hbm, buf.at[nxt], gsem.at[nxt], n_rows)

    @pl.when(i + 2 < n_steps)
    def _():
        idx_copy(i + 2, i % 2).start()

    _row_gather_wait(buf.at[i % 2], gsem.at[i % 2])


def _dispatch_kernel(pend_ref, nused_ref, dest_ref, h2_ref, xs_ref, zbuf, idx_smem, isem, csem, zsem, *,
                     tm, n_blocks):
    i = pl.program_id(0)
    n = pl.num_programs(0)
    rows = TOP_K * tm

    def idx_copy(blk, slot):
        return pltpu.make_async_copy(dest_ref.at[blk, 0], idx_smem.at[pl.ds(_slot_offset(slot, rows), rows)],
                                     isem.at[slot])

    def zero_copy(block_start):
        start = pl.multiple_of(block_start, MOE_ROWS)
        return pltpu.make_async_copy(zbuf, xs_ref.at[pl.ds(start, MOE_ROWS)], zsem)

    def rows_done():
        return pltpu.make_async_copy(xs_ref.at[pl.ds(0, rows)], xs_ref.at[pl.ds(0, rows)], csem)

    @pl.when(i == 0)
    def _():
        idx_copy(0, 0).start()
        zbuf[...] = jnp.zeros(zbuf.shape, F32)
        n_used = nused_ref[0]

        def last_block(e, carry):
            zero_copy(jnp.maximum(pend_ref[e] - MOE_ROWS, 0)).start()
            return carry
        lax.fori_loop(0, N_EXPERTS, last_block, 0)

        def tail_block(b, carry):
            zero_copy(b * MOE_ROWS).start()
            return carry
        lax.fori_loop(n_used, n_blocks, tail_block, 0)

        def drain(b, carry):
            zero_copy(0).wait()
            return carry
        lax.fori_loop(0, N_EXPERTS + n_blocks - n_used, drain, 0)

    slot = i % 2
    idx_copy(i, slot).wait()

    @pl.when(i + 1 < n)
    def _():
        idx_copy(i + 1, 1 - slot).start()

    base = _slot_offset(slot, rows)

    def group(g, carry):
        r0 = g * ROW_GROUP
        for j in range(ROW_GROUP):
            src = h2_ref.at[g, pl.ds(j, 1)]
            for kk in range(TOP_K):
                dst = xs_ref.at[pl.ds(idx_smem[base + kk * tm + r0 + j], 1)]
                pltpu.make_async_copy(src, dst, csem).start(priority=kk % 2)
        return carry
    lax.fori_loop(0, tm // ROW_GROUP, group, 0)
    rows_done().wait()


def _dispatch(pend, n_used, dest_tiles, h2_all, *, tm, n_blocks):
    t_all, d = h2_all.shape
    grid_spec = pltpu.PrefetchScalarGridSpec(
        num_scalar_prefetch=2,
        grid=(t_all // tm,),
        in_specs=[pl.BlockSpec(memory_space=pl.ANY),
                  pl.BlockSpec((tm // ROW_GROUP, ROW_GROUP, d), lambda i, pe, nu: (i, 0, 0))],
        out_specs=pl.BlockSpec(memory_space=pl.ANY),
        scratch_shapes=[pltpu.VMEM((MOE_ROWS, d), F32),
                        pltpu.SMEM((2 * TOP_K * tm,), jnp.int32),
                        pltpu.SemaphoreType.DMA((2,)),
                        pltpu.SemaphoreType.DMA(()),
                        pltpu.SemaphoreType.DMA(())],
    )
    return pl.pallas_call(
        functools.partial(_dispatch_kernel, tm=tm, n_blocks=n_blocks),
        grid_spec=grid_spec,
        out_shape=jax.ShapeDtypeStruct((n_blocks * MOE_ROWS, d), F32),
        compiler_params=_params(("arbitrary",)),
        name="dispatch",
    )(pend, n_used, dest_tiles, h2_all.reshape(t_all // ROW_GROUP, ROW_GROUP, d))


def _moe_kernel(be_ref, nused_ref, xs_ref, wgu_ref, bgu_ref, wdn_ref, bdn_ref, y_ref, wgu_bf, wdn_bf, *, d_model):
    i = pl.program_id(0)
    n_used = nused_ref[0]

    @pl.when(i < n_used)
    def _():
        prev = be_ref[jnp.maximum(i - 1, 0)]

        @pl.when((i == 0) | (be_ref[i] != prev))
        def _():
            wgu_bf[...] = wgu_ref[...].astype(BF16)
            wdn_bf[...] = wdn_ref[...].astype(BF16)

        xb = xs_ref[...].astype(BF16)
        gu = _dot(xb, wgu_bf[...]) + bgu_ref[...]
        g = jnp.minimum(gu[:, :d_model], SWIGLU_LIMIT)
        lin = jnp.clip(gu[:, d_model:], -SWIGLU_LIMIT, SWIGLU_LIMIT)
        act = g * jax.nn.sigmoid(SWIGLU_ALPHA * g) * (lin + 1.0)
        y_ref[...] = _dot(act.astype(BF16), wdn_bf[...]) + bdn_ref[...]

    @pl.when(i >= n_used)
    def _():
        y_ref[...] = jnp.zeros(y_ref.shape, F32)


def _moe_experts(block_e, n_used, xs, w_gu, b_gu, w_dn, b_dn):
    n_blocks = xs.shape[0] // MOE_ROWS
    e, d, d2 = w_gu.shape
    grid_spec = pltpu.PrefetchScalarGridSpec(
        num_scalar_prefetch=2,
        grid=(n_blocks,),
        in_specs=[pl.BlockSpec((MOE_ROWS, d), lambda i, be, nu: (jnp.minimum(i, nu[0] - 1), 0)),
                  pl.BlockSpec((None, d, d2), lambda i, be, nu: (be[i], 0, 0)),
                  pl.BlockSpec((None, 1, d2), lambda i, be, nu: (be[i], 0, 0)),
                  pl.BlockSpec((None, d, d), lambda i, be, nu: (be[i], 0, 0)),
                  pl.BlockSpec((None, 1, d), lambda i, be, nu: (be[i], 0, 0))],
        out_specs=pl.BlockSpec((MOE_ROWS, d), lambda i, be, nu: (i, 0)),
        scratch_shapes=[pltpu.VMEM((d, d2), BF16),
                        pltpu.VMEM((d, d), BF16)],
    )
    return pl.pallas_call(
        functools.partial(_moe_kernel, d_model=d),
        grid_spec=grid_spec,
        out_shape=jax.ShapeDtypeStruct((n_blocks * MOE_ROWS, d), F32),
        compiler_params=_params(("arbitrary",)),
        name="moe_experts",
    )(block_e, n_used, xs, w_gu, b_gu.reshape(e, 1, d2), w_dn, b_dn.reshape(e, 1, d))


def _combine_kernel(pos_ref, ys_ref, x1_ref, gate_ref, gm_ref, gfin_ref, y_ref, ybuf, idx_smem, isem, gsem, *,
                    tm, first_tile, final_norm):
    i = pl.program_id(0)
    _gather_pipeline(i, pl.num_programs(0), pos_ref, idx_smem, isem, ys_ref, ybuf, gsem, TOP_K * tm, first_tile)
    gate = jnp.transpose(jnp.concatenate([gate_ref[...], jnp.zeros((LANES - ROW_GROUP, tm), F32)], axis=0))
    f = jnp.zeros(x1_ref.shape, F32)
    for kk in range(TOP_K):
        f = f + gate[:, kk:kk + 1] * ybuf[i % 2, kk * tm:(kk + 1) * tm, 0, :]
    x2 = x1_ref[...] + gm_ref[...] * f
    y_ref[...] = _rms(x2, gfin_ref[...]) if final_norm else x2


def _combine(pos_tiles, ys, x1, gate, g_m, final_g, *, tm, first_tile, per_row, tiles_per_batch, final_norm):
    t, d = x1.shape
    row = lambda w: pl.BlockSpec((tm, w), lambda i: (i, 0))
    return pl.pallas_call(
        functools.partial(_combine_kernel, tm=tm, first_tile=first_tile, final_norm=final_norm),
        grid=(t // tm,),
        in_specs=[pl.BlockSpec(memory_space=pl.ANY), pl.BlockSpec(memory_space=pl.ANY), row(d),
                  pl.BlockSpec((ROW_GROUP, tm), lambda i: (0, first_tile + i)),
                  _mod_spec(per_row, tm, d, tiles_per_batch), _const_spec((1, d))],
        out_specs=row(d),
        out_shape=jax.ShapeDtypeStruct((t, d), F32),
        scratch_shapes=[pltpu.VMEM((2, TOP_K * tm, 1, d), F32),
                        pltpu.SMEM((2 * TOP_K * tm,), jnp.int32),
                        pltpu.SemaphoreType.DMA((2,)),
                        pltpu.SemaphoreType.DMA((2,))],
        compiler_params=_params(("arbitrary",)),
        name="combine",
    )(pos_tiles, ys, x1, gate, g_m, final_g)


def _rope_tables(pos):
    inv = ROPE_THETA ** (-jnp.arange(0, QK_ROPE, 2, dtype=F32) / QK_ROPE)
    ang = pos.astype(F32)[:, None] * inv[None, :]
    cos, sin = jnp.cos(ang), jnp.sin(ang)
    n = pos.shape[0]
    cc = jnp.concatenate([cos, cos, jnp.ones((n, LANES - QK_ROPE), F32)], axis=1)
    ss = jnp.concatenate([sin, sin, jnp.zeros((n, LANES - QK_ROPE), F32)], axis=1)
    return cc, ss


def _swap_halves(w):
    half = QK_ROPE // 2
    return jnp.concatenate([-w[..., half:], w[..., :half]], axis=-1)


def _layer_weights(l, w_in, norm_mix_g, q_norm_g, w_uq, kv_norm_g, w_uk, w_uv, w_pa, sg_norm_g, sg_norm_b,
                   w_pb, w_o, norm_ffn_g, router_w, router_b):
    d = w_in.shape[1]
    wi = w_in[l]
    o_kr = Q_RANK + KV_RANK
    o_u = o_kr + QK_ROPE
    o_v = o_u + SG_WIDTH
    o_ga = o_v + SG_WIDTH
    kr = wi[:, o_kr:o_u]
    zpad = jnp.zeros((d, LANES - QK_ROPE), F32)
    w_in_r = jnp.concatenate([wi[:, :o_kr], wi[:, o_u:o_ga], wi[:, o_ga:],
                              kr, zpad, _swap_halves(kr), zpad], axis=1).astype(BF16)
    uq = w_uq[l]
    nope, rope = uq[..., :QK_NOPE], uq[..., QK_NOPE:]
    z32 = jnp.zeros(rope.shape[:2] + (HEAD_SLOT - QK_NOPE - QK_ROPE,), F32)
    wq = jnp.concatenate([rope, nope, z32], axis=-1).reshape(Q_RANK, QK_WIDTH).astype(BF16)
    wqs = jnp.concatenate([_swap_halves(rope), jnp.zeros_like(nope), z32], axis=-1)
    wqs = wqs.reshape(Q_RANK, QK_WIDTH).astype(BF16)
    uk = w_uk[l]
    zk_lo = jnp.zeros(uk.shape[:2] + (QK_ROPE,), F32)
    zk_hi = jnp.zeros(uk.shape[:2] + (HEAD_SLOT - QK_NOPE - QK_ROPE,), F32)
    wk = jnp.concatenate([zk_lo, uk, zk_hi], axis=-1).reshape(KV_RANK, QK_WIDTH).astype(BF16)
    wv = w_uv[l].reshape(KV_RANK, V_WIDTH).astype(BF16)
    wv_slot = jnp.concatenate([w_uv[l], jnp.zeros_like(w_uv[l])], axis=-1).reshape(KV_RANK, QK_WIDTH).astype(BF16)
    vone = jnp.tile(jnp.concatenate([jnp.zeros((V_HEAD,), F32), jnp.ones((HEAD_SLOT - V_HEAD,), F32)]),
                    MLA_HEADS).reshape(1, QK_WIDTH)
    ukt = jnp.transpose(uk, (1, 2, 0))
    eye = jnp.broadcast_to(jnp.eye(QK_ROPE, LANES, dtype=F32), (MLA_HEADS, QK_ROPE, LANES))
    top = jnp.concatenate([jnp.zeros((MLA_HEADS, QK_ROPE, KV_RANK), F32), eye], axis=-1)
    mid = jnp.concatenate([ukt, jnp.zeros((MLA_HEADS, QK_NOPE, LANES), F32)], axis=-1)
    bot = jnp.zeros((MLA_HEADS, HEAD_SLOT - QK_NOPE - QK_ROPE, KV_RANK + LANES), F32)
    mabs = jnp.concatenate([top, mid, bot], axis=1).astype(BF16)
    rw = jnp.pad(router_w[l], ((0, 0), (0, ROUTER_PAD - N_EXPERTS)))
    rw_hi = rw.astype(BF16)
    rw_lo = (rw - rw_hi.astype(F32)).astype(BF16)
    rb = jnp.concatenate([router_b[l], jnp.full((ROUTER_PAD - N_EXPERTS,), NEG_BIG, F32)]).reshape(1, ROUTER_PAD)
    return dict(
        w_in_r=w_in_r, gmix=norm_mix_g[l].reshape(1, d),
        gq=(q_norm_g[l] * (ATTN_SCALE * LOG2_E)).reshape(1, Q_RANK),
        gkv=kv_norm_g[l].reshape(1, KV_RANK), wq=wq, wqs=wqs, wk=wk, wv=wv, wv_slot=wv_slot, vone=vone, mabs=mabs,
        sgg=sg_norm_g[l].reshape(1, SG_WIDTH), sgb=sg_norm_b[l].reshape(1, SG_WIDTH),
        w_pa=w_pa[l].astype(BF16), w_pb=w_pb[l].astype(BF16), w_o=w_o[l].astype(BF16),
        gffn=norm_ffn_g[l].reshape(1, d), rw_hi=rw_hi, rw_lo=rw_lo, rb=rb)


def _spatial_mix_weights(w_s, b_s, seq, n_batch):
    gw = SG_WIDTH // SG_GROUPS
    tril = jnp.tril(jnp.ones((SG_CHUNK, SG_CHUNK), dtype=bool))
    w = jnp.where(tril[None], w_s, 0.0)
    if seq % SG_CHUNK == 0:
        mixw = w
        bias_t = b_s
    else:
        assert seq < SG_CHUNK
        blk = w[:, :seq, :seq]
        pos = jnp.arange(n_batch * seq, dtype=jnp.int32)
        rep = (pos[:, None] % seq == jnp.arange(seq, dtype=jnp.int32)[None, :]).astype(F32)
        tiled = jnp.einsum("rt,gts,cs->grc", rep, blk, rep, precision=lax.Precision.HIGHEST)
        mixw = jnp.where((pos[:, None] // seq == pos[None, :] // seq)[None], tiled, 0.0)
        bias_t = jnp.tile(b_s[:, :seq], (1, n_batch))
    bias = jnp.repeat(jnp.transpose(bias_t), gw, axis=1)
    return mixw.astype(BF16), bias


def _routing_tables(idx, rank, counts_f, n_blocks):
    idx, rank = idx[:TOP_K], rank[:TOP_K]
    counts = counts_f[:, 0].astype(jnp.int32)
    padded = (counts + MOE_ROWS - 1) // MOE_ROWS * MOE_ROWS
    pend = jnp.cumsum(padded).astype(jnp.int32)
    pstart = pend - padded
    experts = jnp.arange(N_EXPERTS, dtype=jnp.int32)
    dest = rank + jnp.sum(jnp.where(idx[..., None] == experts, pstart, 0), axis=-1)
    block_start = jnp.arange(n_blocks, dtype=jnp.int32) * MOE_ROWS
    block_e = jnp.minimum(jnp.sum((pend[None, :] <= block_start[:, None]).astype(jnp.int32), axis=1), N_EXPERTS - 1)
    n_used = (pend[-1:] // MOE_ROWS).astype(jnp.int32)
    return dest.astype(jnp.int32), pend, block_e.astype(jnp.int32), n_used


def _pos_tiles(pos, tm):
    t = pos.shape[1]
    return jnp.transpose(pos.reshape(TOP_K, t // tm, tm), (1, 0, 2)).reshape(t // tm, 1, TOP_K * tm)


def _stack_layers(per_layer):
    return per_layer[0][None] if len(per_layer) == 1 else jnp.stack(per_layer)


def _pick_tile(n, pref):
    t = min(n, pref)
    assert n % t == 0 and t % 8 == 0
    return t


def kernel(x_prompt, x_sample, cache_ckv, cache_krope, c_prompt, c_sample, ada_w, ada_b, norm_mix_g, w_in, q_norm_g, w_uq, kv_norm_g, w_uk, w_uv, w_pa, sg_norm_g, sg_norm_b, w_spatial, b_spatial, w_pb, w_o, norm_ffn_g, router_w, router_b, w_gu, b_gu, w_dn, b_dn, final_g):
    bp, lp, d = x_prompt.shape
    bs, ls, _ = x_sample.shape
    depth = w_in.shape[0]
    past = cache_ckv.shape[2]
    tp, ts = bp * lp, bs * ls
    assert lp % SG_CHUNK == 0 and ls <= SG_CHUNK

    tm_p = _pick_tile(lp, INPROJ_TILE)
    tm_s = _pick_tile(ts, SAMPLE_TILE)
    t_attn = _pick_tile(lp, ATTN_TILE)
    tpb = lp // tm_p

    cc_p, ss_p = _rope_tables(jnp.arange(lp, dtype=jnp.int32))
    cc_s, ss_s = _rope_tables(past + jnp.arange(ls, dtype=jnp.int32))
    cc_s, ss_s = jnp.tile(cc_s, (bs, 1)), jnp.tile(ss_s, (bs, 1))

    b_all = bp + bs
    b_pad = -(-b_all // 8) * 8
    c_all = jnp.concatenate([c_prompt, c_sample, jnp.zeros((b_pad - b_all, d), F32)], axis=0)

    xp = x_prompt.reshape(tp, d)
    xs = x_sample.reshape(ts, d)
    outs = dict(ckv_p=[], kr_p=[], ckv_s=[], kr_s=[], v_s=[])
    final_g2 = final_g.reshape(1, d)
    for l in range(depth):
        wts = _layer_weights(l, w_in, norm_mix_g, q_norm_g, w_uq, kv_norm_g, w_uk, w_uv, w_pa, sg_norm_g,
                             sg_norm_b, w_pb, w_o, norm_ffn_g, router_w, router_b)
        mod = _adaln(c_all, ada_w[l], ada_b[l])
        mod_p = [mod[:bp, j * d:(j + 1) * d].reshape(bp, 1, d) for j in range(6)]
        mod_s = [jnp.repeat(mod[bp:b_all, j * d:(j + 1) * d], ls, axis=0) for j in range(6)]

        q, k, v, ckv, kr, u, vn, ga, gb = _inproj(xp, mod_p[0], mod_p[1], cc_p, ss_p, wts, tm=tm_p, per_row=False,
                                                  tiles_per_batch=tpb, vn_dtype=BF16)
        o = _attn_prompt(q.reshape(bp, lp, QK_WIDTH), k.reshape(bp, lp, QK_WIDTH), v.reshape(bp, lp, QK_WIDTH),
                         tq=t_attn, tk=t_attn).reshape(tp, V_WIDTH)
        mixw, bias = _spatial_mix_weights(w_spatial[l], b_spatial[l], lp, bp)
        tm_m = _pick_tile(lp, MERGE_TILE)
        x1p = _merge(o, u, vn, ga, gb, xp, mod_p[2], mixw, bias, wts, tm=tm_m, chunk=SG_CHUNK, per_row=False,
                     tiles_per_batch=lp // tm_m)
        outs["ckv_p"].append(ckv.reshape(bp, lp, KV_RANK))
        outs["kr_p"].append(jnp.transpose(kr, (0, 2, 1)))

        q, k, v, ckv, kr, u, vn, ga, gb = _inproj(xs, mod_s[0], mod_s[1], cc_s, ss_s, wts, tm=tm_s, per_row=True,
                                                  tiles_per_batch=1, vn_dtype=F32)
        ckv3 = ckv.reshape(bs, ls, KV_RANK)
        kr_rbl = kr.reshape(QK_ROPE, bs, ls)
        kr3 = jnp.transpose(kr_rbl, (1, 2, 0))
        o = _attn_sample(q.reshape(bs, ls, QK_WIDTH), cache_ckv[l], jnp.transpose(cache_krope[l], (0, 2, 1)),
                         ckv3, jnp.transpose(kr_rbl, (1, 0, 2)), wts["mabs"], wts["wv"]).reshape(ts, V_WIDTH)
        mixw, bias = _spatial_mix_weights(w_spatial[l], b_spatial[l], ls, tm_s // ls)
        x1s = _merge(o, u, vn, ga, gb, xs, mod_s[2], mixw, bias, wts, tm=tm_s, chunk=tm_s, per_row=True,
                     tiles_per_batch=1)
        outs["ckv_s"].append(ckv3)
        outs["kr_s"].append(kr3)
        outs["v_s"].append(vn.reshape(bs, ls, SG_WIDTH))

        t_all = tp + ts
        tm_r = _pick_tile(math.gcd(tp, ts), ROW_TILE)
        h2_all, idx, gate, rank, counts = _router(x1p, mod_p[3], mod_p[4], x1s, mod_s[3], mod_s[4], wts, tm=tm_r,
                                                  tiles_per_batch=lp // tm_r)
        n_blocks = -(-(t_all * TOP_K) // MOE_ROWS) + N_EXPERTS
        pos, pend, block_e, n_used = _routing_tables(idx, rank, counts, n_blocks)
        pos_tiles = _pos_tiles(pos, tm_r)
        x_sorted = _dispatch(pend, n_used, pos_tiles, h2_all, tm=tm_r, n_blocks=n_blocks)
        ys = _moe_experts(block_e, n_used, x_sorted, w_gu[l], b_gu[l], w_dn[l], b_dn[l])
        last = l == depth - 1
        xp = _combine(pos_tiles, ys, x1p, gate, mod_p[5], final_g2, tm=tm_r, first_tile=0, per_row=False,
                      tiles_per_batch=lp // tm_r, final_norm=last)
        xs = _combine(pos_tiles, ys, x1s, gate, mod_s[5], final_g2, tm=tm_r, first_tile=tp // tm_r, per_row=True,
                      tiles_per_batch=1, final_norm=last)
    return (xp.reshape(bp, lp, d), xs.reshape(bs, ls, d),
            _stack_layers(outs["ckv_p"]), _stack_layers(outs["kr_p"]),
            _stack_layers(outs["ckv_s"]), _stack_layers(outs["kr_s"]), _stack_layers(outs["v_s"]))
```

```python
import functools
import math

import jax
import jax.numpy as jnp
from jax import lax
from jax.experimental import pallas as pl
from jax.experimental.pallas import tpu as pltpu

F32 = jnp.float32
BF16 = jnp.bfloat16

LANES = 128
VMEM_LIMIT_BYTES = 56 * 1024 * 1024

CHUNK = 64
CHUNK_SHIFT = 6
MLA_HEADS = 8
QK_NOPE = 64
QK_ROPE = 32
V_HEAD = 64
V_HEAD_SHIFT = 6
Q_RANK = 384
KV_RANK = 256
ROPE_THETA = 10000.0
ATTN_SCALE = 1.0 / math.sqrt(QK_NOPE + QK_ROPE)
LOG2_E = math.log2(math.e)
SG_CHUNK = 128
SG_GROUPS = 4
SG_WIDTH = 512
N_EXPERTS = 32
TOP_K = 4
SWIGLU_LIMIT = 7.0
SWIGLU_ALPHA = 1.702
EPS = 1e-6

HEAD_SLOT = LANES
QK_WIDTH = MLA_HEADS * HEAD_SLOT
V_WIDTH = MLA_HEADS * V_HEAD
MOE_ROWS = 512
ROW_TILE = 512
INPROJ_TILE = 512
MERGE_TILE = 512
SAMPLE_TILE = 512
ATTN_TILE = 1024
ATTN_SUB_KEYS = 256
ROUTER_PAD = LANES
NEG_BIG = -1e30

_C_CQ = 0
_C_CKV = _C_CQ + Q_RANK
_C_U = _C_CKV + KV_RANK
_C_V = _C_U + SG_WIDTH
_C_GA = _C_V + SG_WIDTH


def _params(sem):
    return pltpu.CompilerParams(dimension_semantics=sem, vmem_limit_bytes=VMEM_LIMIT_BYTES)


def _dot(a, b):
    return jnp.dot(a, b, preferred_element_type=F32)


def _dot_nt(a, b):
    return lax.dot_general(a, b, (((1,), (1,)), ((), ())), preferred_element_type=F32)


def _rms(x, g):
    return x * lax.rsqrt(jnp.mean(x * x, axis=-1, keepdims=True) + EPS) * g


def _adaln_kernel(c_ref, w_ref, b_ref, o_ref):
    c = c_ref[...]
    s = (c * jax.nn.sigmoid(c)).astype(BF16)
    o_ref[...] = _dot(s, w_ref[...].astype(BF16)) + b_ref[...]


def _adaln(c_all, ada_w, ada_b):
    bp, d = c_all.shape
    n = ada_w.shape[1]
    return pl.pallas_call(
        _adaln_kernel,
        grid=(n // d,),
        in_specs=[pl.BlockSpec((bp, d), lambda j: (0, 0)),
                  pl.BlockSpec((d, d), lambda j: (0, j)),
                  pl.BlockSpec((1, d), lambda j: (0, j))],
        out_specs=pl.BlockSpec((bp, d), lambda j: (0, j)),
        out_shape=jax.ShapeDtypeStruct((bp, n), F32),
        compiler_params=_params(("arbitrary",)),
        name="adaln",
    )(c_all, ada_w, ada_b.reshape(1, n))


def _rope_slot(a, cc, ss):
    half = QK_ROPE // 2
    lane = lax.broadcasted_iota(jnp.int32, a.shape, 1)
    swapped = jnp.where(lane < half, -pltpu.roll(a, LANES - half, axis=1), pltpu.roll(a, half, axis=1))
    return a * cc + swapped * ss


def _inproj_kernel(x_ref, sh_ref, sc_ref, gmix_ref, cc_ref, ss_ref, win_ref, gq_ref, gkv_ref,
                   wq_ref, wk_ref, wv_ref, vone_ref, sgg_ref, sgb_ref,
                   q_ref, k_ref, v_ref, ckv_ref, kr_ref, u_ref, vn_ref, ga_ref, gb_ref, *, d_model):
    x = x_ref[...]
    h = (_rms(x, gmix_ref[...]) * (1.0 + sc_ref[...]) + sh_ref[...]).astype(BF16)

    def proj(lo, width):
        return _dot(h, win_ref[:, lo:lo + width])

    cc = cc_ref[...]
    ss = ss_ref[...]
    c_gb = _C_GA + d_model
    c_kr = c_gb + d_model

    cqn = _rms(proj(_C_CQ, Q_RANK), gq_ref[...]).astype(BF16)
    qa = _dot(cqn, wq_ref[...])
    for hd in range(MLA_HEADS):
        sl = slice(hd * HEAD_SLOT, (hd + 1) * HEAD_SLOT)
        q_ref[:, sl] = _rope_slot(qa[:, sl], cc, ss).astype(BF16)

    ckvn = _rms(proj(_C_CKV, KV_RANK), gkv_ref[...])
    ckv_ref[...] = ckvn
    ckvb = ckvn.astype(BF16)
    krs = _rope_slot(proj(c_kr, LANES), cc, ss)
    kr_ref[...] = jnp.transpose(krs)[:QK_ROPE]
    kn = _dot(ckvb, wk_ref[...])
    for hd in range(MLA_HEADS):
        sl = slice(hd * HEAD_SLOT, (hd + 1) * HEAD_SLOT)
        k_ref[:, sl] = (kn[:, sl] + krs).astype(BF16)
    v_ref[...] = (_dot(ckvb, wv_ref[...]) + vone_ref[...]).astype(BF16)

    u_ref[...] = proj(_C_U, SG_WIDTH).astype(u_ref.dtype)
    vv = proj(_C_V, SG_WIDTH)
    mu = jnp.mean(vv, axis=-1, keepdims=True)
    vc = vv - mu
    var = jnp.mean(vc * vc, axis=-1, keepdims=True)
    vn_ref[...] = (vc * lax.rsqrt(var + EPS) * sgg_ref[...] + sgb_ref[...]).astype(vn_ref.dtype)
    ga_ref[...] = proj(_C_GA, d_model).astype(BF16)
    gb_ref[...] = proj(c_gb, d_model).astype(BF16)


def _mod_spec(per_row, tm, d, tiles_per_batch):
    if per_row:
        return pl.BlockSpec((tm, d), lambda i: (i, 0))
    return pl.BlockSpec((None, 1, d), lambda i: (i // tiles_per_batch, 0, 0))


def _const_spec(shape):
    nd = len(shape)
    return pl.BlockSpec(shape, lambda i: (0,) * nd)


def _inproj(x2d, shift, scale, cc, ss, wts, *, tm, per_row, tiles_per_batch, vn_dtype):
    t, d = x2d.shape
    n_tab = cc.shape[0] // tm
    row = lambda w: pl.BlockSpec((tm, w), lambda i: (i, 0))
    tab = pl.BlockSpec((tm, LANES), lambda i: (i % n_tab, 0))
    mod = _mod_spec(per_row, tm, d, tiles_per_batch)
    consts = [wts["w_in_r"], wts["gq"], wts["gkv"], wts["wq"], wts["wk"], wts["wv_slot"],
              wts["vone"], wts["sgg"], wts["sgb"]]
    if per_row:
        kr_shape = jax.ShapeDtypeStruct((QK_ROPE, t), F32)
        kr_spec = pl.BlockSpec((QK_ROPE, tm), lambda i: (0, i))
    else:
        kr_shape = jax.ShapeDtypeStruct((t // (tiles_per_batch * tm), QK_ROPE, tiles_per_batch * tm), F32)
        kr_spec = pl.BlockSpec((None, QK_ROPE, tm), lambda i: (i // tiles_per_batch, 0, i % tiles_per_batch))
    out_shapes = [jax.ShapeDtypeStruct((t, QK_WIDTH), BF16), jax.ShapeDtypeStruct((t, QK_WIDTH), BF16),
                  jax.ShapeDtypeStruct((t, QK_WIDTH), BF16), jax.ShapeDtypeStruct((t, KV_RANK), F32),
                  kr_shape, jax.ShapeDtypeStruct((t, SG_WIDTH), BF16),
                  jax.ShapeDtypeStruct((t, SG_WIDTH), vn_dtype), jax.ShapeDtypeStruct((t, d), BF16),
                  jax.ShapeDtypeStruct((t, d), BF16)]
    out_specs = [row(s.shape[1]) for s in out_shapes]
    out_specs[4] = kr_spec
    return pl.pallas_call(
        functools.partial(_inproj_kernel, d_model=d),
        grid=(t // tm,),
        in_specs=[row(d), mod, mod, _const_spec((1, d)), tab, tab] + [_const_spec(c.shape) for c in consts],
        out_specs=out_specs,
        out_shape=out_shapes,
        compiler_params=_params(("arbitrary",)),
        name="inproj",
    )(x2d, shift, scale, wts["gmix"], cc, ss, *consts)


def _attn_kernel(qi_ref, kj_ref, flag_ref, q_ref, k_ref, v_ref, o_ref, m_sc, acc_sc, *, tq, tk, sub):
    s_id = pl.program_id(1)
    qi = qi_ref[s_id]
    kj = kj_ref[s_id]
    flags = flag_ref[s_id]

    @pl.when(kj == 0)
    def _():
        m_sc[...] = jnp.full(m_sc.shape, -jnp.inf, F32)
        acc_sc[...] = jnp.zeros(acc_sc.shape, F32)

    def sweep(bias):
        for kb in range(tk // sub):
            keys = slice(kb * sub, (kb + 1) * sub)
            rows = slice(kb * sub if bias is not None else 0, tq)
            for hd in range(MLA_HEADS):
                sl = slice(hd * HEAD_SLOT, (hd + 1) * HEAD_SLOT)
                s = _dot_nt(q_ref[rows, sl], k_ref[keys, sl])
                if bias is not None:
                    s = s + bias[rows, keys]
                tiles = [s[:, c * LANES:(c + 1) * LANES] for c in range(sub // LANES)]
                m_tile = tiles[0]
                for t in tiles[1:]:
                    m_tile = jnp.maximum(m_tile, t)
                m_old = m_sc[hd, rows]
                m_new = jnp.maximum(m_old, jnp.max(m_tile, axis=-1, keepdims=True))
                alpha = jnp.exp2(m_old - m_new)
                p = jnp.concatenate([jnp.exp2(t - m_new).astype(BF16) for t in tiles], axis=1)
                acc_sc[hd, rows] = alpha * acc_sc[hd, rows] + _dot(p, v_ref[keys, sl])
                m_sc[hd, rows] = m_new

    @pl.when((flags & 2) == 0)
    def _():
        sweep(None)

    @pl.when((flags & 2) != 0)
    def _():
        row = lax.broadcasted_iota(jnp.int32, (tq, tk), 0) + qi * tq
        col = lax.broadcasted_iota(jnp.int32, (tq, tk), 1) + kj * tk
        sweep(jnp.where((col >> CHUNK_SHIFT) <= (row >> CHUNK_SHIFT), 0.0, -jnp.inf))

    @pl.when((flags & 1) != 0)
    def _():
        lane = lax.broadcasted_iota(jnp.int32, (tq, LANES), 1)
        for pr in range(MLA_HEADS // 2):
            outs = []
            for hd in (2 * pr, 2 * pr + 1):
                acc = acc_sc[hd]
                outs.append(acc / pltpu.roll(acc, V_HEAD, axis=1))
            pair = jnp.where(lane < V_HEAD, outs[0], pltpu.roll(outs[1], V_HEAD, axis=1))
            o_ref[:, pr * LANES:(pr + 1) * LANES] = pair.astype(BF16)


def _attn_prompt(q, k, v, *, tq, tk):
    b, l, _ = q.shape
    assert tq == tk
    nq = l // tq
    qi_l, kj_l, flag_l = [], [], []
    for i in range(nq):
        n_kv = ((i + 1) * tq - 1) // tk + 1
        for j in range(n_kv):
            qi_l.append(i)
            kj_l.append(j)
            masked = ((j + 1) * tk - 1) // CHUNK > (i * tq) // CHUNK
            flag_l.append((1 if j == n_kv - 1 else 0) | (2 if masked else 0))
    steps = len(qi_l)
    grid_spec = pltpu.PrefetchScalarGridSpec(
        num_scalar_prefetch=3,
        grid=(b, steps),
        in_specs=[pl.BlockSpec((None, tq, QK_WIDTH), lambda bi, s, qi, kj, fl: (bi, qi[s], 0)),
                  pl.BlockSpec((None, tk, QK_WIDTH), lambda bi, s, qi, kj, fl: (bi, kj[s], 0)),
                  pl.BlockSpec((None, tk, QK_WIDTH), lambda bi, s, qi, kj, fl: (bi, kj[s], 0))],
        out_specs=pl.BlockSpec((None, tq, V_WIDTH), lambda bi, s, qi, kj, fl: (bi, qi[s], 0)),
        scratch_shapes=[pltpu.VMEM((MLA_HEADS, tq, LANES), F32), pltpu.VMEM((MLA_HEADS, tq, LANES), F32)],
    )
    return pl.pallas_call(
        functools.partial(_attn_kernel, tq=tq, tk=tk, sub=min(tk, ATTN_SUB_KEYS)),
        grid_spec=grid_spec,
        out_shape=jax.ShapeDtypeStruct((b, l, V_WIDTH), BF16),
        compiler_params=_params(("arbitrary", "arbitrary")),
        name="attn_prompt",
    )(jnp.asarray(qi_l, jnp.int32), jnp.asarray(kj_l, jnp.int32), jnp.asarray(flag_l, jnp.int32), q, k, v)


def _attn_sample_kernel(q_ref, pckv_ref, pkr_ref, nckv_ref, nkr_ref, mabs_ref, wv_ref, o_ref, *, ls, past):
    hl = MLA_HEADS * ls
    qcat = jnp.concatenate(
        [_dot(q_ref[:, hd * HEAD_SLOT:(hd + 1) * HEAD_SLOT], mabs_ref[hd]) for hd in range(MLA_HEADS)],
        axis=0).astype(BF16)
    q_abs = qcat[:, :KV_RANK]
    q_rope = qcat[:, KV_RANK:]
    pckv = pckv_ref[...].astype(BF16)
    nckv = nckv_ref[...].astype(BF16)

    def pad_rows(kr_t):
        return jnp.concatenate([kr_t, jnp.zeros((LANES - QK_ROPE, kr_t.shape[1]), kr_t.dtype)], axis=0).astype(BF16)

    s_past = _dot_nt(q_abs, pckv) + _dot(q_rope, pad_rows(pkr_ref[...]))
    s_new = _dot_nt(q_abs, nckv) + _dot(q_rope, pad_rows(nkr_ref[...]))

    qpos_1 = lax.broadcasted_iota(jnp.int32, (ls, 1), 0) + past
    qchunk = jnp.concatenate([qpos_1] * MLA_HEADS, axis=0) >> CHUNK_SHIFT
    kchunk_past = lax.broadcasted_iota(jnp.int32, (hl, past), 1) >> CHUNK_SHIFT
    kchunk_new = (lax.broadcasted_iota(jnp.int32, (hl, ls), 1) + past) >> CHUNK_SHIFT
    s_past = jnp.where(kchunk_past <= qchunk, s_past, -jnp.inf)
    s_new = jnp.where(kchunk_new <= qchunk, s_new, -jnp.inf)

    m = jnp.maximum(jnp.max(s_past, axis=-1, keepdims=True), jnp.max(s_new, axis=-1, keepdims=True))
    p_past = jnp.exp2(s_past - m)
    p_new = jnp.exp2(s_new - m)
    denom = jnp.sum(p_past, axis=-1, keepdims=True) + jnp.sum(p_new, axis=-1, keepdims=True)
    olat = (_dot(p_past.astype(BF16), pckv) + _dot(p_new.astype(BF16), nckv)) / denom
    ofull = _dot(olat.astype(BF16), wv_ref[...])
    col_head = lax.broadcasted_iota(jnp.int32, (ls, V_WIDTH), 1) >> V_HEAD_SHIFT
    out = jnp.zeros((ls, V_WIDTH), F32)
    for hd in range(MLA_HEADS):
        out = out + jnp.where(col_head == hd, ofull[hd * ls:(hd + 1) * ls], 0.0)
    o_ref[...] = out.astype(BF16)


def _attn_sample(q, past_ckv, past_kr_t, new_ckv, new_kr_t, mabs, wv):
    b, ls, _ = q.shape
    past = past_ckv.shape[1]
    blk = lambda n, w: pl.BlockSpec((None, n, w), lambda i: (i, 0, 0))
    return pl.pallas_call(
        functools.partial(_attn_sample_kernel, ls=ls, past=past),
        grid=(b,),
        in_specs=[blk(ls, QK_WIDTH), blk(past, KV_RANK), blk(QK_ROPE, past), blk(ls, KV_RANK), blk(QK_ROPE, ls),
                  _const_spec(mabs.shape), _const_spec(wv.shape)],
        out_specs=blk(ls, V_WIDTH),
        out_shape=jax.ShapeDtypeStruct((b, ls, V_WIDTH), BF16),
        compiler_params=_params(("arbitrary",)),
        name="attn_sample",
    )(q, past_ckv, past_kr_t, new_ckv, new_kr_t, mabs, wv)


def _merge_kernel(o_ref, u_ref, vn_ref, ga_ref, gb_ref, x_ref, gate_ref, mix_ref, bias_ref,
                  wpa_ref, wpb_ref, wo_ref, x1_ref, sg_sc, *, chunk):
    tm = x_ref.shape[0]
    gw = SG_WIDTH // SG_GROUPS
    for c in range(tm // chunk):
        rows = slice(c * chunk, (c + 1) * chunk)
        for g in range(SG_GROUPS):
            cols = slice(g * gw, (g + 1) * gw)
            mixed = _dot(mix_ref[g], vn_ref[rows, cols].astype(BF16)) + bias_ref[:, cols]
            sg_sc[rows, cols] = (u_ref[rows, cols].astype(F32) * mixed).astype(BF16)
    ya = _dot(o_ref[...], wpa_ref[...])
    yb = _dot(sg_sc[...], wpb_ref[...])
    m = jax.nn.sigmoid(ga_ref[...].astype(F32)) * ya + jax.nn.sigmoid(gb_ref[...].astype(F32)) * yb
    x1_ref[...] = x_ref[...] + gate_ref[...] * _dot(m.astype(BF16), wo_ref[...])


def _merge(o, u, vn, ga, gb, x2d, gate, mixw, bias, wts, *, tm, chunk, per_row, tiles_per_batch):
    t, d = x2d.shape
    row = lambda w: pl.BlockSpec((tm, w), lambda i: (i, 0))
    consts = [mixw, bias, wts["w_pa"], wts["w_pb"], wts["w_o"]]
    return pl.pallas_call(
        functools.partial(_merge_kernel, chunk=chunk),
        grid=(t // tm,),
        in_specs=[row(V_WIDTH), row(SG_WIDTH), row(SG_WIDTH), row(d), row(d), row(d),
                  _mod_spec(per_row, tm, d, tiles_per_batch)] + [_const_spec(c.shape) for c in consts],
        out_specs=row(d),
        out_shape=jax.ShapeDtypeStruct((t, d), F32),
        scratch_shapes=[pltpu.VMEM((tm, SG_WIDTH), BF16)],
        compiler_params=_params(("arbitrary",)),
        name="merge",
    )(o, u, vn, ga, gb, x2d, gate, *consts)


def _router_kernel(xp_ref, shp_ref, scp_ref, xs_ref, shs_ref, scs_ref, g_ref, whi_ref, wlo_ref, rb_ref,
                   h2_ref, idx_ref, gate_ref, rank_ref, cnt_ref, carry_sc, *, n_prompt_tiles):
    i = pl.program_id(0)
    out_refs = (g_ref, whi_ref, wlo_ref, rb_ref, h2_ref, idx_ref, gate_ref, rank_ref, cnt_ref, carry_sc)

    @pl.when(i == 0)
    def _():
        carry_sc[...] = jnp.zeros(carry_sc.shape, F32)

    @pl.when(i < n_prompt_tiles)
    def _():
        _route_rows(xp_ref, shp_ref, scp_ref, *out_refs)

    @pl.when(i >= n_prompt_tiles)
    def _():
        _route_rows(xs_ref, shs_ref, scs_ref, *out_refs)


def _route_rows(x1_ref, sh_ref, sc_ref, g_ref, whi_ref, wlo_ref, rb_ref, h2_ref, idx_ref, gate_ref, rank_ref,
                cnt_ref, carry_sc):
    h2 = _rms(x1_ref[...], g_ref[...]) * (1.0 + sc_ref[...]) + sh_ref[...]
    h2_ref[...] = h2
    hi = h2.astype(BF16)
    lo = (h2 - hi.astype(F32)).astype(BF16)
    logits = _dot(hi, whi_ref[...]) + _dot(lo, whi_ref[...]) + _dot(hi, wlo_ref[...]) + rb_ref[...]
    tm = logits.shape[0]
    work = jnp.transpose(logits)[:N_EXPERTS]
    expert = lax.broadcasted_iota(jnp.int32, work.shape, 0)
    vals, idxs = [], []
    for _ in range(TOP_K):
        mx = jnp.max(work, axis=0, keepdims=True)
        ix = jnp.min(jnp.where(work == mx, expert, N_EXPERTS), axis=0, keepdims=True)
        vals.append(mx)
        idxs.append(ix)
        work = jnp.where(expert == ix, -jnp.inf, work)
    es = [jnp.exp(v - vals[0]) for v in vals]
    tot = es[0]
    for e in es[1:]:
        tot = tot + e

    onehot = jnp.zeros(work.shape, F32)
    for j in range(TOP_K):
        onehot = jnp.where(expert == idxs[j], 1.0, onehot)
    earlier = (lax.broadcasted_iota(jnp.int32, (tm, tm), 0) < lax.broadcasted_iota(jnp.int32, (tm, tm), 1))
    within = _dot(onehot.astype(BF16), jnp.where(earlier, 1.0, 0.0).astype(BF16))
    carry = carry_sc[...]
    rank_full = within + (jnp.tile(carry, (1, tm // LANES)) if tm >= LANES else carry[:, :tm])
    ranks = [jnp.sum(jnp.where(expert == idxs[j], rank_full, 0.0), axis=0, keepdims=True) for j in range(TOP_K)]
    carry_sc[...] = carry_sc[...] + jnp.sum(onehot, axis=1, keepdims=True)
    cnt_ref[...] = carry_sc[...]

    row = lax.broadcasted_iota(jnp.int32, (ROW_GROUP, tm), 0)
    idx8 = jnp.zeros((ROW_GROUP, tm), jnp.int32)
    gate8 = jnp.zeros((ROW_GROUP, tm), F32)
    rank8 = jnp.zeros((ROW_GROUP, tm), F32)
    for j in range(TOP_K):
        idx8 = jnp.where(row == j, idxs[j], idx8)
        gate8 = jnp.where(row == j, es[j] / tot, gate8)
        rank8 = jnp.where(row == j, ranks[j], rank8)
    idx_ref[...] = idx8
    gate_ref[...] = gate8
    rank_ref[...] = rank8.astype(jnp.int32)


def _router(x1p, shift_p, scale_p, x1s, shift_s, scale_s, wts, *, tm, tiles_per_batch):
    tp, d = x1p.shape
    ts = x1s.shape[0]
    n_p, n_s = tp // tm, ts // tm
    t_all = tp + ts
    row = lambda w: pl.BlockSpec((tm, w), lambda i: (i, 0))
    p_row = pl.BlockSpec((tm, d), lambda i: (jnp.minimum(i, n_p - 1), 0))
    p_mod = pl.BlockSpec((None, 1, d), lambda i: (jnp.minimum(i, n_p - 1) // tiles_per_batch, 0, 0))
    s_row = pl.BlockSpec((tm, d), lambda i: (jnp.maximum(i - n_p, 0), 0))
    by_choice = pl.BlockSpec((ROW_GROUP, tm), lambda i: (0, i))
    consts = [wts["gffn"], wts["rw_hi"], wts["rw_lo"], wts["rb"]]
    return pl.pallas_call(
        functools.partial(_router_kernel, n_prompt_tiles=n_p),
        grid=(n_p + n_s,),
        in_specs=[p_row, p_mod, p_mod, s_row, s_row, s_row] + [_const_spec(c.shape) for c in consts],
        out_specs=[row(d), by_choice, by_choice, by_choice, _const_spec((N_EXPERTS, LANES))],
        out_shape=[jax.ShapeDtypeStruct((t_all, d), F32), jax.ShapeDtypeStruct((ROW_GROUP, t_all), jnp.int32),
                   jax.ShapeDtypeStruct((ROW_GROUP, t_all), F32), jax.ShapeDtypeStruct((ROW_GROUP, t_all), jnp.int32),
                   jax.ShapeDtypeStruct((N_EXPERTS, LANES), F32)],
        scratch_shapes=[pltpu.VMEM((N_EXPERTS, LANES), F32)],
        compiler_params=_params(("arbitrary",)),
        name="router",
    )(x1p, shift_p, scale_p, x1s, shift_s, scale_s, *consts)


ROW_GROUP = 8


def _slot_offset(slot, n_rows):
    return slot * n_rows if isinstance(slot, int) else pl.multiple_of(slot * n_rows, n_rows)


def _row_gather_start(idx_smem, slot, src_hbm, dst_vmem, sem, n_rows):
    base = _slot_offset(slot, n_rows)

    def group(g, carry):
        r0 = g * ROW_GROUP
        for j in range(ROW_GROUP):
            pltpu.make_async_copy(src_hbm.at[pl.ds(idx_smem[base + r0 + j], 1)], dst_vmem.at[r0 + j],
                                  sem).start(priority=j % 2)
        return carry
    lax.fori_loop(0, n_rows // ROW_GROUP, group, 0)


def _row_gather_wait(dst_vmem, sem):
    pltpu.make_async_copy(dst_vmem, dst_vmem, sem).wait()


def _gather_pipeline(i, n_steps, idx_hbm, idx_smem, isem, src_hbm, buf, gsem, n_rows, first_tile):
    def idx_copy(blk, slot):
        return pltpu.make_async_copy(idx_hbm.at[first_tile + blk, 0],
                                     idx_smem.at[pl.ds(_slot_offset(slot, n_rows), n_rows)], isem.at[slot])

    @pl.when(i == 0)
    def _():
        idx_copy(0, 0).start()
        idx_copy(0, 0).wait()
        _row_gather_start(idx_smem, 0, src_hbm, buf.at[0], gsem.at[0], n_rows)

        @pl.when(n_steps > 1)
        def _():
            idx_copy(1, 1).start()

    nxt = (i + 1) % 2

    @pl.when(i + 1 < n_steps)
    def _():
        idx_copy(i + 1, nxt).wait()
        _row_gather_start(idx_smem, nxt, src_hbm, buf.at[nxt], gsem.at[nxt], n_rows)

    @pl.when(i + 2 < n_steps)
    def _():
        idx_copy(i + 2, i % 2).start()

    _row_gather_wait(buf.at[i % 2], gsem.at[i % 2])


def _dispatch_kernel(pend_ref, nused_ref, dest_ref, h2_ref, xs_ref, zbuf, idx_smem, isem, csem, zsem, *,
                     tm, n_blocks):
    i = pl.program_id(0)
    n = pl.num_programs(0)
    rows = TOP_K * tm

    def idx_copy(blk, slot):
        return pltpu.make_async_copy(dest_ref.at[blk, 0], idx_smem.at[pl.ds(_slot_offset(slot, rows), rows)],
                                     isem.at[slot])

    def zero_copy(block_start):
        start = pl.multiple_of(block_start, MOE_ROWS)
        return pltpu.make_async_copy(zbuf, xs_ref.at[pl.ds(start, MOE_ROWS)], zsem)

    def rows_done():
        return pltpu.make_async_copy(xs_ref.at[pl.ds(0, rows)], xs_ref.at[pl.ds(0, rows)], csem)

    @pl.when(i == 0)
    def _():
        idx_copy(0, 0).start()
        zbuf[...] = jnp.zeros(zbuf.shape, F32)
        n_used = nused_ref[0]

        def last_block(e, carry):
            zero_copy(jnp.maximum(pend_ref[e] - MOE_ROWS, 0)).start()
            return carry
        lax.fori_loop(0, N_EXPERTS, last_block, 0)

        def tail_block(b, carry):
            zero_copy(b * MOE_ROWS).start()
            return carry
        lax.fori_loop(n_used, n_blocks, tail_block, 0)

        def drain(b, carry):
            zero_copy(0).wait()
            return carry
        lax.fori_loop(0, N_EXPERTS + n_blocks - n_used, drain, 0)

    slot = i % 2
    idx_copy(i, slot).wait()

    @pl.when(i + 1 < n)
    def _():
        idx_copy(i + 1, 1 - slot).start()

    base = _slot_offset(slot, rows)

    def group(g, carry):
        r0 = g * ROW_GROUP
        for j in range(ROW_GROUP):
            src = h2_ref.at[g, pl.ds(j, 1)]
            for kk in range(TOP_K):
                dst = xs_ref.at[pl.ds(idx_smem[base + kk * tm + r0 + j], 1)]
                pltpu.make_async_copy(src, dst, csem).start(priority=kk % 2)
        return carry
    lax.fori_loop(0, tm // ROW_GROUP, group, 0)
    rows_done().wait()


def _dispatch(pend, n_used, dest_tiles, h2_all, *, tm, n_blocks):
    t_all, d = h2_all.shape
    grid_spec = pltpu.PrefetchScalarGridSpec(
        num_scalar_prefetch=2,
        grid=(t_all // tm,),
        in_specs=[pl.BlockSpec(memory_space=pl.ANY),
                  pl.BlockSpec((tm // ROW_GROUP, ROW_GROUP, d), lambda i, pe, nu: (i, 0, 0))],
        out_specs=pl.BlockSpec(memory_space=pl.ANY),
        scratch_shapes=[pltpu.VMEM((MOE_ROWS, d), F32),
                        pltpu.SMEM((2 * TOP_K * tm,), jnp.int32),
                        pltpu.SemaphoreType.DMA((2,)),
                        pltpu.SemaphoreType.DMA(()),
                        pltpu.SemaphoreType.DMA(())],
    )
    return pl.pallas_call(
        functools.partial(_dispatch_kernel, tm=tm, n_blocks=n_blocks),
        grid_spec=grid_spec,
        out_shape=jax.ShapeDtypeStruct((n_blocks * MOE_ROWS, d), F32),
        compiler_params=_params(("arbitrary",)),
        name="dispatch",
    )(pend, n_used, dest_tiles, h2_all.reshape(t_all // ROW_GROUP, ROW_GROUP, d))


def _moe_kernel(be_ref, nused_ref, xs_ref, wgu_ref, bgu_ref, wdn_ref, bdn_ref, y_ref, wgu_bf, wdn_bf, *, d_model):
    i = pl.program_id(0)
    n_used = nused_ref[0]

    @pl.when(i < n_used)
    def _():
        prev = be_ref[jnp.maximum(i - 1, 0)]

        @pl.when((i == 0) | (be_ref[i] != prev))
        def _():
            wgu_bf[...] = wgu_ref[...].astype(BF16)
            wdn_bf[...] = wdn_ref[...].astype(BF16)

        xb = xs_ref[...].astype(BF16)
        gu = _dot(xb, wgu_bf[...]) + bgu_ref[...]
        g = jnp.minimum(gu[:, :d_model], SWIGLU_LIMIT)
        lin = jnp.clip(gu[:, d_model:], -SWIGLU_LIMIT, SWIGLU_LIMIT)
        act = g * jax.nn.sigmoid(SWIGLU_ALPHA * g) * (lin + 1.0)
        y_ref[...] = _dot(act.astype(BF16), wdn_bf[...]) + bdn_ref[...]

    @pl.when(i >= n_used)
    def _():
        y_ref[...] = jnp.zeros(y_ref.shape, F32)


def _moe_experts(block_e, n_used, xs, w_gu, b_gu, w_dn, b_dn):
    n_blocks = xs.shape[0] // MOE_ROWS
    e, d, d2 = w_gu.shape
    grid_spec = pltpu.PrefetchScalarGridSpec(
        num_scalar_prefetch=2,
        grid=(n_blocks,),
        in_specs=[pl.BlockSpec((MOE_ROWS, d), lambda i, be, nu: (jnp.minimum(i, nu[0] - 1), 0)),
                  pl.BlockSpec((None, d, d2), lambda i, be, nu: (be[i], 0, 0)),
                  pl.BlockSpec((None, 1, d2), lambda i, be, nu: (be[i], 0, 0)),
                  pl.BlockSpec((None, d, d), lambda i, be, nu: (be[i], 0, 0)),
                  pl.BlockSpec((None, 1, d), lambda i, be, nu: (be[i], 0, 0))],
        out_specs=pl.BlockSpec((MOE_ROWS, d), lambda i, be, nu: (i, 0)),
        scratch_shapes=[pltpu.VMEM((d, d2), BF16),
                        pltpu.VMEM((d, d), BF16)],
    )
    return pl.pallas_call(
        functools.partial(_moe_kernel, d_model=d),
        grid_spec=grid_spec,
        out_shape=jax.ShapeDtypeStruct((n_blocks * MOE_ROWS, d), F32),
        compiler_params=_params(("arbitrary",)),
        name="moe_experts",
    )(block_e, n_used, xs, w_gu, b_gu.reshape(e, 1, d2), w_dn, b_dn.reshape(e, 1, d))


def _combine_kernel(pos_ref, ys_ref, x1_ref, gate_ref, gm_ref, gfin_ref, y_ref, ybuf, idx_smem, isem, gsem, *,
                    tm, first_tile, final_norm):
    i = pl.program_id(0)
    _gather_pipeline(i, pl.num_programs(0), pos_ref, idx_smem, isem, ys_ref, ybuf, gsem, TOP_K * tm, first_tile)
    gate = jnp.transpose(jnp.concatenate([gate_ref[...], jnp.zeros((LANES - ROW_GROUP, tm), F32)], axis=0))
    f = jnp.zeros(x1_ref.shape, F32)
    for kk in range(TOP_K):
        f = f + gate[:, kk:kk + 1] * ybuf[i % 2, kk * tm:(kk + 1) * tm, 0, :]
    x2 = x1_ref[...] + gm_ref[...] * f
    y_ref[...] = _rms(x2, gfin_ref[...]) if final_norm else x2


def _combine(pos_tiles, ys, x1, gate, g_m, final_g, *, tm, first_tile, per_row, tiles_per_batch, final_norm):
    t, d = x1.shape
    row = lambda w: pl.BlockSpec((tm, w), lambda i: (i, 0))
    return pl.pallas_call(
        functools.partial(_combine_kernel, tm=tm, first_tile=first_tile, final_norm=final_norm),
        grid=(t // tm,),
        in_specs=[pl.BlockSpec(memory_space=pl.ANY), pl.BlockSpec(memory_space=pl.ANY), row(d),
                  pl.BlockSpec((ROW_GROUP, tm), lambda i: (0, first_tile + i)),
                  _mod_spec(per_row, tm, d, tiles_per_batch), _const_spec((1, d))],
        out_specs=row(d),
        out_shape=jax.ShapeDtypeStruct((t, d), F32),
        scratch_shapes=[pltpu.VMEM((2, TOP_K * tm, 1, d), F32),
                        pltpu.SMEM((2 * TOP_K * tm,), jnp.int32),
                        pltpu.SemaphoreType.DMA((2,)),
                        pltpu.SemaphoreType.DMA((2,))],
        compiler_params=_params(("arbitrary",)),
        name="combine",
    )(pos_tiles, ys, x1, gate, g_m, final_g)


def _rope_tables(pos):
    inv = ROPE_THETA ** (-jnp.arange(0, QK_ROPE, 2, dtype=F32) / QK_ROPE)
    ang = pos.astype(F32)[:, None] * inv[None, :]
    cos, sin = jnp.cos(ang), jnp.sin(ang)
    n = pos.shape[0]
    cc = jnp.concatenate([cos, cos, jnp.ones((n, LANES - QK_ROPE), F32)], axis=1)
    ss = jnp.concatenate([sin, sin, jnp.zeros((n, LANES - QK_ROPE), F32)], axis=1)
    return cc, ss


def _layer_weights(l, w_in, norm_mix_g, q_norm_g, w_uq, kv_norm_g, w_uk, w_uv, w_pa, sg_norm_g, sg_norm_b,
                   w_pb, w_o, norm_ffn_g, router_w, router_b):
    d = w_in.shape[1]
    wi = w_in[l]
    o_kr = Q_RANK + KV_RANK
    o_u = o_kr + QK_ROPE
    o_v = o_u + SG_WIDTH
    o_ga = o_v + SG_WIDTH
    kr = wi[:, o_kr:o_u]
    zpad = jnp.zeros((d, LANES - QK_ROPE), F32)
    w_in_r = jnp.concatenate([wi[:, :o_kr], wi[:, o_u:o_ga], wi[:, o_ga:], kr, zpad], axis=1).astype(BF16)
    uq = w_uq[l]
    nope, rope = uq[..., :QK_NOPE], uq[..., QK_NOPE:]
    z32 = jnp.zeros(rope.shape[:2] + (HEAD_SLOT - QK_NOPE - QK_ROPE,), F32)
    wq = jnp.concatenate([rope, nope, z32], axis=-1).reshape(Q_RANK, QK_WIDTH).astype(BF16)
    uk = w_uk[l]
    zk_lo = jnp.zeros(uk.shape[:2] + (QK_ROPE,), F32)
    zk_hi = jnp.zeros(uk.shape[:2] + (HEAD_SLOT - QK_NOPE - QK_ROPE,), F32)
    wk = jnp.concatenate([zk_lo, uk, zk_hi], axis=-1).reshape(KV_RANK, QK_WIDTH).astype(BF16)
    wv = w_uv[l].reshape(KV_RANK, V_WIDTH).astype(BF16)
    wv_slot = jnp.concatenate([w_uv[l], jnp.zeros_like(w_uv[l])], axis=-1).reshape(KV_RANK, QK_WIDTH).astype(BF16)
    vone = jnp.tile(jnp.concatenate([jnp.zeros((V_HEAD,), F32), jnp.ones((HEAD_SLOT - V_HEAD,), F32)]),
                    MLA_HEADS).reshape(1, QK_WIDTH)
    ukt = jnp.transpose(uk, (1, 2, 0))
    eye = jnp.broadcast_to(jnp.eye(QK_ROPE, LANES, dtype=F32), (MLA_HEADS, QK_ROPE, LANES))
    top = jnp.concatenate([jnp.zeros((MLA_HEADS, QK_ROPE, KV_RANK), F32), eye], axis=-1)
    mid = jnp.concatenate([ukt, jnp.zeros((MLA_HEADS, QK_NOPE, LANES), F32)], axis=-1)
    bot = jnp.zeros((MLA_HEADS, HEAD_SLOT - QK_NOPE - QK_ROPE, KV_RANK + LANES), F32)
    mabs = jnp.concatenate([top, mid, bot], axis=1).astype(BF16)
    rw = jnp.pad(router_w[l], ((0, 0), (0, ROUTER_PAD - N_EXPERTS)))
    rw_hi = rw.astype(BF16)
    rw_lo = (rw - rw_hi.astype(F32)).astype(BF16)
    rb = jnp.concatenate([router_b[l], jnp.full((ROUTER_PAD - N_EXPERTS,), NEG_BIG, F32)]).reshape(1, ROUTER_PAD)
    return dict(
        w_in_r=w_in_r, gmix=norm_mix_g[l].reshape(1, d),
        gq=(q_norm_g[l] * (ATTN_SCALE * LOG2_E)).reshape(1, Q_RANK),
        gkv=kv_norm_g[l].reshape(1, KV_RANK), wq=wq, wk=wk, wv=wv, wv_slot=wv_slot, vone=vone, mabs=mabs,
        sgg=sg_norm_g[l].reshape(1, SG_WIDTH), sgb=sg_norm_b[l].reshape(1, SG_WIDTH),
        w_pa=w_pa[l].astype(BF16), w_pb=w_pb[l].astype(BF16), w_o=w_o[l].astype(BF16),
        gffn=norm_ffn_g[l].reshape(1, d), rw_hi=rw_hi, rw_lo=rw_lo, rb=rb)


def _spatial_mix_weights(w_s, b_s, seq, n_batch):
    gw = SG_WIDTH // SG_GROUPS
    tril = jnp.tril(jnp.ones((SG_CHUNK, SG_CHUNK), dtype=bool))
    w = jnp.where(tril[None], w_s, 0.0)
    if seq % SG_CHUNK == 0:
        mixw = w
        bias_t = b_s
    else:
        assert seq < SG_CHUNK
        blk = w[:, :seq, :seq]
        pos = jnp.arange(n_batch * seq, dtype=jnp.int32)
        rep = (pos[:, None] % seq == jnp.arange(seq, dtype=jnp.int32)[None, :]).astype(F32)
        tiled = jnp.einsum("rt,gts,cs->grc", rep, blk, rep, precision=lax.Precision.HIGHEST)
        mixw = jnp.where((pos[:, None] // seq == pos[None, :] // seq)[None], tiled, 0.0)
        bias_t = jnp.tile(b_s[:, :seq], (1, n_batch))
    bias = jnp.repeat(jnp.transpose(bias_t), gw, axis=1)
    return mixw.astype(BF16), bias


def _routing_tables(idx, rank, counts_f, n_blocks):
    idx, rank = idx[:TOP_K], rank[:TOP_K]
    counts = counts_f[:, 0].astype(jnp.int32)
    padded = (counts + MOE_ROWS - 1) // MOE_ROWS * MOE_ROWS
    pend = jnp.cumsum(padded).astype(jnp.int32)
    pstart = pend - padded
    experts = jnp.arange(N_EXPERTS, dtype=jnp.int32)
    dest = rank + jnp.sum(jnp.where(idx[..., None] == experts, pstart, 0), axis=-1)
    block_start = jnp.arange(n_blocks, dtype=jnp.int32) * MOE_ROWS
    block_e = jnp.minimum(jnp.sum((pend[None, :] <= block_start[:, None]).astype(jnp.int32), axis=1), N_EXPERTS - 1)
    n_used = (pend[-1:] // MOE_ROWS).astype(jnp.int32)
    return dest.astype(jnp.int32), pend, block_e.astype(jnp.int32), n_used


def _pos_tiles(pos, tm):
    t = pos.shape[1]
    return jnp.transpose(pos.reshape(TOP_K, t // tm, tm), (1, 0, 2)).reshape(t // tm, 1, TOP_K * tm)


def _stack_layers(per_layer):
    return per_layer[0][None] if len(per_layer) == 1 else jnp.stack(per_layer)


def _pick_tile(n, pref):
    t = min(n, pref)
    assert n % t == 0 and t % 8 == 0
    return t


def kernel(x_prompt, x_sample, cache_ckv, cache_krope, c_prompt, c_sample, ada_w, ada_b, norm_mix_g, w_in, q_norm_g, w_uq, kv_norm_g, w_uk, w_uv, w_pa, sg_norm_g, sg_norm_b, w_spatial, b_spatial, w_pb, w_o, norm_ffn_g, router_w, router_b, w_gu, b_gu, w_dn, b_dn, final_g):
    bp, lp, d = x_prompt.shape
    bs, ls, _ = x_sample.shape
    depth = w_in.shape[0]
    past = cache_ckv.shape[2]
    tp, ts = bp * lp, bs * ls
    assert lp % SG_CHUNK == 0 and ls <= SG_CHUNK

    tm_p = _pick_tile(lp, INPROJ_TILE)
    tm_s = _pick_tile(ts, SAMPLE_TILE)
    t_attn = _pick_tile(lp, ATTN_TILE)
    tpb = lp // tm_p

    cc_p, ss_p = _rope_tables(jnp.arange(lp, dtype=jnp.int32))
    cc_s, ss_s = _rope_tables(past + jnp.arange(ls, dtype=jnp.int32))
    cc_s, ss_s = jnp.tile(cc_s, (bs, 1)), jnp.tile(ss_s, (bs, 1))

    b_all = bp + bs
    b_pad = -(-b_all // 8) * 8
    c_all = jnp.concatenate([c_prompt, c_sample, jnp.zeros((b_pad - b_all, d), F32)], axis=0)

    xp = x_prompt.reshape(tp, d)
    xs = x_sample.reshape(ts, d)
    outs = dict(ckv_p=[], kr_p=[], ckv_s=[], kr_s=[], v_s=[])
    final_g2 = final_g.reshape(1, d)
    for l in range(depth):
        wts = _layer_weights(l, w_in, norm_mix_g, q_norm_g, w_uq, kv_norm_g, w_uk, w_uv, w_pa, sg_norm_g,
                             sg_norm_b, w_pb, w_o, norm_ffn_g, router_w, router_b)
        mod = _adaln(c_all, ada_w[l], ada_b[l])
        mod_p = [mod[:bp, j * d:(j + 1) * d].reshape(bp, 1, d) for j in range(6)]
        mod_s = [jnp.repeat(mod[bp:b_all, j * d:(j + 1) * d], ls, axis=0) for j in range(6)]

        q, k, v, ckv, kr, u, vn, ga, gb = _inproj(xp, mod_p[0], mod_p[1], cc_p, ss_p, wts, tm=tm_p, per_row=False,
                                                  tiles_per_batch=tpb, vn_dtype=BF16)
        o = _attn_prompt(q.reshape(bp, lp, QK_WIDTH), k.reshape(bp, lp, QK_WIDTH), v.reshape(bp, lp, QK_WIDTH),
                         tq=t_attn, tk=t_attn).reshape(tp, V_WIDTH)
        mixw, bias = _spatial_mix_weights(w_spatial[l], b_spatial[l], lp, bp)
        tm_m = _pick_tile(lp, MERGE_TILE)
        x1p = _merge(o, u, vn, ga, gb, xp, mod_p[2], mixw, bias, wts, tm=tm_m, chunk=SG_CHUNK, per_row=False,
                     tiles_per_batch=lp // tm_m)
        outs["ckv_p"].append(ckv.reshape(bp, lp, KV_RANK))
        outs["kr_p"].append(jnp.transpose(kr, (0, 2, 1)))

        q, k, v, ckv, kr, u, vn, ga, gb = _inproj(xs, mod_s[0], mod_s[1], cc_s, ss_s, wts, tm=tm_s, per_row=True,
                                                  tiles_per_batch=1, vn_dtype=F32)
        ckv3 = ckv.reshape(bs, ls, KV_RANK)
        kr_rbl = kr.reshape(QK_ROPE, bs, ls)
        kr3 = jnp.transpose(kr_rbl, (1, 2, 0))
        o = _attn_sample(q.reshape(bs, ls, QK_WIDTH), cache_ckv[l], jnp.transpose(cache_krope[l], (0, 2, 1)),
                         ckv3, jnp.transpose(kr_rbl, (1, 0, 2)), wts["mabs"], wts["wv"]).reshape(ts, V_WIDTH)
        mixw, bias = _spatial_mix_weights(w_spatial[l], b_spatial[l], ls, tm_s // ls)
        x1s = _merge(o, u, vn, ga, gb, xs, mod_s[2], mixw, bias, wts, tm=tm_s, chunk=tm_s, per_row=True,
                     tiles_per_batch=1)
        outs["ckv_s"].append(ckv3)
        outs["kr_s"].append(kr3)
        outs["v_s"].append(vn.reshape(bs, ls, SG_WIDTH))

        t_all = tp + ts
        tm_r = _pick_tile(math.gcd(tp, ts), ROW_TILE)
        h2_all, idx, gate, rank, counts = _router(x1p, mod_p[3], mod_p[4], x1s, mod_s[3], mod_s[4], wts, tm=tm_r,
                                                  tiles_per_batch=lp // tm_r)
        n_blocks = -(-(t_all * TOP_K) // MOE_ROWS) + N_EXPERTS
        pos, pend, block_e, n_used = _routing_tables(idx, rank, counts, n_blocks)
        pos_tiles = _pos_tiles(pos, tm_r)
        x_sorted = _dispatch(pend, n_used, pos_tiles, h2_all, tm=tm_r, n_blocks=n_blocks)
        ys = _moe_experts(block_e, n_used, x_sorted, w_gu[l], b_gu[l], w_dn[l], b_dn[l])
        last = l == depth - 1
        xp = _combine(pos_tiles, ys, x1p, gate, mod_p[5], final_g2, tm=tm_r, first_tile=0, per_row=False,
                      tiles_per_batch=lp // tm_r, final_norm=last)
        xs = _combine(pos_tiles, ys, x1s, gate, mod_s[5], final_g2, tm=tm_r, first_tile=tp // tm_r, per_row=True,
                      tiles_per_batch=1, final_norm=last)
    return (xp.reshape(bp, lp, d), xs.reshape(bs, ls, d),
            _stack_layers(outs["ckv_p"]), _stack_layers(outs["kr_p"]),
            _stack_layers(outs["ckv_s"]), _stack_layers(outs["kr_s"]), _stack_layers(outs["v_s"]))
```

```python
import functools
import math

import jax
import jax.numpy as jnp
from jax import lax
from jax.experimental import pallas as pl
from jax.experimental.pallas import tpu as pltpu

F32 = jnp.float32
BF16 = jnp.bfloat16

LANES = 128
VMEM_LIMIT_BYTES = 56 * 1024 * 1024

CHUNK = 64
CHUNK_SHIFT = 6
MLA_HEADS = 8
QK_NOPE = 64
QK_ROPE = 32
V_HEAD = 64
V_HEAD_SHIFT = 6
Q_RANK = 384
KV_RANK = 256
ROPE_THETA = 10000.0
ATTN_SCALE = 1.0 / math.sqrt(QK_NOPE + QK_ROPE)
LOG2_E = math.log2(math.e)
SG_CHUNK = 128
SG_GROUPS = 4
SG_WIDTH = 512
N_EXPERTS = 32
TOP_K = 4
SWIGLU_LIMIT = 7.0
SWIGLU_ALPHA = 1.702
EPS = 1e-6

HEAD_SLOT = LANES
QK_WIDTH = MLA_HEADS * HEAD_SLOT
V_WIDTH = MLA_HEADS * V_HEAD
MOE_ROWS = 512
ROW_TILE = 512
INPROJ_TILE = 512
MERGE_TILE = 512
SAMPLE_TILE = 512
ATTN_TILE = 1024
ATTN_SUB_KEYS = 256
ROUTER_PAD = LANES
NEG_BIG = -1e30

_C_CQ = 0
_C_CKV = _C_CQ + Q_RANK
_C_U = _C_CKV + KV_RANK
_C_V = _C_U + SG_WIDTH
_C_GA = _C_V + SG_WIDTH


def _params(sem):
    return pltpu.CompilerParams(dimension_semantics=sem, vmem_limit_bytes=VMEM_LIMIT_BYTES)


def _dot(a, b):
    return jnp.dot(a, b, preferred_element_type=F32)


def _dot_nt(a, b):
    return lax.dot_general(a, b, (((1,), (1,)), ((), ())), preferred_element_type=F32)


def _rms(x, g):
    return x * lax.rsqrt(jnp.mean(x * x, axis=-1, keepdims=True) + EPS) * g


def _adaln_kernel(c_ref, w_ref, b_ref, o_ref):
    c = c_ref[...]
    s = (c * jax.nn.sigmoid(c)).astype(BF16)
    o_ref[...] = _dot(s, w_ref[...].astype(BF16)) + b_ref[...]


def _adaln(c_all, ada_w, ada_b):
    bp, d = c_all.shape
    n = ada_w.shape[1]
    return pl.pallas_call(
        _adaln_kernel,
        grid=(n // d,),
        in_specs=[pl.BlockSpec((bp, d), lambda j: (0, 0)),
                  pl.BlockSpec((d, d), lambda j: (0, j)),
                  pl.BlockSpec((1, d), lambda j: (0, j))],
        out_specs=pl.BlockSpec((bp, d), lambda j: (0, j)),
        out_shape=jax.ShapeDtypeStruct((bp, n), F32),
        compiler_params=_params(("arbitrary",)),
        name="adaln",
    )(c_all, ada_w, ada_b.reshape(1, n))


def _rope_slot(a, cc, ss):
    half = QK_ROPE // 2
    lane = lax.broadcasted_iota(jnp.int32, a.shape, 1)
    swapped = jnp.where(lane < half, -pltpu.roll(a, LANES - half, axis=1), pltpu.roll(a, half, axis=1))
    return a * cc + swapped * ss


def _inproj_kernel(x_ref, sh_ref, sc_ref, gmix_ref, cc_ref, ss_ref, win_ref, gq_ref, gkv_ref,
                   wq_ref, wk_ref, wv_ref, vone_ref, sgg_ref, sgb_ref,
                   q_ref, k_ref, v_ref, ckv_ref, kr_ref, u_ref, vn_ref, ga_ref, gb_ref, *, d_model):
    x = x_ref[...]
    h = (_rms(x, gmix_ref[...]) * (1.0 + sc_ref[...]) + sh_ref[...]).astype(BF16)

    def proj(lo, width):
        return _dot(h, win_ref[:, lo:lo + width])

    cc = cc_ref[...]
    ss = ss_ref[...]
    c_gb = _C_GA + d_model
    c_kr = c_gb + d_model

    cqn = _rms(proj(_C_CQ, Q_RANK), gq_ref[...]).astype(BF16)
    qa = _dot(cqn, wq_ref[...])
    for hd in range(MLA_HEADS):
        sl = slice(hd * HEAD_SLOT, (hd + 1) * HEAD_SLOT)
        q_ref[:, sl] = _rope_slot(qa[:, sl], cc, ss).astype(BF16)

    ckvn = _rms(proj(_C_CKV, KV_RANK), gkv_ref[...])
    ckv_ref[...] = ckvn
    ckvb = ckvn.astype(BF16)
    krs = _rope_slot(proj(c_kr, LANES), cc, ss)
    kr_ref[...] = jnp.transpose(krs)[:QK_ROPE]
    kn = _dot(ckvb, wk_ref[...])
    for hd in range(MLA_HEADS):
        sl = slice(hd * HEAD_SLOT, (hd + 1) * HEAD_SLOT)
        k_ref[:, sl] = (kn[:, sl] + krs).astype(BF16)
    v_ref[...] = (_dot(ckvb, wv_ref[...]) + vone_ref[...]).astype(BF16)

    u_ref[...] = proj(_C_U, SG_WIDTH).astype(u_ref.dtype)
    vv = proj(_C_V, SG_WIDTH)
    mu = jnp.mean(vv, axis=-1, keepdims=True)
    vc = vv - mu
    var = jnp.mean(vc * vc, axis=-1, keepdims=True)
    vn_ref[...] = (vc * lax.rsqrt(var + EPS) * sgg_ref[...] + sgb_ref[...]).astype(vn_ref.dtype)
    ga_ref[...] = proj(_C_GA, d_model).astype(BF16)
    gb_ref[...] = proj(c_gb, d_model).astype(BF16)


def _mod_spec(per_row, tm, d, tiles_per_batch):
    if per_row:
        return pl.BlockSpec((tm, d), lambda i: (i, 0))
    return pl.BlockSpec((None, 1, d), lambda i: (i // tiles_per_batch, 0, 0))


def _const_spec(shape):
    nd = len(shape)
    return pl.BlockSpec(shape, lambda i: (0,) * nd)


def _inproj(x2d, shift, scale, cc, ss, wts, *, tm, per_row, tiles_per_batch, vn_dtype):
    t, d = x2d.shape
    n_tab = cc.shape[0] // tm
    row = lambda w: pl.BlockSpec((tm, w), lambda i: (i, 0))
    tab = pl.BlockSpec((tm, LANES), lambda i: (i % n_tab, 0))
    mod = _mod_spec(per_row, tm, d, tiles_per_batch)
    consts = [wts["w_in_r"], wts["gq"], wts["gkv"], wts["wq"], wts["wk"], wts["wv_slot"],
              wts["vone"], wts["sgg"], wts["sgb"]]
    if per_row:
        kr_shape = jax.ShapeDtypeStruct((QK_ROPE, t), F32)
        kr_spec = pl.BlockSpec((QK_ROPE, tm), lambda i: (0, i))
    else:
        kr_shape = jax.ShapeDtypeStruct((t // (tiles_per_batch * tm), QK_ROPE, tiles_per_batch * tm), F32)
        kr_spec = pl.BlockSpec((None, QK_ROPE, tm), lambda i: (i // tiles_per_batch, 0, i % tiles_per_batch))
    out_shapes = [jax.ShapeDtypeStruct((t, QK_WIDTH), BF16), jax.ShapeDtypeStruct((t, QK_WIDTH), BF16),
                  jax.ShapeDtypeStruct((t, QK_WIDTH), BF16), jax.ShapeDtypeStruct((t, KV_RANK), F32),
                  kr_shape, jax.ShapeDtypeStruct((t, SG_WIDTH), BF16),
                  jax.ShapeDtypeStruct((t, SG_WIDTH), vn_dtype), jax.ShapeDtypeStruct((t, d), BF16),
                  jax.ShapeDtypeStruct((t, d), BF16)]
    out_specs = [row(s.shape[1]) for s in out_shapes]
    out_specs[4] = kr_spec
    return pl.pallas_call(
        functools.partial(_inproj_kernel, d_model=d),
        grid=(t // tm,),
        in_specs=[row(d), mod, mod, _const_spec((1, d)), tab, tab] + [_const_spec(c.shape) for c in consts],
        out_specs=out_specs,
        out_shape=out_shapes,
        compiler_params=_params(("arbitrary",)),
        name="inproj",
    )(x2d, shift, scale, wts["gmix"], cc, ss, *consts)


def _attn_kernel(qi_ref, kj_ref, flag_ref, q_ref, k_ref, v_ref, o_ref, m_sc, acc_sc, *, tq, tk, sub):
    s_id = pl.program_id(1)
    qi = qi_ref[s_id]
    kj = kj_ref[s_id]
    flags = flag_ref[s_id]

    @pl.when(kj == 0)
    def _():
        m_sc[...] = jnp.full(m_sc.shape, -jnp.inf, F32)
        acc_sc[...] = jnp.zeros(acc_sc.shape, F32)

    def sweep(bias):
        for kb in range(tk // sub):
            keys = slice(kb * sub, (kb + 1) * sub)
            rows = slice(kb * sub if bias is not None else 0, tq)
            for hd in range(MLA_HEADS):
                sl = slice(hd * HEAD_SLOT, (hd + 1) * HEAD_SLOT)
                s = _dot_nt(q_ref[rows, sl], k_ref[keys, sl])
                if bias is not None:
                    s = s + bias[rows, keys]
                tiles = [s[:, c * LANES:(c + 1) * LANES] for c in range(sub // LANES)]
                m_tile = tiles[0]
                for t in tiles[1:]:
                    m_tile = jnp.maximum(m_tile, t)
                m_old = m_sc[hd, rows]
                m_new = jnp.maximum(m_old, jnp.max(m_tile, axis=-1, keepdims=True))
                alpha = jnp.exp2(m_old - m_new)
                p = jnp.concatenate([jnp.exp2(t - m_new).astype(BF16) for t in tiles], axis=1)
                acc_sc[hd, rows] = alpha * acc_sc[hd, rows] + _dot(p, v_ref[keys, sl])
                m_sc[hd, rows] = m_new

    @pl.when((flags & 2) == 0)
    def _():
        sweep(None)

    @pl.when((flags & 2) != 0)
    def _():
        row = lax.broadcasted_iota(jnp.int32, (tq, tk), 0) + qi * tq
        col = lax.broadcasted_iota(jnp.int32, (tq, tk), 1) + kj * tk
        sweep(jnp.where((col >> CHUNK_SHIFT) <= (row >> CHUNK_SHIFT), 0.0, -jnp.inf))

    @pl.when((flags & 1) != 0)
    def _():
        lane = lax.broadcasted_iota(jnp.int32, (tq, LANES), 1)
        for pr in range(MLA_HEADS // 2):
            outs = []
            for hd in (2 * pr, 2 * pr + 1):
                acc = acc_sc[hd]
                outs.append(acc / pltpu.roll(acc, V_HEAD, axis=1))
            pair = jnp.where(lane < V_HEAD, outs[0], pltpu.roll(outs[1], V_HEAD, axis=1))
            o_ref[:, pr * LANES:(pr + 1) * LANES] = pair.astype(BF16)


def _attn_prompt(q, k, v, *, tq, tk):
    b, l, _ = q.shape
    assert tq == tk
    nq = l // tq
    qi_l, kj_l, flag_l = [], [], []
    for i in range(nq):
        n_kv = ((i + 1) * tq - 1) // tk + 1
        for j in range(n_kv):
            qi_l.append(i)
            kj_l.append(j)
            masked = ((j + 1) * tk - 1) // CHUNK > (i * tq) // CHUNK
            flag_l.append((1 if j == n_kv - 1 else 0) | (2 if masked else 0))
    steps = len(qi_l)
    grid_spec = pltpu.PrefetchScalarGridSpec(
        num_scalar_prefetch=3,
        grid=(b, steps),
        in_specs=[pl.BlockSpec((None, tq, QK_WIDTH), lambda bi, s, qi, kj, fl: (bi, qi[s], 0)),
                  pl.BlockSpec((None, tk, QK_WIDTH), lambda bi, s, qi, kj, fl: (bi, kj[s], 0)),
                  pl.BlockSpec((None, tk, QK_WIDTH), lambda bi, s, qi, kj, fl: (bi, kj[s], 0))],
        out_specs=pl.BlockSpec((None, tq, V_WIDTH), lambda bi, s, qi, kj, fl: (bi, qi[s], 0)),
        scratch_shapes=[pltpu.VMEM((MLA_HEADS, tq, LANES), F32), pltpu.VMEM((MLA_HEADS, tq, LANES), F32)],
    )
    return pl.pallas_call(
        functools.partial(_attn_kernel, tq=tq, tk=tk, sub=min(tk, ATTN_SUB_KEYS)),
        grid_spec=grid_spec,
        out_shape=jax.ShapeDtypeStruct((b, l, V_WIDTH), BF16),
        compiler_params=_params(("arbitrary", "arbitrary")),
        name="attn_prompt",
    )(jnp.asarray(qi_l, jnp.int32), jnp.asarray(kj_l, jnp.int32), jnp.asarray(flag_l, jnp.int32), q, k, v)


def _attn_sample_kernel(q_ref, pckv_ref, pkr_ref, nckv_ref, nkr_ref, mabs_ref, wv_ref, o_ref, *, ls, past):
    hl = MLA_HEADS * ls
    qcat = jnp.concatenate(
        [_dot(q_ref[:, hd * HEAD_SLOT:(hd + 1) * HEAD_SLOT], mabs_ref[hd]) for hd in range(MLA_HEADS)],
        axis=0).astype(BF16)
    q_abs = qcat[:, :KV_RANK]
    q_rope = qcat[:, KV_RANK:]
    pckv = pckv_ref[...].astype(BF16)
    nckv = nckv_ref[...].astype(BF16)

    def pad_rows(kr_t):
        return jnp.concatenate([kr_t, jnp.zeros((LANES - QK_ROPE, kr_t.shape[1]), kr_t.dtype)], axis=0).astype(BF16)

    s_past = _dot_nt(q_abs, pckv) + _dot(q_rope, pad_rows(pkr_ref[...]))
    s_new = _dot_nt(q_abs, nckv) + _dot(q_rope, pad_rows(nkr_ref[...]))

    qpos_1 = lax.broadcasted_iota(jnp.int32, (ls, 1), 0) + past
    qchunk = jnp.concatenate([qpos_1] * MLA_HEADS, axis=0) >> CHUNK_SHIFT
    kchunk_past = lax.broadcasted_iota(jnp.int32, (hl, past), 1) >> CHUNK_SHIFT
    kchunk_new = (lax.broadcasted_iota(jnp.int32, (hl, ls), 1) + past) >> CHUNK_SHIFT
    s_past = jnp.where(kchunk_past <= qchunk, s_past, -jnp.inf)
    s_new = jnp.where(kchunk_new <= qchunk, s_new, -jnp.inf)

    m = jnp.maximum(jnp.max(s_past, axis=-1, keepdims=True), jnp.max(s_new, axis=-1, keepdims=True))
    p_past = jnp.exp2(s_past - m)
    p_new = jnp.exp2(s_new - m)
    denom = jnp.sum(p_past, axis=-1, keepdims=True) + jnp.sum(p_new, axis=-1, keepdims=True)
    olat = (_dot(p_past.astype(BF16), pckv) + _dot(p_new.astype(BF16), nckv)) / denom
    ofull = _dot(olat.astype(BF16), wv_ref[...])
    col_head = lax.broadcasted_iota(jnp.int32, (ls, V_WIDTH), 1) >> V_HEAD_SHIFT
    out = jnp.zeros((ls, V_WIDTH), F32)
    for hd in range(MLA_HEADS):
        out = out + jnp.where(col_head == hd, ofull[hd * ls:(hd + 1) * ls], 0.0)
    o_ref[...] = out.astype(BF16)


def _attn_sample(q, past_ckv, past_kr_t, new_ckv, new_kr_t, mabs, wv):
    b, ls, _ = q.shape
    past = past_ckv.shape[1]
    blk = lambda n, w: pl.BlockSpec((None, n, w), lambda i: (i, 0, 0))
    return pl.pallas_call(
        functools.partial(_attn_sample_kernel, ls=ls, past=past),
        grid=(b,),
        in_specs=[blk(ls, QK_WIDTH), blk(past, KV_RANK), blk(QK_ROPE, past), blk(ls, KV_RANK), blk(QK_ROPE, ls),
                  _const_spec(mabs.shape), _const_spec(wv.shape)],
        out_specs=blk(ls, V_WIDTH),
        out_shape=jax.ShapeDtypeStruct((b, ls, V_WIDTH), BF16),
        compiler_params=_params(("arbitrary",)),
        name="attn_sample",
    )(q, past_ckv, past_kr_t, new_ckv, new_kr_t, mabs, wv)


def _merge_kernel(o_ref, u_ref, vn_ref, ga_ref, gb_ref, x_ref, gate_ref, mix_ref, bias_ref,
                  wpa_ref, wpb_ref, wo_ref, x1_ref, sg_sc, *, chunk):
    tm = x_ref.shape[0]
    gw = SG_WIDTH // SG_GROUPS
    for c in range(tm // chunk):
        rows = slice(c * chunk, (c + 1) * chunk)
        for g in range(SG_GROUPS):
            cols = slice(g * gw, (g + 1) * gw)
            mixed = _dot(mix_ref[g], vn_ref[rows, cols].astype(BF16)) + bias_ref[:, cols]
            sg_sc[rows, cols] = (u_ref[rows, cols].astype(F32) * mixed).astype(BF16)
    ya = _dot(o_ref[...], wpa_ref[...])
    yb = _dot(sg_sc[...], wpb_ref[...])
    m = jax.nn.sigmoid(ga_ref[...].astype(F32)) * ya + jax.nn.sigmoid(gb_ref[...].astype(F32)) * yb
    x1_ref[...] = x_ref[...] + gate_ref[...] * _dot(m.astype(BF16), wo_ref[...])


def _merge(o, u, vn, ga, gb, x2d, gate, mixw, bias, wts, *, tm, chunk, per_row, tiles_per_batch):
    t, d = x2d.shape
    row = lambda w: pl.BlockSpec((tm, w), lambda i: (i, 0))
    consts = [mixw, bias, wts["w_pa"], wts["w_pb"], wts["w_o"]]
    return pl.pallas_call(
        functools.partial(_merge_kernel, chunk=chunk),
        grid=(t // tm,),
        in_specs=[row(V_WIDTH), row(SG_WIDTH), row(SG_WIDTH), row(d), row(d), row(d),
                  _mod_spec(per_row, tm, d, tiles_per_batch)] + [_const_spec(c.shape) for c in consts],
        out_specs=row(d),
        out_shape=jax.ShapeDtypeStruct((t, d), F32),
        scratch_shapes=[pltpu.VMEM((tm, SG_WIDTH), BF16)],
        compiler_params=_params(("arbitrary",)),
        name="merge",
    )(o, u, vn, ga, gb, x2d, gate, *consts)


def _router_kernel(xp_ref, shp_ref, scp_ref, xs_ref, shs_ref, scs_ref, g_ref, whi_ref, wlo_ref, rb_ref,
                   h2_ref, idx_ref, gate_ref, rank_ref, cnt_ref, carry_sc, *, n_prompt_tiles):
    i = pl.program_id(0)
    out_refs = (g_ref, whi_ref, wlo_ref, rb_ref, h2_ref, idx_ref, gate_ref, rank_ref, cnt_ref, carry_sc)

    @pl.when(i == 0)
    def _():
        carry_sc[...] = jnp.zeros(carry_sc.shape, F32)

    @pl.when(i < n_prompt_tiles)
    def _():
        _route_rows(xp_ref, shp_ref, scp_ref, *out_refs)

    @pl.when(i >= n_prompt_tiles)
    def _():
        _route_rows(xs_ref, shs_ref, scs_ref, *out_refs)


def _route_rows(x1_ref, sh_ref, sc_ref, g_ref, whi_ref, wlo_ref, rb_ref, h2_ref, idx_ref, gate_ref, rank_ref,
                cnt_ref, carry_sc):
    h2 = _rms(x1_ref[...], g_ref[...]) * (1.0 + sc_ref[...]) + sh_ref[...]
    h2_ref[...] = h2
    hi = h2.astype(BF16)
    lo = (h2 - hi.astype(F32)).astype(BF16)
    logits = _dot(hi, whi_ref[...]) + _dot(lo, whi_ref[...]) + _dot(hi, wlo_ref[...]) + rb_ref[...]
    tm = logits.shape[0]
    work = jnp.transpose(logits)[:N_EXPERTS]
    expert = lax.broadcasted_iota(jnp.int32, work.shape, 0)
    vals, idxs = [], []
    for _ in range(TOP_K):
        mx = jnp.max(work, axis=0, keepdims=True)
        ix = jnp.min(jnp.where(work == mx, expert, N_EXPERTS), axis=0, keepdims=True)
        vals.append(mx)
        idxs.append(ix)
        work = jnp.where(expert == ix, -jnp.inf, work)
    es = [jnp.exp(v - vals[0]) for v in vals]
    tot = es[0]
    for e in es[1:]:
        tot = tot + e

    onehot = jnp.zeros(work.shape, F32)
    for j in range(TOP_K):
        onehot = jnp.where(expert == idxs[j], 1.0, onehot)
    earlier = (lax.broadcasted_iota(jnp.int32, (tm, tm), 0) < lax.broadcasted_iota(jnp.int32, (tm, tm), 1))
    within = _dot(onehot.astype(BF16), jnp.where(earlier, 1.0, 0.0).astype(BF16))
    carry = carry_sc[...]
    rank_full = within + (jnp.tile(carry, (1, tm // LANES)) if tm >= LANES else carry[:, :tm])
    ranks = [jnp.sum(jnp.where(expert == idxs[j], rank_full, 0.0), axis=0, keepdims=True) for j in range(TOP_K)]
    carry_sc[...] = carry_sc[...] + jnp.sum(onehot, axis=1, keepdims=True)
    cnt_ref[...] = carry_sc[...]

    row = lax.broadcasted_iota(jnp.int32, (ROW_GROUP, tm), 0)
    idx8 = jnp.zeros((ROW_GROUP, tm), jnp.int32)
    gate8 = jnp.zeros((ROW_GROUP, tm), F32)
    rank8 = jnp.zeros((ROW_GROUP, tm), F32)
    for j in range(TOP_K):
        idx8 = jnp.where(row == j, idxs[j], idx8)
        gate8 = jnp.where(row == j, es[j] / tot, gate8)
        rank8 = jnp.where(row == j, ranks[j], rank8)
    idx_ref[...] = idx8
    gate_ref[...] = gate8
    rank_ref[...] = rank8.astype(jnp.int32)


def _router(x1p, shift_p, scale_p, x1s, shift_s, scale_s, wts, *, tm, tiles_per_batch):
    tp, d = x1p.shape
    ts = x1s.shape[0]
    n_p, n_s = tp // tm, ts // tm
    t_all = tp + ts
    row = lambda w: pl.BlockSpec((tm, w), lambda i: (i, 0))
    p_row = pl.BlockSpec((tm, d), lambda i: (jnp.minimum(i, n_p - 1), 0))
    p_mod = pl.BlockSpec((None, 1, d), lambda i: (jnp.minimum(i, n_p - 1) // tiles_per_batch, 0, 0))
    s_row = pl.BlockSpec((tm, d), lambda i: (jnp.maximum(i - n_p, 0), 0))
    by_choice = pl.BlockSpec((ROW_GROUP, tm), lambda i: (0, i))
    consts = [wts["gffn"], wts["rw_hi"], wts["rw_lo"], wts["rb"]]
    return pl.pallas_call(
        functools.partial(_router_kernel, n_prompt_tiles=n_p),
        grid=(n_p + n_s,),
        in_specs=[p_row, p_mod, p_mod, s_row, s_row, s_row] + [_const_spec(c.shape) for c in consts],
        out_specs=[row(d), by_choice, by_choice, by_choice, _const_spec((N_EXPERTS, LANES))],
        out_shape=[jax.ShapeDtypeStruct((t_all, d), F32), jax.ShapeDtypeStruct((ROW_GROUP, t_all), jnp.int32),
                   jax.ShapeDtypeStruct((ROW_GROUP, t_all), F32), jax.ShapeDtypeStruct((ROW_GROUP, t_all), jnp.int32),
                   jax.ShapeDtypeStruct((N_EXPERTS, LANES), F32)],
        scratch_shapes=[pltpu.VMEM((N_EXPERTS, LANES), F32)],
        compiler_params=_params(("arbitrary",)),
        name="router",
    )(x1p, shift_p, scale_p, x1s, shift_s, scale_s, *consts)


ROW_GROUP = 8


def _slot_offset(slot, n_rows):
    return slot * n_rows if isinstance(slot, int) else pl.multiple_of(slot * n_rows, n_rows)


def _row_gather_start(idx_smem, slot, src_hbm, dst_vmem, sem, n_rows):
    base = _slot_offset(slot, n_rows)

    def group(g, carry):
        r0 = g * ROW_GROUP
        for j in range(ROW_GROUP):
            pltpu.make_async_copy(src_hbm.at[pl.ds(idx_smem[base + r0 + j], 1)], dst_vmem.at[r0 + j],
                                  sem).start(priority=j % 2)
        return carry
    lax.fori_loop(0, n_rows // ROW_GROUP, group, 0)


def _row_gather_wait(dst_vmem, sem):
    pltpu.make_async_copy(dst_vmem, dst_vmem, sem).wait()


def _gather_pipeline(i, n_steps, idx_hbm, idx_smem, isem, src_hbm, buf, gsem, n_rows, first_tile):
    def idx_copy(blk, slot):
        return pltpu.make_async_copy(idx_hbm.at[first_tile + blk, 0],
                                     idx_smem.at[pl.ds(_slot_offset(slot, n_rows), n_rows)], isem.at[slot])

    @pl.when(i == 0)
    def _():
        idx_copy(0, 0).start()
        idx_copy(0, 0).wait()
        _row_gather_start(idx_smem, 0, src_hbm, buf.at[0], gsem.at[0], n_rows)

        @pl.when(n_steps > 1)
        def _():
            idx_copy(1, 1).start()

    nxt = (i + 1) % 2

    @pl.when(i + 1 < n_steps)
    def _():
        idx_copy(i + 1, nxt).wait()
        _row_gather_start(idx_smem, nxt, src_hbm, buf.at[nxt], gsem.at[nxt], n_rows)

    @pl.when(i + 2 < n_steps)
    def _():
        idx_copy(i + 2, i % 2).start()

    _row_gather_wait(buf.at[i % 2], gsem.at[i % 2])


def _dispatch_kernel(pend_ref, nused_ref, dest_ref, h2_ref, xs_ref, zbuf, tbuf, idx_smem, isem, tsem, csem, zsem, *,
                     tm, n_blocks):
    i = pl.program_id(0)
    n = pl.num_programs(0)
    rows = TOP_K * tm
    groups = tm // ROW_GROUP

    def idx_copy(blk, slot):
        return pltpu.make_async_copy(dest_ref.at[blk, 0], idx_smem.at[pl.ds(_slot_offset(slot, rows), rows)],
                                     isem.at[slot])

    def tile_copy(t, ring):
        return pltpu.make_async_copy(h2_ref.at[pl.ds(t * groups, groups)], tbuf.at[ring], tsem.at[ring])

    def zero_copy(block_start):
        start = pl.multiple_of(block_start, MOE_ROWS)
        return pltpu.make_async_copy(zbuf, xs_ref.at[pl.ds(start, MOE_ROWS)], zsem)

    def rows_done(parity):
        return pltpu.make_async_copy(xs_ref.at[pl.ds(0, rows)], xs_ref.at[pl.ds(0, rows)], csem.at[parity])

    @pl.when(i == 0)
    def _():
        idx_copy(0, 0).start()
        tile_copy(0, 0).start()

        @pl.when(n > 1)
        def _():
            tile_copy(1, 1).start()

        zbuf[...] = jnp.zeros(zbuf.shape, F32)
        n_used = nused_ref[0]

        def last_block(e, carry):
            zero_copy(jnp.maximum(pend_ref[e] - MOE_ROWS, 0)).start()
            return carry
        lax.fori_loop(0, N_EXPERTS, last_block, 0)

        def tail_block(b, carry):
            zero_copy(b * MOE_ROWS).start()
            return carry
        lax.fori_loop(n_used, n_blocks, tail_block, 0)

        def drain(b, carry):
            zero_copy(0).wait()
            return carry
        lax.fori_loop(0, N_EXPERTS + n_blocks - n_used, drain, 0)

    slot = i % 2
    idx_copy(i, slot).wait()

    @pl.when(i + 1 < n)
    def _():
        idx_copy(i + 1, 1 - slot).start()

    base = _slot_offset(slot, rows)
    ring = i % 3
    tile_copy(i, ring).wait()

    def group(g, carry):
        r0 = g * ROW_GROUP
        for j in range(ROW_GROUP):
            src = tbuf.at[ring, g, pl.ds(j, 1)]
            for kk in range(TOP_K):
                dst = xs_ref.at[pl.ds(idx_smem[base + kk * tm + r0 + j], 1)]
                pltpu.make_async_copy(src, dst, csem.at[slot]).start(priority=kk % 2)
        return carry
    lax.fori_loop(0, groups, group, 0)

    @pl.when(i >= 1)
    def _():
        rows_done(1 - slot).wait()

    @pl.when(i + 2 < n)
    def _():
        tile_copy(i + 2, (i + 2) % 3).start()

    @pl.when(i == n - 1)
    def _():
        rows_done(slot).wait()


def _dispatch(pend, n_used, dest_tiles, h2_all, *, tm, n_blocks):
    t_all, d = h2_all.shape
    grid_spec = pltpu.PrefetchScalarGridSpec(
        num_scalar_prefetch=2,
        grid=(t_all // tm,),
        in_specs=[pl.BlockSpec(memory_space=pl.ANY), pl.BlockSpec(memory_space=pl.ANY)],
        out_specs=pl.BlockSpec(memory_space=pl.ANY),
        scratch_shapes=[pltpu.VMEM((MOE_ROWS, d), F32),
                        pltpu.VMEM((3, tm // ROW_GROUP, ROW_GROUP, d), F32),
                        pltpu.SMEM((2 * TOP_K * tm,), jnp.int32),
                        pltpu.SemaphoreType.DMA((2,)),
                        pltpu.SemaphoreType.DMA((3,)),
                        pltpu.SemaphoreType.DMA((2,)),
                        pltpu.SemaphoreType.DMA(())],
    )
    return pl.pallas_call(
        functools.partial(_dispatch_kernel, tm=tm, n_blocks=n_blocks),
        grid_spec=grid_spec,
        out_shape=jax.ShapeDtypeStruct((n_blocks * MOE_ROWS, d), F32),
        compiler_params=_params(("arbitrary",)),
        name="dispatch",
    )(pend, n_used, dest_tiles, h2_all.reshape(t_all // ROW_GROUP, ROW_GROUP, d))


def _moe_kernel(be_ref, nused_ref, xs_ref, wgu_ref, bgu_ref, wdn_ref, bdn_ref, y_ref, wgu_bf, wdn_bf, *, d_model):
    i = pl.program_id(0)
    n_used = nused_ref[0]

    @pl.when(i < n_used)
    def _():
        prev = be_ref[jnp.maximum(i - 1, 0)]

        @pl.when((i == 0) | (be_ref[i] != prev))
        def _():
            wgu_bf[...] = wgu_ref[...].astype(BF16)
            wdn_bf[...] = wdn_ref[...].astype(BF16)

        xb = xs_ref[...].astype(BF16)
        gu = _dot(xb, wgu_bf[...]) + bgu_ref[...]
        g = jnp.minimum(gu[:, :d_model], SWIGLU_LIMIT)
        lin = jnp.clip(gu[:, d_model:], -SWIGLU_LIMIT, SWIGLU_LIMIT)
        act = g * jax.nn.sigmoid(SWIGLU_ALPHA * g) * (lin + 1.0)
        y_ref[...] = _dot(act.astype(BF16), wdn_bf[...]) + bdn_ref[...]

    @pl.when(i >= n_used)
    def _():
        y_ref[...] = jnp.zeros(y_ref.shape, F32)


def _moe_experts(block_e, n_used, xs, w_gu, b_gu, w_dn, b_dn):
    n_blocks = xs.shape[0] // MOE_ROWS
    e, d, d2 = w_gu.shape
    grid_spec = pltpu.PrefetchScalarGridSpec(
        num_scalar_prefetch=2,
        grid=(n_blocks,),
        in_specs=[pl.BlockSpec((MOE_ROWS, d), lambda i, be, nu: (jnp.minimum(i, nu[0] - 1), 0)),
                  pl.BlockSpec((None, d, d2), lambda i, be, nu: (be[i], 0, 0)),
                  pl.BlockSpec((None, 1, d2), lambda i, be, nu: (be[i], 0, 0)),
                  pl.BlockSpec((None, d, d), lambda i, be, nu: (be[i], 0, 0)),
                  pl.BlockSpec((None, 1, d), lambda i, be, nu: (be[i], 0, 0))],
        out_specs=pl.BlockSpec((MOE_ROWS, d), lambda i, be, nu: (i, 0)),
        scratch_shapes=[pltpu.VMEM((d, d2), BF16),
                        pltpu.VMEM((d, d), BF16)],
    )
    return pl.pallas_call(
        functools.partial(_moe_kernel, d_model=d),
        grid_spec=grid_spec,
        out_shape=jax.ShapeDtypeStruct((n_blocks * MOE_ROWS, d), F32),
        compiler_params=_params(("arbitrary",)),
        name="moe_experts",
    )(block_e, n_used, xs, w_gu, b_gu.reshape(e, 1, d2), w_dn, b_dn.reshape(e, 1, d))


def _combine_kernel(pos_ref, ys_ref, x1_ref, gate_ref, gm_ref, gfin_ref, y_ref, ybuf, idx_smem, isem, gsem, *,
                    tm, first_tile, final_norm):
    i = pl.program_id(0)
    _gather_pipeline(i, pl.num_programs(0), pos_ref, idx_smem, isem, ys_ref, ybuf, gsem, TOP_K * tm, first_tile)
    gate = jnp.transpose(jnp.concatenate([gate_ref[...], jnp.zeros((LANES - ROW_GROUP, tm), F32)], axis=0))
    f = jnp.zeros(x1_ref.shape, F32)
    for kk in range(TOP_K):
        f = f + gate[:, kk:kk + 1] * ybuf[i % 2, kk * tm:(kk + 1) * tm, 0, :]
    x2 = x1_ref[...] + gm_ref[...] * f
    y_ref[...] = _rms(x2, gfin_ref[...]) if final_norm else x2


def _combine(pos_tiles, ys, x1, gate, g_m, final_g, *, tm, first_tile, per_row, tiles_per_batch, final_norm):
    t, d = x1.shape
    row = lambda w: pl.BlockSpec((tm, w), lambda i: (i, 0))
    return pl.pallas_call(
        functools.partial(_combine_kernel, tm=tm, first_tile=first_tile, final_norm=final_norm),
        grid=(t // tm,),
        in_specs=[pl.BlockSpec(memory_space=pl.ANY), pl.BlockSpec(memory_space=pl.ANY), row(d),
                  pl.BlockSpec((ROW_GROUP, tm), lambda i: (0, first_tile + i)),
                  _mod_spec(per_row, tm, d, tiles_per_batch), _const_spec((1, d))],
        out_specs=row(d),
        out_shape=jax.ShapeDtypeStruct((t, d), F32),
        scratch_shapes=[pltpu.VMEM((2, TOP_K * tm, 1, d), F32),
                        pltpu.SMEM((2 * TOP_K * tm,), jnp.int32),
                        pltpu.SemaphoreType.DMA((2,)),
                        pltpu.SemaphoreType.DMA((2,))],
        compiler_params=_params(("arbitrary",)),
        name="combine",
    )(pos_tiles, ys, x1, gate, g_m, final_g)


def _rope_tables(pos):
    inv = ROPE_THETA ** (-jnp.arange(0, QK_ROPE, 2, dtype=F32) / QK_ROPE)
    ang = pos.astype(F32)[:, None] * inv[None, :]
    cos, sin = jnp.cos(ang), jnp.sin(ang)
    n = pos.shape[0]
    cc = jnp.concatenate([cos, cos, jnp.ones((n, LANES - QK_ROPE), F32)], axis=1)
    ss = jnp.concatenate([sin, sin, jnp.zeros((n, LANES - QK_ROPE), F32)], axis=1)
    return cc, ss


def _layer_weights(l, w_in, norm_mix_g, q_norm_g, w_uq, kv_norm_g, w_uk, w_uv, w_pa, sg_norm_g, sg_norm_b,
                   w_pb, w_o, norm_ffn_g, router_w, router_b):
    d = w_in.shape[1]
    wi = w_in[l]
    o_kr = Q_RANK + KV_RANK
    o_u = o_kr + QK_ROPE
    o_v = o_u + SG_WIDTH
    o_ga = o_v + SG_WIDTH
    kr = wi[:, o_kr:o_u]
    zpad = jnp.zeros((d, LANES - QK_ROPE), F32)
    w_in_r = jnp.concatenate([wi[:, :o_kr], wi[:, o_u:o_ga], wi[:, o_ga:], kr, zpad], axis=1).astype(BF16)
    uq = w_uq[l]
    nope, rope = uq[..., :QK_NOPE], uq[..., QK_NOPE:]
    z32 = jnp.zeros(rope.shape[:2] + (HEAD_SLOT - QK_NOPE - QK_ROPE,), F32)
    wq = jnp.concatenate([rope, nope, z32], axis=-1).reshape(Q_RANK, QK_WIDTH).astype(BF16)
    uk = w_uk[l]
    zk_lo = jnp.zeros(uk.shape[:2] + (QK_ROPE,), F32)
    zk_hi = jnp.zeros(uk.shape[:2] + (HEAD_SLOT - QK_NOPE - QK_ROPE,), F32)
    wk = jnp.concatenate([zk_lo, uk, zk_hi], axis=-1).reshape(KV_RANK, QK_WIDTH).astype(BF16)
    wv = w_uv[l].reshape(KV_RANK, V_WIDTH).astype(BF16)
    wv_slot = jnp.concatenate([w_uv[l], jnp.zeros_like(w_uv[l])], axis=-1).reshape(KV_RANK, QK_WIDTH).astype(BF16)
    vone = jnp.tile(jnp.concatenate([jnp.zeros((V_HEAD,), F32), jnp.ones((HEAD_SLOT - V_HEAD,), F32)]),
                    MLA_HEADS).reshape(1, QK_WIDTH)
    ukt = jnp.transpose(uk, (1, 2, 0))
    eye = jnp.broadcast_to(jnp.eye(QK_ROPE, LANES, dtype=F32), (MLA_HEADS, QK_ROPE, LANES))
    top = jnp.concatenate([jnp.zeros((MLA_HEADS, QK_ROPE, KV_RANK), F32), eye], axis=-1)
    mid = jnp.concatenate([ukt, jnp.zeros((MLA_HEADS, QK_NOPE, LANES), F32)], axis=-1)
    bot = jnp.zeros((MLA_HEADS, HEAD_SLOT - QK_NOPE - QK_ROPE, KV_RANK + LANES), F32)
    mabs = jnp.concatenate([top, mid, bot], axis=1).astype(BF16)
    rw = jnp.pad(router_w[l], ((0, 0), (0, ROUTER_PAD - N_EXPERTS)))
    rw_hi = rw.astype(BF16)
    rw_lo = (rw - rw_hi.astype(F32)).astype(BF16)
    rb = jnp.concatenate([router_b[l], jnp.full((ROUTER_PAD - N_EXPERTS,), NEG_BIG, F32)]).reshape(1, ROUTER_PAD)
    return dict(
        w_in_r=w_in_r, gmix=norm_mix_g[l].reshape(1, d),
        gq=(q_norm_g[l] * (ATTN_SCALE * LOG2_E)).reshape(1, Q_RANK),
        gkv=kv_norm_g[l].reshape(1, KV_RANK), wq=wq, wk=wk, wv=wv, wv_slot=wv_slot, vone=vone, mabs=mabs,
        sgg=sg_norm_g[l].reshape(1, SG_WIDTH), sgb=sg_norm_b[l].reshape(1, SG_WIDTH),
        w_pa=w_pa[l].astype(BF16), w_pb=w_pb[l].astype(BF16), w_o=w_o[l].astype(BF16),
        gffn=norm_ffn_g[l].reshape(1, d), rw_hi=rw_hi, rw_lo=rw_lo, rb=rb)


def _spatial_mix_weights(w_s, b_s, seq, n_batch):
    gw = SG_WIDTH // SG_GROUPS
    tril = jnp.tril(jnp.ones((SG_CHUNK, SG_CHUNK), dtype=bool))
    w = jnp.where(tril[None], w_s, 0.0)
    if seq % SG_CHUNK == 0:
        mixw = w
        bias_t = b_s
    else:
        assert seq < SG_CHUNK
        blk = w[:, :seq, :seq]
        pos = jnp.arange(n_batch * seq, dtype=jnp.int32)
        rep = (pos[:, None] % seq == jnp.arange(seq, dtype=jnp.int32)[None, :]).astype(F32)
        tiled = jnp.einsum("rt,gts,cs->grc", rep, blk, rep, precision=lax.Precision.HIGHEST)
        mixw = jnp.where((pos[:, None] // seq == pos[None, :] // seq)[None], tiled, 0.0)
        bias_t = jnp.tile(b_s[:, :seq], (1, n_batch))
    bias = jnp.repeat(jnp.transpose(bias_t), gw, axis=1)
    return mixw.astype(BF16), bias


def _routing_tables(idx, rank, counts_f, n_blocks):
    idx, rank = idx[:TOP_K], rank[:TOP_K]
    counts = counts_f[:, 0].astype(jnp.int32)
    padded = (counts + MOE_ROWS - 1) // MOE_ROWS * MOE_ROWS
    pend = jnp.cumsum(padded).astype(jnp.int32)
    pstart = pend - padded
    experts = jnp.arange(N_EXPERTS, dtype=jnp.int32)
    dest = rank + jnp.sum(jnp.where(idx[..., None] == experts, pstart, 0), axis=-1)
    block_start = jnp.arange(n_blocks, dtype=jnp.int32) * MOE_ROWS
    block_e = jnp.minimum(jnp.sum((pend[None, :] <= block_start[:, None]).astype(jnp.int32), axis=1), N_EXPERTS - 1)
    n_used = (pend[-1:] // MOE_ROWS).astype(jnp.int32)
    return dest.astype(jnp.int32), pend, block_e.astype(jnp.int32), n_used


def _pos_tiles(pos, tm):
    t = pos.shape[1]
    return jnp.transpose(pos.reshape(TOP_K, t // tm, tm), (1, 0, 2)).reshape(t // tm, 1, TOP_K * tm)


def _stack_layers(per_layer):
    return per_layer[0][None] if len(per_layer) == 1 else jnp.stack(per_layer)


def _pick_tile(n, pref):
    t = min(n, pref)
    assert n % t == 0 and t % 8 == 0
    return t


def kernel(x_prompt, x_sample, cache_ckv, cache_krope, c_prompt, c_sample, ada_w, ada_b, norm_mix_g, w_in, q_norm_g, w_uq, kv_norm_g, w_uk, w_uv, w_pa, sg_norm_g, sg_norm_b, w_spatial, b_spatial, w_pb, w_o, norm_ffn_g, router_w, router_b, w_gu, b_gu, w_dn, b_dn, final_g):
    bp, lp, d = x_prompt.shape
    bs, ls, _ = x_sample.shape
    depth = w_in.shape[0]
    past = cache_ckv.shape[2]
    tp, ts = bp * lp, bs * ls
    assert lp % SG_CHUNK == 0 and ls <= SG_CHUNK

    tm_p = _pick_tile(lp, INPROJ_TILE)
    tm_s = _pick_tile(ts, SAMPLE_TILE)
    t_attn = _pick_tile(lp, ATTN_TILE)
    tpb = lp // tm_p

    cc_p, ss_p = _rope_tables(jnp.arange(lp, dtype=jnp.int32))
    cc_s, ss_s = _rope_tables(past + jnp.arange(ls, dtype=jnp.int32))
    cc_s, ss_s = jnp.tile(cc_s, (bs, 1)), jnp.tile(ss_s, (bs, 1))

    b_all = bp + bs
    b_pad = -(-b_all // 8) * 8
    c_all = jnp.concatenate([c_prompt, c_sample, jnp.zeros((b_pad - b_all, d), F32)], axis=0)

    xp = x_prompt.reshape(tp, d)
    xs = x_sample.reshape(ts, d)
    outs = dict(ckv_p=[], kr_p=[], ckv_s=[], kr_s=[], v_s=[])
    final_g2 = final_g.reshape(1, d)
    for l in range(depth):
        wts = _layer_weights(l, w_in, norm_mix_g, q_norm_g, w_uq, kv_norm_g, w_uk, w_uv, w_pa, sg_norm_g,
                             sg_norm_b, w_pb, w_o, norm_ffn_g, router_w, router_b)
        mod = _adaln(c_all, ada_w[l], ada_b[l])
        mod_p = [mod[:bp, j * d:(j + 1) * d].reshape(bp, 1, d) for j in range(6)]
        mod_s = [jnp.repeat(mod[bp:b_all, j * d:(j + 1) * d], ls, axis=0) for j in range(6)]

        q, k, v, ckv, kr, u, vn, ga, gb = _inproj(xp, mod_p[0], mod_p[1], cc_p, ss_p, wts, tm=tm_p, per_row=False,
                                                  tiles_per_batch=tpb, vn_dtype=BF16)
        o = _attn_prompt(q.reshape(bp, lp, QK_WIDTH), k.reshape(bp, lp, QK_WIDTH), v.reshape(bp, lp, QK_WIDTH),
                         tq=t_attn, tk=t_attn).reshape(tp, V_WIDTH)
        mixw, bias = _spatial_mix_weights(w_spatial[l], b_spatial[l], lp, bp)
        tm_m = _pick_tile(lp, MERGE_TILE)
        x1p = _merge(o, u, vn, ga, gb, xp, mod_p[2], mixw, bias, wts, tm=tm_m, chunk=SG_CHUNK, per_row=False,
                     tiles_per_batch=lp // tm_m)
        outs["ckv_p"].append(ckv.reshape(bp, lp, KV_RANK))
        outs["kr_p"].append(jnp.transpose(kr, (0, 2, 1)))

        q, k, v, ckv, kr, u, vn, ga, gb = _inproj(xs, mod_s[0], mod_s[1], cc_s, ss_s, wts, tm=tm_s, per_row=True,
                                                  tiles_per_batch=1, vn_dtype=F32)
        ckv3 = ckv.reshape(bs, ls, KV_RANK)
        kr_rbl = kr.reshape(QK_ROPE, bs, ls)
        kr3 = jnp.transpose(kr_rbl, (1, 2, 0))
        o = _attn_sample(q.reshape(bs, ls, QK_WIDTH), cache_ckv[l], jnp.transpose(cache_krope[l], (0, 2, 1)),
                         ckv3, jnp.transpose(kr_rbl, (1, 0, 2)), wts["mabs"], wts["wv"]).reshape(ts, V_WIDTH)
        mixw, bias = _spatial_mix_weights(w_spatial[l], b_spatial[l], ls, tm_s // ls)
        x1s = _merge(o, u, vn, ga, gb, xs, mod_s[2], mixw, bias, wts, tm=tm_s, chunk=tm_s, per_row=True,
                     tiles_per_batch=1)
        outs["ckv_s"].append(ckv3)
        outs["kr_s"].append(kr3)
        outs["v_s"].append(vn.reshape(bs, ls, SG_WIDTH))

        t_all = tp + ts
        tm_r = _pick_tile(math.gcd(tp, ts), ROW_TILE)
        h2_all, idx, gate, rank, counts = _router(x1p, mod_p[3], mod_p[4], x1s, mod_s[3], mod_s[4], wts, tm=tm_r,
                                                  tiles_per_batch=lp // tm_r)
        n_blocks = -(-(t_all * TOP_K) // MOE_ROWS) + N_EXPERTS
        pos, pend, block_e, n_used = _routing_tables(idx, rank, counts, n_blocks)
        pos_tiles = _pos_tiles(pos, tm_r)
        x_sorted = _dispatch(pend, n_used, pos_tiles, h2_all, tm=tm_r, n_blocks=n_blocks)
        ys = _moe_experts(block_e, n_used, x_sorted, w_gu[l], b_gu[l], w_dn[l], b_dn[l])
        last = l == depth - 1
        xp = _combine(pos_tiles, ys, x1p, gate, mod_p[5], final_g2, tm=tm_r, first_tile=0, per_row=False,
                      tiles_per_batch=lp // tm_r, final_norm=last)
        xs = _combine(pos_tiles, ys, x1s, gate, mod_s[5], final_g2, tm=tm_r, first_tile=tp // tm_r, per_row=True,
                      tiles_per_batch=1, final_norm=last)
    return (xp.reshape(bp, lp, d), xs.reshape(bs, ls, d),
            _stack_layers(outs["ckv_p"]), _stack_layers(outs["kr_p"]),
            _stack_layers(outs["ckv_s"]), _stack_layers(outs["kr_s"]), _stack_layers(outs["v_s"]))
```

```python
import functools
import math

import jax
import jax.numpy as jnp
from jax import lax
from jax.experimental import pallas as pl
from jax.experimental.pallas import tpu as pltpu

F32 = jnp.float32
BF16 = jnp.bfloat16

LANES = 128
VMEM_LIMIT_BYTES = 56 * 1024 * 1024

CHUNK = 64
CHUNK_SHIFT = 6
MLA_HEADS = 8
QK_NOPE = 64
QK_ROPE = 32
V_HEAD = 64
V_HEAD_SHIFT = 6
Q_RANK = 384
KV_RANK = 256
ROPE_THETA = 10000.0
ATTN_SCALE = 1.0 / math.sqrt(QK_NOPE + QK_ROPE)
LOG2_E = math.log2(math.e)
SG_CHUNK = 128
SG_GROUPS = 4
SG_WIDTH = 512
N_EXPERTS = 32
TOP_K = 4
SWIGLU_LIMIT = 7.0
SWIGLU_ALPHA = 1.702
EPS = 1e-6

HEAD_SLOT = LANES
QK_WIDTH = MLA_HEADS * HEAD_SLOT
V_WIDTH = MLA_HEADS * V_HEAD
MOE_ROWS = 512
ROW_TILE = 512
INPROJ_TILE = 512
MERGE_TILE = 512
SAMPLE_TILE = 512
ATTN_TILE = 1024
ATTN_SUB_KEYS = 256
ROUTER_PAD = LANES
NEG_BIG = -1e30

_C_CQ = 0
_C_CKV = _C_CQ + Q_RANK
_C_U = _C_CKV + KV_RANK
_C_V = _C_U + SG_WIDTH
_C_GA = _C_V + SG_WIDTH


def _params(sem):
    return pltpu.CompilerParams(dimension_semantics=sem, vmem_limit_bytes=VMEM_LIMIT_BYTES)


def _dot(a, b):
    return jnp.dot(a, b, preferred_element_type=F32)


def _dot_nt(a, b):
    return lax.dot_general(a, b, (((1,), (1,)), ((), ())), preferred_element_type=F32)


def _rms(x, g):
    return x * lax.rsqrt(jnp.mean(x * x, axis=-1, keepdims=True) + EPS) * g


def _adaln_kernel(c_ref, w_ref, b_ref, o_ref):
    c = c_ref[...]
    s = (c * jax.nn.sigmoid(c)).astype(BF16)
    o_ref[...] = _dot(s, w_ref[...].astype(BF16)) + b_ref[...]


def _adaln(c_all, ada_w, ada_b):
    bp, d = c_all.shape
    n = ada_w.shape[1]
    return pl.pallas_call(
        _adaln_kernel,
        grid=(n // d,),
        in_specs=[pl.BlockSpec((bp, d), lambda j: (0, 0)),
                  pl.BlockSpec((d, d), lambda j: (0, j)),
                  pl.BlockSpec((1, d), lambda j: (0, j))],
        out_specs=pl.BlockSpec((bp, d), lambda j: (0, j)),
        out_shape=jax.ShapeDtypeStruct((bp, n), F32),
        compiler_params=_params(("arbitrary",)),
        name="adaln",
    )(c_all, ada_w, ada_b.reshape(1, n))


def _rope_slot(a, cc, ss):
    half = QK_ROPE // 2
    lane = lax.broadcasted_iota(jnp.int32, a.shape, 1)
    swapped = jnp.where(lane < half, -pltpu.roll(a, LANES - half, axis=1), pltpu.roll(a, half, axis=1))
    return a * cc + swapped * ss


def _inproj_kernel(x_ref, sh_ref, sc_ref, gmix_ref, cc_ref, ss_ref, win_ref, gq_ref, gkv_ref,
                   wq_ref, wk_ref, wv_ref, vone_ref, sgg_ref, sgb_ref,
                   q_ref, k_ref, v_ref, ckv_ref, kr_ref, u_ref, vn_ref, ga_ref, gb_ref, *, d_model):
    x = x_ref[...]
    h = (_rms(x, gmix_ref[...]) * (1.0 + sc_ref[...]) + sh_ref[...]).astype(BF16)

    def proj(lo, width):
        return _dot(h, win_ref[:, lo:lo + width])

    cc = cc_ref[...]
    ss = ss_ref[...]
    c_gb = _C_GA + d_model
    c_kr = c_gb + d_model

    cqn = _rms(proj(_C_CQ, Q_RANK), gq_ref[...]).astype(BF16)
    qa = _dot(cqn, wq_ref[...])
    for hd in range(MLA_HEADS):
        sl = slice(hd * HEAD_SLOT, (hd + 1) * HEAD_SLOT)
        q_ref[:, sl] = _rope_slot(qa[:, sl], cc, ss).astype(BF16)

    ckvn = _rms(proj(_C_CKV, KV_RANK), gkv_ref[...])
    ckv_ref[...] = ckvn
    ckvb = ckvn.astype(BF16)
    krs = _rope_slot(proj(c_kr, LANES), cc, ss)
    kr_ref[...] = jnp.transpose(krs)[:QK_ROPE]
    kn = _dot(ckvb, wk_ref[...])
    for hd in range(MLA_HEADS):
        sl = slice(hd * HEAD_SLOT, (hd + 1) * HEAD_SLOT)
        k_ref[:, sl] = (kn[:, sl] + krs).astype(BF16)
    v_ref[...] = (_dot(ckvb, wv_ref[...]) + vone_ref[...]).astype(BF16)

    u_ref[...] = proj(_C_U, SG_WIDTH).astype(u_ref.dtype)
    vv = proj(_C_V, SG_WIDTH)
    mu = jnp.mean(vv, axis=-1, keepdims=True)
    vc = vv - mu
    var = jnp.mean(vc * vc, axis=-1, keepdims=True)
    vn_ref[...] = (vc * lax.rsqrt(var + EPS) * sgg_ref[...] + sgb_ref[...]).astype(vn_ref.dtype)
    ga_ref[...] = proj(_C_GA, d_model).astype(BF16)
    gb_ref[...] = proj(c_gb, d_model).astype(BF16)


def _mod_spec(per_row, tm, d, tiles_per_batch):
    if per_row:
        return pl.BlockSpec((tm, d), lambda i: (i, 0))
    return pl.BlockSpec((None, 1, d), lambda i: (i // tiles_per_batch, 0, 0))


def _const_spec(shape):
    nd = len(shape)
    return pl.BlockSpec(shape, lambda i: (0,) * nd)


def _inproj(x2d, shift, scale, cc, ss, wts, *, tm, per_row, tiles_per_batch, vn_dtype):
    t, d = x2d.shape
    n_tab = cc.shape[0] // tm
    row = lambda w: pl.BlockSpec((tm, w), lambda i: (i, 0))
    tab = pl.BlockSpec((tm, LANES), lambda i: (i % n_tab, 0))
    mod = _mod_spec(per_row, tm, d, tiles_per_batch)
    consts = [wts["w_in_r"], wts["gq"], wts["gkv"], wts["wq"], wts["wk"], wts["wv_slot"],
              wts["vone"], wts["sgg"], wts["sgb"]]
    if per_row:
        kr_shape = jax.ShapeDtypeStruct((QK_ROPE, t), F32)
        kr_spec = pl.BlockSpec((QK_ROPE, tm), lambda i: (0, i))
    else:
        kr_shape = jax.ShapeDtypeStruct((t // (tiles_per_batch * tm), QK_ROPE, tiles_per_batch * tm), F32)
        kr_spec = pl.BlockSpec((None, QK_ROPE, tm), lambda i: (i // tiles_per_batch, 0, i % tiles_per_batch))
    out_shapes = [jax.ShapeDtypeStruct((t, QK_WIDTH), BF16), jax.ShapeDtypeStruct((t, QK_WIDTH), BF16),
                  jax.ShapeDtypeStruct((t, QK_WIDTH), BF16), jax.ShapeDtypeStruct((t, KV_RANK), F32),
                  kr_shape, jax.ShapeDtypeStruct((t, SG_WIDTH), BF16),
                  jax.ShapeDtypeStruct((t, SG_WIDTH), vn_dtype), jax.ShapeDtypeStruct((t, d), BF16),
                  jax.ShapeDtypeStruct((t, d), BF16)]
    out_specs = [row(s.shape[1]) for s in out_shapes]
    out_specs[4] = kr_spec
    return pl.pallas_call(
        functools.partial(_inproj_kernel, d_model=d),
        grid=(t // tm,),
        in_specs=[row(d), mod, mod, _const_spec((1, d)), tab, tab] + [_const_spec(c.shape) for c in consts],
        out_specs=out_specs,
        out_shape=out_shapes,
        compiler_params=_params(("arbitrary",)),
        name="inproj",
    )(x2d, shift, scale, wts["gmix"], cc, ss, *consts)


def _attn_kernel(qi_ref, kj_ref, flag_ref, q_ref, k_ref, v_ref, o_ref, m_sc, acc_sc, *, tq, tk, sub):
    s_id = pl.program_id(1)
    qi = qi_ref[s_id]
    kj = kj_ref[s_id]
    flags = flag_ref[s_id]

    @pl.when(kj == 0)
    def _():
        m_sc[...] = jnp.full(m_sc.shape, -jnp.inf, F32)
        acc_sc[...] = jnp.zeros(acc_sc.shape, F32)

    def sweep(bias):
        for kb in range(tk // sub):
            keys = slice(kb * sub, (kb + 1) * sub)
            rows = slice(kb * sub if bias is not None else 0, tq)
            for hd in range(MLA_HEADS):
                sl = slice(hd * HEAD_SLOT, (hd + 1) * HEAD_SLOT)
                s = _dot_nt(q_ref[rows, sl], k_ref[keys, sl])
                if bias is not None:
                    s = s + bias[rows, keys]
                tiles = [s[:, c * LANES:(c + 1) * LANES] for c in range(sub // LANES)]
                m_tile = tiles[0]
                for t in tiles[1:]:
                    m_tile = jnp.maximum(m_tile, t)
                m_old = m_sc[hd, rows]
                m_new = jnp.maximum(m_old, jnp.max(m_tile, axis=-1, keepdims=True))
                alpha = jnp.exp2(m_old - m_new)
                p = jnp.concatenate([jnp.exp2(t - m_new).astype(BF16) for t in tiles], axis=1)
                acc_sc[hd, rows] = alpha * acc_sc[hd, rows] + _dot(p, v_ref[keys, sl])
                m_sc[hd, rows] = m_new

    @pl.when((flags & 2) == 0)
    def _():
        sweep(None)

    @pl.when((flags & 2) != 0)
    def _():
        row = lax.broadcasted_iota(jnp.int32, (tq, tk), 0) + qi * tq
        col = lax.broadcasted_iota(jnp.int32, (tq, tk), 1) + kj * tk
        sweep(jnp.where((col >> CHUNK_SHIFT) <= (row >> CHUNK_SHIFT), 0.0, -jnp.inf))

    @pl.when((flags & 1) != 0)
    def _():
        lane = lax.broadcasted_iota(jnp.int32, (tq, LANES), 1)
        for pr in range(MLA_HEADS // 2):
            outs = []
            for hd in (2 * pr, 2 * pr + 1):
                acc = acc_sc[hd]
                outs.append(acc / pltpu.roll(acc, V_HEAD, axis=1))
            pair = jnp.where(lane < V_HEAD, outs[0], pltpu.roll(outs[1], V_HEAD, axis=1))
            o_ref[:, pr * LANES:(pr + 1) * LANES] = pair.astype(BF16)


def _attn_prompt(q, k, v, *, tq, tk):
    b, l, _ = q.shape
    assert tq == tk
    nq = l // tq
    qi_l, kj_l, flag_l = [], [], []
    for i in range(nq):
        n_kv = ((i + 1) * tq - 1) // tk + 1
        for j in range(n_kv):
            qi_l.append(i)
            kj_l.append(j)
            masked = ((j + 1) * tk - 1) // CHUNK > (i * tq) // CHUNK
            flag_l.append((1 if j == n_kv - 1 else 0) | (2 if masked else 0))
    steps = len(qi_l)
    grid_spec = pltpu.PrefetchScalarGridSpec(
        num_scalar_prefetch=3,
        grid=(b, steps),
        in_specs=[pl.BlockSpec((None, tq, QK_WIDTH), lambda bi, s, qi, kj, fl: (bi, qi[s], 0)),
                  pl.BlockSpec((None, tk, QK_WIDTH), lambda bi, s, qi, kj, fl: (bi, kj[s], 0)),
                  pl.BlockSpec((None, tk, QK_WIDTH), lambda bi, s, qi, kj, fl: (bi, kj[s], 0))],
        out_specs=pl.BlockSpec((None, tq, V_WIDTH), lambda bi, s, qi, kj, fl: (bi, qi[s], 0)),
        scratch_shapes=[pltpu.VMEM((MLA_HEADS, tq, LANES), F32), pltpu.VMEM((MLA_HEADS, tq, LANES), F32)],
    )
    return pl.pallas_call(
        functools.partial(_attn_kernel, tq=tq, tk=tk, sub=min(tk, ATTN_SUB_KEYS)),
        grid_spec=grid_spec,
        out_shape=jax.ShapeDtypeStruct((b, l, V_WIDTH), BF16),
        compiler_params=_params(("arbitrary", "arbitrary")),
        name="attn_prompt",
    )(jnp.asarray(qi_l, jnp.int32), jnp.asarray(kj_l, jnp.int32), jnp.asarray(flag_l, jnp.int32), q, k, v)


def _attn_sample_kernel(q_ref, pckv_ref, pkr_ref, nckv_ref, nkr_ref, mabs_ref, wv_ref, o_ref, *, ls, past):
    hl = MLA_HEADS * ls
    qcat = jnp.concatenate(
        [_dot(q_ref[:, hd * HEAD_SLOT:(hd + 1) * HEAD_SLOT], mabs_ref[hd]) for hd in range(MLA_HEADS)],
        axis=0).astype(BF16)
    q_abs = qcat[:, :KV_RANK]
    q_rope = qcat[:, KV_RANK:]
    pckv = pckv_ref[...].astype(BF16)
    nckv = nckv_ref[...].astype(BF16)

    def pad_rows(kr_t):
        return jnp.concatenate([kr_t, jnp.zeros((LANES - QK_ROPE, kr_t.shape[1]), kr_t.dtype)], axis=0).astype(BF16)

    s_past = _dot_nt(q_abs, pckv) + _dot(q_rope, pad_rows(pkr_ref[...]))
    s_new = _dot_nt(q_abs, nckv) + _dot(q_rope, pad_rows(nkr_ref[...]))

    qpos_1 = lax.broadcasted_iota(jnp.int32, (ls, 1), 0) + past
    qchunk = jnp.concatenate([qpos_1] * MLA_HEADS, axis=0) >> CHUNK_SHIFT
    kchunk_past = lax.broadcasted_iota(jnp.int32, (hl, past), 1) >> CHUNK_SHIFT
    kchunk_new = (lax.broadcasted_iota(jnp.int32, (hl, ls), 1) + past) >> CHUNK_SHIFT
    s_past = jnp.where(kchunk_past <= qchunk, s_past, -jnp.inf)
    s_new = jnp.where(kchunk_new <= qchunk, s_new, -jnp.inf)

    m = jnp.maximum(jnp.max(s_past, axis=-1, keepdims=True), jnp.max(s_new, axis=-1, keepdims=True))
    p_past = jnp.exp2(s_past - m)
    p_new = jnp.exp2(s_new - m)
    denom = jnp.sum(p_past, axis=-1, keepdims=True) + jnp.sum(p_new, axis=-1, keepdims=True)
    olat = (_dot(p_past.astype(BF16), pckv) + _dot(p_new.astype(BF16), nckv)) / denom
    ofull = _dot(olat.astype(BF16), wv_ref[...])
    col_head = lax.broadcasted_iota(jnp.int32, (ls, V_WIDTH), 1) >> V_HEAD_SHIFT
    out = jnp.zeros((ls, V_WIDTH), F32)
    for hd in range(MLA_HEADS):
        out = out + jnp.where(col_head == hd, ofull[hd * ls:(hd + 1) * ls], 0.0)
    o_ref[...] = out.astype(BF16)


def _attn_sample(q, past_ckv, past_kr_t, new_ckv, new_kr_t, mabs, wv):
    b, ls, _ = q.shape
    past = past_ckv.shape[1]
    blk = lambda n, w: pl.BlockSpec((None, n, w), lambda i: (i, 0, 0))
    return pl.pallas_call(
        functools.partial(_attn_sample_kernel, ls=ls, past=past),
        grid=(b,),
        in_specs=[blk(ls, QK_WIDTH), blk(past, KV_RANK), blk(QK_ROPE, past), blk(ls, KV_RANK), blk(QK_ROPE, ls),
                  _const_spec(mabs.shape), _const_spec(wv.shape)],
        out_specs=blk(ls, V_WIDTH),
        out_shape=jax.ShapeDtypeStruct((b, ls, V_WIDTH), BF16),
        compiler_params=_params(("arbitrary",)),
        name="attn_sample",
    )(q, past_ckv, past_kr_t, new_ckv, new_kr_t, mabs, wv)


def _merge_kernel(o_ref, u_ref, vn_ref, ga_ref, gb_ref, x_ref, gate_ref, mix_ref, bias_ref,
                  wpa_ref, wpb_ref, wo_ref, x1_ref, sg_sc, *, chunk):
    tm = x_ref.shape[0]
    gw = SG_WIDTH // SG_GROUPS
    for c in range(tm // chunk):
        rows = slice(c * chunk, (c + 1) * chunk)
        for g in range(SG_GROUPS):
            cols = slice(g * gw, (g + 1) * gw)
            mixed = _dot(mix_ref[g], vn_ref[rows, cols].astype(BF16)) + bias_ref[:, cols]
            sg_sc[rows, cols] = (u_ref[rows, cols].astype(F32) * mixed).astype(BF16)
    ya = _dot(o_ref[...], wpa_ref[...])
    yb = _dot(sg_sc[...], wpb_ref[...])
    m = jax.nn.sigmoid(ga_ref[...].astype(F32)) * ya + jax.nn.sigmoid(gb_ref[...].astype(F32)) * yb
    x1_ref[...] = x_ref[...] + gate_ref[...] * _dot(m.astype(BF16), wo_ref[...])


def _merge(o, u, vn, ga, gb, x2d, gate, mixw, bias, wts, *, tm, chunk, per_row, tiles_per_batch):
    t, d = x2d.shape
    row = lambda w: pl.BlockSpec((tm, w), lambda i: (i, 0))
    consts = [mixw, bias, wts["w_pa"], wts["w_pb"], wts["w_o"]]
    return pl.pallas_call(
        functools.partial(_merge_kernel, chunk=chunk),
        grid=(t // tm,),
        in_specs=[row(V_WIDTH), row(SG_WIDTH), row(SG_WIDTH), row(d), row(d), row(d),
                  _mod_spec(per_row, tm, d, tiles_per_batch)] + [_const_spec(c.shape) for c in consts],
        out_specs=row(d),
        out_shape=jax.ShapeDtypeStruct((t, d), F32),
        scratch_shapes=[pltpu.VMEM((tm, SG_WIDTH), BF16)],
        compiler_params=_params(("arbitrary",)),
        name="merge",
    )(o, u, vn, ga, gb, x2d, gate, *consts)


def _router_kernel(xp_ref, shp_ref, scp_ref, xs_ref, shs_ref, scs_ref, g_ref, whi_ref, wlo_ref, rb_ref,
                   h2_ref, idx_ref, gate_ref, rank_ref, cnt_ref, carry_sc, *, n_prompt_tiles):
    i = pl.program_id(0)
    out_refs = (g_ref, whi_ref, wlo_ref, rb_ref, h2_ref, idx_ref, gate_ref, rank_ref, cnt_ref, carry_sc)

    @pl.when(i == 0)
    def _():
        carry_sc[...] = jnp.zeros(carry_sc.shape, F32)

    @pl.when(i < n_prompt_tiles)
    def _():
        _route_rows(xp_ref, shp_ref, scp_ref, *out_refs)

    @pl.when(i >= n_prompt_tiles)
    def _():
        _route_rows(xs_ref, shs_ref, scs_ref, *out_refs)


def _route_rows(x1_ref, sh_ref, sc_ref, g_ref, whi_ref, wlo_ref, rb_ref, h2_ref, idx_ref, gate_ref, rank_ref,
                cnt_ref, carry_sc):
    h2 = _rms(x1_ref[...], g_ref[...]) * (1.0 + sc_ref[...]) + sh_ref[...]
    h2_ref[...] = h2
    hi = h2.astype(BF16)
    lo = (h2 - hi.astype(F32)).astype(BF16)
    logits = _dot(hi, whi_ref[...]) + _dot(lo, whi_ref[...]) + _dot(hi, wlo_ref[...]) + rb_ref[...]
    tm = logits.shape[0]
    work = jnp.transpose(logits)[:N_EXPERTS]
    expert = lax.broadcasted_iota(jnp.int32, work.shape, 0)
    vals, idxs = [], []
    for _ in range(TOP_K):
        mx = jnp.max(work, axis=0, keepdims=True)
        ix = jnp.min(jnp.where(work == mx, expert, N_EXPERTS), axis=0, keepdims=True)
        vals.append(mx)
        idxs.append(ix)
        work = jnp.where(expert == ix, -jnp.inf, work)
    es = [jnp.exp(v - vals[0]) for v in vals]
    tot = es[0]
    for e in es[1:]:
        tot = tot + e

    onehot = jnp.zeros(work.shape, F32)
    for j in range(TOP_K):
        onehot = jnp.where(expert == idxs[j], 1.0, onehot)
    earlier = (lax.broadcasted_iota(jnp.int32, (tm, tm), 0) < lax.broadcasted_iota(jnp.int32, (tm, tm), 1))
    within = _dot(onehot.astype(BF16), jnp.where(earlier, 1.0, 0.0).astype(BF16))
    carry = carry_sc[...]
    rank_full = within + (jnp.tile(carry, (1, tm // LANES)) if tm >= LANES else carry[:, :tm])
    ranks = [jnp.sum(jnp.where(expert == idxs[j], rank_full, 0.0), axis=0, keepdims=True) for j in range(TOP_K)]
    carry_sc[...] = carry_sc[...] + jnp.sum(onehot, axis=1, keepdims=True)
    cnt_ref[...] = carry_sc[...]

    row = lax.broadcasted_iota(jnp.int32, (ROW_GROUP, tm), 0)
    idx8 = jnp.zeros((ROW_GROUP, tm), jnp.int32)
    gate8 = jnp.zeros((ROW_GROUP, tm), F32)
    rank8 = jnp.zeros((ROW_GROUP, tm), F32)
    for j in range(TOP_K):
        idx8 = jnp.where(row == j, idxs[j], idx8)
        gate8 = jnp.where(row == j, es[j] / tot, gate8)
        rank8 = jnp.where(row == j, ranks[j], rank8)
    idx_ref[...] = idx8
    gate_ref[...] = gate8
    rank_ref[...] = rank8.astype(jnp.int32)


def _router(x1p, shift_p, scale_p, x1s, shift_s, scale_s, wts, *, tm, tiles_per_batch):
    tp, d = x1p.shape
    ts = x1s.shape[0]
    n_p, n_s = tp // tm, ts // tm
    t_all = tp + ts
    row = lambda w: pl.BlockSpec((tm, w), lambda i: (i, 0))
    p_row = pl.BlockSpec((tm, d), lambda i: (jnp.minimum(i, n_p - 1), 0))
    p_mod = pl.BlockSpec((None, 1, d), lambda i: (jnp.minimum(i, n_p - 1) // tiles_per_batch, 0, 0))
    s_row = pl.BlockSpec((tm, d), lambda i: (jnp.maximum(i - n_p, 0), 0))
    by_choice = pl.BlockSpec((ROW_GROUP, tm), lambda i: (0, i))
    consts = [wts["gffn"], wts["rw_hi"], wts["rw_lo"], wts["rb"]]
    return pl.pallas_call(
        functools.partial(_router_kernel, n_prompt_tiles=n_p),
        grid=(n_p + n_s,),
        in_specs=[p_row, p_mod, p_mod, s_row, s_row, s_row] + [_const_spec(c.shape) for c in consts],
        out_specs=[row(d), by_choice, by_choice, by_choice, _const_spec((N_EXPERTS, LANES))],
        out_shape=[jax.ShapeDtypeStruct((t_all, d), F32), jax.ShapeDtypeStruct((ROW_GROUP, t_all), jnp.int32),
                   jax.ShapeDtypeStruct((ROW_GROUP, t_all), F32), jax.ShapeDtypeStruct((ROW_GROUP, t_all), jnp.int32),
                   jax.ShapeDtypeStruct((N_EXPERTS, LANES), F32)],
        scratch_shapes=[pltpu.VMEM((N_EXPERTS, LANES), F32)],
        compiler_params=_params(("arbitrary",)),
        name="router",
    )(x1p, shift_p, scale_p, x1s, shift_s, scale_s, *consts)


ROW_GROUP = 8


def _slot_offset(slot, n_rows):
    return slot * n_rows if isinstance(slot, int) else pl.multiple_of(slot * n_rows, n_rows)


def _row_gather_start(idx_smem, slot, src_hbm, dst_vmem, sem, n_rows):
    base = _slot_offset(slot, n_rows)

    def group(g, carry):
        r0 = g * ROW_GROUP
        for j in range(ROW_GROUP):
            pltpu.make_async_copy(src_hbm.at[pl.ds(idx_smem[base + r0 + j], 1)], dst_vmem.at[r0 + j],
                                  sem).start(priority=j % 2)
        return carry
    lax.fori_loop(0, n_rows // ROW_GROUP, group, 0)


def _row_gather_wait(dst_vmem, sem):
    pltpu.make_async_copy(dst_vmem, dst_vmem, sem).wait()


def _gather_pipeline(i, n_steps, idx_hbm, idx_smem, isem, src_hbm, buf, gsem, n_rows, first_tile):
    def idx_copy(blk, slot):
        return pltpu.make_async_copy(idx_hbm.at[first_tile + blk, 0],
                                     idx_smem.at[pl.ds(_slot_offset(slot, n_rows), n_rows)], isem.at[slot])

    @pl.when(i == 0)
    def _():
        idx_copy(0, 0).start()
        idx_copy(0, 0).wait()
        _row_gather_start(idx_smem, 0, src_hbm, buf.at[0], gsem.at[0], n_rows)

        @pl.when(n_steps > 1)
        def _():
            idx_copy(1, 1).start()

    nxt = (i + 1) % 2

    @pl.when(i + 1 < n_steps)
    def _():
        idx_copy(i + 1, nxt).wait()
        _row_gather_start(idx_smem, nxt, src_hbm, buf.at[nxt], gsem.at[nxt], n_rows)

    @pl.when(i + 2 < n_steps)
    def _():
        idx_copy(i + 2, i % 2).start()

    _row_gather_wait(buf.at[i % 2], gsem.at[i % 2])


def _dispatch_kernel(pend_ref, nused_ref, dest_ref, h2_ref, xs_ref, zbuf, tbuf, idx_smem, isem, tsem, csem, zsem, *,
                     tm, n_blocks):
    i = pl.program_id(0)
    n = pl.num_programs(0)
    rows = TOP_K * tm
    groups = tm // ROW_GROUP

    def idx_copy(blk, slot):
        return pltpu.make_async_copy(dest_ref.at[blk, 0], idx_smem.at[pl.ds(_slot_offset(slot, rows), rows)],
                                     isem.at[slot])

    def tile_copy(t, ring):
        return pltpu.make_async_copy(h2_ref.at[pl.ds(t * groups, groups)], tbuf.at[ring], tsem.at[ring])

    def zero_copy(block_start):
        start = pl.multiple_of(block_start, MOE_ROWS)
        return pltpu.make_async_copy(zbuf, xs_ref.at[pl.ds(start, MOE_ROWS)], zsem)

    def rows_done(parity):
        return pltpu.make_async_copy(xs_ref.at[pl.ds(0, rows)], xs_ref.at[pl.ds(0, rows)], csem.at[parity])

    @pl.when(i == 0)
    def _():
        idx_copy(0, 0).start()
        tile_copy(0, 0).start()

        @pl.when(n > 1)
        def _():
            tile_copy(1, 1).start()

        zbuf[...] = jnp.zeros(zbuf.shape, F32)
        n_used = nused_ref[0]

        def last_block(e, carry):
            zero_copy(jnp.maximum(pend_ref[e] - MOE_ROWS, 0)).start()
            return carry
        lax.fori_loop(0, N_EXPERTS, last_block, 0)

        def tail_block(b, carry):
            zero_copy(b * MOE_ROWS).start()
            return carry
        lax.fori_loop(n_used, n_blocks, tail_block, 0)

        def drain(b, carry):
            zero_copy(0).wait()
            return carry
        lax.fori_loop(0, N_EXPERTS + n_blocks - n_used, drain, 0)

    slot = i % 2
    idx_copy(i, slot).wait()

    @pl.when(i + 1 < n)
    def _():
        idx_copy(i + 1, 1 - slot).start()

    base = _slot_offset(slot, rows)
    ring = i % 3
    tile_copy(i, ring).wait()

    def group(g, carry):
        r0 = g * ROW_GROUP
        for j in range(ROW_GROUP):
            src = tbuf.at[ring, g, pl.ds(j, 1)]
            for kk in range(TOP_K):
                dst = xs_ref.at[pl.ds(idx_smem[base + kk * tm + r0 + j], 1)]
                pltpu.make_async_copy(src, dst, csem.at[slot]).start(priority=kk % 2)
        return carry
    lax.fori_loop(0, groups, group, 0)

    @pl.when(i >= 1)
    def _():
        rows_done(1 - slot).wait()

    @pl.when(i + 2 < n)
    def _():
        tile_copy(i + 2, (i + 2) % 3).start()

    @pl.when(i == n - 1)
    def _():
        rows_done(slot).wait()


def _dispatch(pend, n_used, dest_tiles, h2_all, *, tm, n_blocks):
    t_all, d = h2_all.shape
    grid_spec = pltpu.PrefetchScalarGridSpec(
        num_scalar_prefetch=2,
        grid=(t_all // tm,),
        in_specs=[pl.BlockSpec(memory_space=pl.ANY), pl.BlockSpec(memory_space=pl.ANY)],
        out_specs=pl.BlockSpec(memory_space=pl.ANY),
        scratch_shapes=[pltpu.VMEM((MOE_ROWS, d), F32),
                        pltpu.VMEM((3, tm // ROW_GROUP, ROW_GROUP, d), F32),
                        pltpu.SMEM((2 * TOP_K * tm,), jnp.int32),
                        pltpu.SemaphoreType.DMA((2,)),
                        pltpu.SemaphoreType.DMA((3,)),
                        pltpu.SemaphoreType.DMA((2,)),
                        pltpu.SemaphoreType.DMA(())],
    )
    return pl.pallas_call(
        functools.partial(_dispatch_kernel, tm=tm, n_blocks=n_blocks),
        grid_spec=grid_spec,
        out_shape=jax.ShapeDtypeStruct((n_blocks * MOE_ROWS, d), F32),
        compiler_params=_params(("arbitrary",)),
        name="dispatch",
    )(pend, n_used, dest_tiles, h2_all.reshape(t_all // ROW_GROUP, ROW_GROUP, d))


def _moe_kernel(be_ref, nused_ref, nexte_ref, xs_ref, wgu_ref, bgu_ref, wdn_hbm, bdn_ref, y_ref,
                wgu_bf, wdn_bf, wdn_stage, wsem, *, d_model):
    i = pl.program_id(0)
    n_used = nused_ref[0]

    def wdn_copy(e):
        return pltpu.make_async_copy(wdn_hbm.at[e], wdn_stage, wsem)

    @pl.when(i < n_used)
    def _():
        prev = be_ref[jnp.maximum(i - 1, 0)]

        @pl.when((i == 0) | (be_ref[i] != prev))
        def _():
            @pl.when(i == 0)
            def _():
                wdn_copy(be_ref[0]).start()

            wdn_copy(be_ref[i]).wait()
            wgu_bf[...] = wgu_ref[...].astype(BF16)
            wdn_bf[...] = wdn_stage[...].astype(BF16)
            nxt = nexte_ref[i]

            @pl.when(nxt < N_EXPERTS)
            def _():
                wdn_copy(nxt).start()

        xb = xs_ref[...].astype(BF16)
        gu = _dot(xb, wgu_bf[...]) + bgu_ref[...]
        g = jnp.minimum(gu[:, :d_model], SWIGLU_LIMIT)
        lin = jnp.clip(gu[:, d_model:], -SWIGLU_LIMIT, SWIGLU_LIMIT)
        act = g * jax.nn.sigmoid(SWIGLU_ALPHA * g) * (lin + 1.0)
        y_ref[...] = _dot(act.astype(BF16), wdn_bf[...]) + bdn_ref[...]

    @pl.when(i >= n_used)
    def _():
        y_ref[...] = jnp.zeros(y_ref.shape, F32)


def _moe_experts(block_e, n_used, next_e, xs, w_gu, b_gu, w_dn, b_dn):
    n_blocks = xs.shape[0] // MOE_ROWS
    e, d, d2 = w_gu.shape
    grid_spec = pltpu.PrefetchScalarGridSpec(
        num_scalar_prefetch=3,
        grid=(n_blocks,),
        in_specs=[pl.BlockSpec((MOE_ROWS, d), lambda i, be, nu, ne: (jnp.minimum(i, nu[0] - 1), 0)),
                  pl.BlockSpec((None, d, d2), lambda i, be, nu, ne: (be[i], 0, 0)),
                  pl.BlockSpec((None, 1, d2), lambda i, be, nu, ne: (be[i], 0, 0)),
                  pl.BlockSpec(memory_space=pl.ANY),
                  pl.BlockSpec((None, 1, d), lambda i, be, nu, ne: (be[i], 0, 0))],
        out_specs=pl.BlockSpec((MOE_ROWS, d), lambda i, be, nu, ne: (i, 0)),
        scratch_shapes=[pltpu.VMEM((d, d2), BF16),
                        pltpu.VMEM((d, d), BF16),
                        pltpu.VMEM((d, d), F32),
                        pltpu.SemaphoreType.DMA(())],
    )
    return pl.pallas_call(
        functools.partial(_moe_kernel, d_model=d),
        grid_spec=grid_spec,
        out_shape=jax.ShapeDtypeStruct((n_blocks * MOE_ROWS, d), F32),
        compiler_params=_params(("arbitrary",)),
        name="moe_experts",
    )(block_e, n_used, next_e, xs, w_gu, b_gu.reshape(e, 1, d2), w_dn, b_dn.reshape(e, 1, d))


def _combine_kernel(pos_ref, ys_ref, x1_ref, gate_ref, gm_ref, gfin_ref, y_ref, ybuf, idx_smem, isem, gsem, *,
                    tm, first_tile, final_norm):
    i = pl.program_id(0)
    _gather_pipeline(i, pl.num_programs(0), pos_ref, idx_smem, isem, ys_ref, ybuf, gsem, TOP_K * tm, first_tile)
    gate = jnp.transpose(jnp.concatenate([gate_ref[...], jnp.zeros((LANES - ROW_GROUP, tm), F32)], axis=0))
    f = jnp.zeros(x1_ref.shape, F32)
    for kk in range(TOP_K):
        f = f + gate[:, kk:kk + 1] * ybuf[i % 2, kk * tm:(kk + 1) * tm, 0, :]
    x2 = x1_ref[...] + gm_ref[...] * f
    y_ref[...] = _rms(x2, gfin_ref[...]) if final_norm else x2


def _combine(pos_tiles, ys, x1, gate, g_m, final_g, *, tm, first_tile, per_row, tiles_per_batch, final_norm):
    t, d = x1.shape
    row = lambda w: pl.BlockSpec((tm, w), lambda i: (i, 0))
    return pl.pallas_call(
        functools.partial(_combine_kernel, tm=tm, first_tile=first_tile, final_norm=final_norm),
        grid=(t // tm,),
        in_specs=[pl.BlockSpec(memory_space=pl.ANY), pl.BlockSpec(memory_space=pl.ANY), row(d),
                  pl.BlockSpec((ROW_GROUP, tm), lambda i: (0, first_tile + i)),
                  _mod_spec(per_row, tm, d, tiles_per_batch), _const_spec((1, d))],
        out_specs=row(d),
        out_shape=jax.ShapeDtypeStruct((t, d), F32),
        scratch_shapes=[pltpu.VMEM((2, TOP_K * tm, 1, d), F32),
                        pltpu.SMEM((2 * TOP_K * tm,), jnp.int32),
                        pltpu.SemaphoreType.DMA((2,)),
                        pltpu.SemaphoreType.DMA((2,))],
        compiler_params=_params(("arbitrary",)),
        name="combine",
    )(pos_tiles, ys, x1, gate, g_m, final_g)


def _rope_tables(pos):
    inv = ROPE_THETA ** (-jnp.arange(0, QK_ROPE, 2, dtype=F32) / QK_ROPE)
    ang = pos.astype(F32)[:, None] * inv[None, :]
    cos, sin = jnp.cos(ang), jnp.sin(ang)
    n = pos.shape[0]
    cc = jnp.concatenate([cos, cos, jnp.ones((n, LANES - QK_ROPE), F32)], axis=1)
    ss = jnp.concatenate([sin, sin, jnp.zeros((n, LANES - QK_ROPE), F32)], axis=1)
    return cc, ss


def _layer_weights(l, w_in, norm_mix_g, q_norm_g, w_uq, kv_norm_g, w_uk, w_uv, w_pa, sg_norm_g, sg_norm_b,
                   w_pb, w_o, norm_ffn_g, router_w, router_b):
    d = w_in.shape[1]
    wi = w_in[l]
    o_kr = Q_RANK + KV_RANK
    o_u = o_kr + QK_ROPE
    o_v = o_u + SG_WIDTH
    o_ga = o_v + SG_WIDTH
    kr = wi[:, o_kr:o_u]
    zpad = jnp.zeros((d, LANES - QK_ROPE), F32)
    w_in_r = jnp.concatenate([wi[:, :o_kr], wi[:, o_u:o_ga], wi[:, o_ga:], kr, zpad], axis=1).astype(BF16)
    uq = w_uq[l]
    nope, rope = uq[..., :QK_NOPE], uq[..., QK_NOPE:]
    z32 = jnp.zeros(rope.shape[:2] + (HEAD_SLOT - QK_NOPE - QK_ROPE,), F32)
    wq = jnp.concatenate([rope, nope, z32], axis=-1).reshape(Q_RANK, QK_WIDTH).astype(BF16)
    uk = w_uk[l]
    zk_lo = jnp.zeros(uk.shape[:2] + (QK_ROPE,), F32)
    zk_hi = jnp.zeros(uk.shape[:2] + (HEAD_SLOT - QK_NOPE - QK_ROPE,), F32)
    wk = jnp.concatenate([zk_lo, uk, zk_hi], axis=-1).reshape(KV_RANK, QK_WIDTH).astype(BF16)
    wv = w_uv[l].reshape(KV_RANK, V_WIDTH).astype(BF16)
    wv_slot = jnp.concatenate([w_uv[l], jnp.zeros_like(w_uv[l])], axis=-1).reshape(KV_RANK, QK_WIDTH).astype(BF16)
    vone = jnp.tile(jnp.concatenate([jnp.zeros((V_HEAD,), F32), jnp.ones((HEAD_SLOT - V_HEAD,), F32)]),
                    MLA_HEADS).reshape(1, QK_WIDTH)
    ukt = jnp.transpose(uk, (1, 2, 0))
    eye = jnp.broadcast_to(jnp.eye(QK_ROPE, LANES, dtype=F32), (MLA_HEADS, QK_ROPE, LANES))
    top = jnp.concatenate([jnp.zeros((MLA_HEADS, QK_ROPE, KV_RANK), F32), eye], axis=-1)
    mid = jnp.concatenate([ukt, jnp.zeros((MLA_HEADS, QK_NOPE, LANES), F32)], axis=-1)
    bot = jnp.zeros((MLA_HEADS, HEAD_SLOT - QK_NOPE - QK_ROPE, KV_RANK + LANES), F32)
    mabs = jnp.concatenate([top, mid, bot], axis=1).astype(BF16)
    rw = jnp.pad(router_w[l], ((0, 0), (0, ROUTER_PAD - N_EXPERTS)))
    rw_hi = rw.astype(BF16)
    rw_lo = (rw - rw_hi.astype(F32)).astype(BF16)
    rb = jnp.concatenate([router_b[l], jnp.full((ROUTER_PAD - N_EXPERTS,), NEG_BIG, F32)]).reshape(1, ROUTER_PAD)
    return dict(
        w_in_r=w_in_r, gmix=norm_mix_g[l].reshape(1, d),
        gq=(q_norm_g[l] * (ATTN_SCALE * LOG2_E)).reshape(1, Q_RANK),
        gkv=kv_norm_g[l].reshape(1, KV_RANK), wq=wq, wk=wk, wv=wv, wv_slot=wv_slot, vone=vone, mabs=mabs,
        sgg=sg_norm_g[l].reshape(1, SG_WIDTH), sgb=sg_norm_b[l].reshape(1, SG_WIDTH),
        w_pa=w_pa[l].astype(BF16), w_pb=w_pb[l].astype(BF16), w_o=w_o[l].astype(BF16),
        gffn=norm_ffn_g[l].reshape(1, d), rw_hi=rw_hi, rw_lo=rw_lo, rb=rb)


def _spatial_mix_weights(w_s, b_s, seq, n_batch):
    gw = SG_WIDTH // SG_GROUPS
    tril = jnp.tril(jnp.ones((SG_CHUNK, SG_CHUNK), dtype=bool))
    w = jnp.where(tril[None], w_s, 0.0)
    if seq % SG_CHUNK == 0:
        mixw = w
        bias_t = b_s
    else:
        assert seq < SG_CHUNK
        blk = w[:, :seq, :seq]
        pos = jnp.arange(n_batch * seq, dtype=jnp.int32)
        rep = (pos[:, None] % seq == jnp.arange(seq, dtype=jnp.int32)[None, :]).astype(F32)
        tiled = jnp.einsum("rt,gts,cs->grc", rep, blk, rep, precision=lax.Precision.HIGHEST)
        mixw = jnp.where((pos[:, None] // seq == pos[None, :] // seq)[None], tiled, 0.0)
        bias_t = jnp.tile(b_s[:, :seq], (1, n_batch))
    bias = jnp.repeat(jnp.transpose(bias_t), gw, axis=1)
    return mixw.astype(BF16), bias


def _routing_tables(idx, rank, counts_f, n_blocks):
    idx, rank = idx[:TOP_K], rank[:TOP_K]
    counts = counts_f[:, 0].astype(jnp.int32)
    padded = (counts + MOE_ROWS - 1) // MOE_ROWS * MOE_ROWS
    pend = jnp.cumsum(padded).astype(jnp.int32)
    pstart = pend - padded
    experts = jnp.arange(N_EXPERTS, dtype=jnp.int32)
    dest = rank + jnp.sum(jnp.where(idx[..., None] == experts, pstart, 0), axis=-1)
    block_start = jnp.arange(n_blocks, dtype=jnp.int32) * MOE_ROWS
    block_e = jnp.minimum(jnp.sum((pend[None, :] <= block_start[:, None]).astype(jnp.int32), axis=1), N_EXPERTS - 1)
    n_used = (pend[-1:] // MOE_ROWS).astype(jnp.int32)
    nonempty = jnp.where(counts > 0, experts, N_EXPERTS)
    later = lax.cummin(nonempty, axis=0, reverse=True)
    after = jnp.concatenate([later[1:], jnp.full((1,), N_EXPERTS, jnp.int32)])
    next_e = after[block_e]
    return dest.astype(jnp.int32), pend, block_e.astype(jnp.int32), n_used, next_e.astype(jnp.int32)


def _pos_tiles(pos, tm):
    t = pos.shape[1]
    return jnp.transpose(pos.reshape(TOP_K, t // tm, tm), (1, 0, 2)).reshape(t // tm, 1, TOP_K * tm)


def _stack_layers(per_layer):
    return per_layer[0][None] if len(per_layer) == 1 else jnp.stack(per_layer)


def _pick_tile(n, pref):
    t = min(n, pref)
    assert n % t == 0 and t % 8 == 0
    return t


def kernel(x_prompt, x_sample, cache_ckv, cache_krope, c_prompt, c_sample, ada_w, ada_b, norm_mix_g, w_in, q_norm_g, w_uq, kv_norm_g, w_uk, w_uv, w_pa, sg_norm_g, sg_norm_b, w_spatial, b_spatial, w_pb, w_o, norm_ffn_g, router_w, router_b, w_gu, b_gu, w_dn, b_dn, final_g):
    bp, lp, d = x_prompt.shape
    bs, ls, _ = x_sample.shape
    depth = w_in.shape[0]
    past = cache_ckv.shape[2]
    tp, ts = bp * lp, bs * ls
    assert lp % SG_CHUNK == 0 and ls <= SG_CHUNK

    tm_p = _pick_tile(lp, INPROJ_TILE)
    tm_s = _pick_tile(ts, SAMPLE_TILE)
    t_attn = _pick_tile(lp, ATTN_TILE)
    tpb = lp // tm_p

    cc_p, ss_p = _rope_tables(jnp.arange(lp, dtype=jnp.int32))
    cc_s, ss_s = _rope_tables(past + jnp.arange(ls, dtype=jnp.int32))
    cc_s, ss_s = jnp.tile(cc_s, (bs, 1)), jnp.tile(ss_s, (bs, 1))

    b_all = bp + bs
    b_pad = -(-b_all // 8) * 8
    c_all = jnp.concatenate([c_prompt, c_sample, jnp.zeros((b_pad - b_all, d), F32)], axis=0)

    xp = x_prompt.reshape(tp, d)
    xs = x_sample.reshape(ts, d)
    outs = dict(ckv_p=[], kr_p=[], ckv_s=[], kr_s=[], v_s=[])
    final_g2 = final_g.reshape(1, d)
    for l in range(depth):
        wts = _layer_weights(l, w_in, norm_mix_g, q_norm_g, w_uq, kv_norm_g, w_uk, w_uv, w_pa, sg_norm_g,
                             sg_norm_b, w_pb, w_o, norm_ffn_g, router_w, router_b)
        mod = _adaln(c_all, ada_w[l], ada_b[l])
        mod_p = [mod[:bp, j * d:(j + 1) * d].reshape(bp, 1, d) for j in range(6)]
        mod_s = [jnp.repeat(mod[bp:b_all, j * d:(j + 1) * d], ls, axis=0) for j in range(6)]

        q, k, v, ckv, kr, u, vn, ga, gb = _inproj(xp, mod_p[0], mod_p[1], cc_p, ss_p, wts, tm=tm_p, per_row=False,
                                                  tiles_per_batch=tpb, vn_dtype=BF16)
        o = _attn_prompt(q.reshape(bp, lp, QK_WIDTH), k.reshape(bp, lp, QK_WIDTH), v.reshape(bp, lp, QK_WIDTH),
                         tq=t_attn, tk=t_attn).reshape(tp, V_WIDTH)
        mixw, bias = _spatial_mix_weights(w_spatial[l], b_spatial[l], lp, bp)
        tm_m = _pick_tile(lp, MERGE_TILE)
        x1p = _merge(o, u, vn, ga, gb, xp, mod_p[2], mixw, bias, wts, tm=tm_m, chunk=SG_CHUNK, per_row=False,
                     tiles_per_batch=lp // tm_m)
        outs["ckv_p"].append(ckv.reshape(bp, lp, KV_RANK))
        outs["kr_p"].append(jnp.transpose(kr, (0, 2, 1)))

        q, k, v, ckv, kr, u, vn, ga, gb = _inproj(xs, mod_s[0], mod_s[1], cc_s, ss_s, wts, tm=tm_s, per_row=True,
                                                  tiles_per_batch=1, vn_dtype=F32)
        ckv3 = ckv.reshape(bs, ls, KV_RANK)
        kr_rbl = kr.reshape(QK_ROPE, bs, ls)
        kr3 = jnp.transpose(kr_rbl, (1, 2, 0))
        o = _attn_sample(q.reshape(bs, ls, QK_WIDTH), cache_ckv[l], jnp.transpose(cache_krope[l], (0, 2, 1)),
                         ckv3, jnp.transpose(kr_rbl, (1, 0, 2)), wts["mabs"], wts["wv"]).reshape(ts, V_WIDTH)
        mixw, bias = _spatial_mix_weights(w_spatial[l], b_spatial[l], ls, tm_s // ls)
        x1s = _merge(o, u, vn, ga, gb, xs, mod_s[2], mixw, bias, wts, tm=tm_s, chunk=tm_s, per_row=True,
                     tiles_per_batch=1)
        outs["ckv_s"].append(ckv3)
        outs["kr_s"].append(kr3)
        outs["v_s"].append(vn.reshape(bs, ls, SG_WIDTH))

        t_all = tp + ts
        tm_r = _pick_tile(math.gcd(tp, ts), ROW_TILE)
        h2_all, idx, gate, rank, counts = _router(x1p, mod_p[3], mod_p[4], x1s, mod_s[3], mod_s[4], wts, tm=tm_r,
                                                  tiles_per_batch=lp // tm_r)
        n_blocks = -(-(t_all * TOP_K) // MOE_ROWS) + N_EXPERTS
        pos, pend, block_e, n_used, next_e = _routing_tables(idx, rank, counts, n_blocks)
        pos_tiles = _pos_tiles(pos, tm_r)
        x_sorted = _dispatch(pend, n_used, pos_tiles, h2_all, tm=tm_r, n_blocks=n_blocks)
        ys = _moe_experts(block_e, n_used, next_e, x_sorted, w_gu[l], b_gu[l], w_dn[l], b_dn[l])
        last = l == depth - 1
        xp = _combine(pos_tiles, ys, x1p, gate, mod_p[5], final_g2, tm=tm_r, first_tile=0, per_row=False,
                      tiles_per_batch=lp // tm_r, final_norm=last)
        xs = _combine(pos_tiles, ys, x1s, gate, mod_s[5], final_g2, tm=tm_r, first_tile=tp // tm_r, per_row=True,
                      tiles_per_batch=1, final_norm=last)
    return (xp.reshape(bp, lp, d), xs.reshape(bs, ls, d),
            _stack_layers(outs["ckv_p"]), _stack_layers(outs["kr_p"]),
            _stack_layers(outs["ckv_s"]), _stack_layers(outs["kr_s"]), _stack_layers(outs["v_s"]))
```
